```python
import jax
import jax.numpy as jnp
from jax import lax
import numpy as np

D_MODEL = 2048
BATCH = 4
SEQ = 2048
DEPTH = 4

N_MIXERS = 3
N_LAYERS_SG = (DEPTH + 2) // 3
N_LAYERS_RET = (DEPTH + 1) // 3
N_LAYERS_RWKV = DEPTH // 3
N_MOD = 6
NORM_EPS = 1e-6
LN_EPS = 1e-5

SG_CHUNK = 128
SG_WIDTH = D_MODEL
SG_GROUPS = 16
SG_GROUP_DIM = SG_WIDTH // SG_GROUPS

RET_HEADS = 8
RET_QK_DIM = D_MODEL // RET_HEADS
RET_V_DIM = 2 * D_MODEL // RET_HEADS
RET_CHUNK = 128
ROPE_BASE = 10000.0
POS_OFFSET_MAX = 1024

RWKV_HEAD_DIM = 64
RWKV_HEADS = D_MODEL // RWKV_HEAD_DIM
RWKV_LORA_DECAY = max(32, int(round(1.8 * D_MODEL ** 0.5 / 32)) * 32)
RWKV_LORA_AAA = max(32, int(round(1.8 * D_MODEL ** 0.5 / 32)) * 32)
RWKV_LORA_GATE = max(32, int(round(0.6 * D_MODEL ** 0.8 / 32)) * 32)
RWKV_GN_EPS = RWKV_HEAD_DIM * 1e-5
RWKV_DECAY_OFFSET = 0.5

D_FF = int(round(8 * D_MODEL / 3 / 128)) * 128
CONV_WIDTH = 3

kernel_name = 'hybrid_sgmlp_retnet_rwkv7_trunk'


def rms_norm(x):
    xf = x.astype(jnp.float32)
    return (xf * lax.rsqrt(jnp.mean(xf * xf, -1, keepdims=True) + NORM_EPS)).astype(x.dtype)


def group_norm(x, gain, bias, n_groups, eps):
    shp = x.shape
    xf = x.astype(jnp.float32).reshape(shp[:-1] + (n_groups, shp[-1] // n_groups))
    mu = jnp.mean(xf, -1, keepdims=True)
    var = jnp.mean(jnp.square(xf - mu), -1, keepdims=True)
    y = ((xf - mu) * lax.rsqrt(var + eps)).reshape(shp).astype(x.dtype)
    return y * gain + bias


def token_shift(x):
    return jnp.pad(x, ((0, 0), (1, 0), (0, 0)))[:, :-1]


def rotary(x, positions):
    half = x.shape[-1] // 2
    inv_freq = ROPE_BASE ** (-jnp.arange(half, dtype=jnp.float32) / half)
    ang = positions.astype(jnp.float32)[..., None] * inv_freq
    cos = jnp.cos(ang)[:, :, None, :]
    sin = jnp.sin(ang)[:, :, None, :]
    xf = x.astype(jnp.float32)
    x1, x2 = xf[..., :half], xf[..., half:]
    return jnp.concatenate([x1 * cos - x2 * sin, x2 * cos + x1 * sin], -1).astype(x.dtype)


def chunked_spatial_gating(x, w_in, ln_g, ln_b, w_s, b_s, w_out):
    b, s, _ = x.shape
    z = jax.nn.gelu(x @ w_in)
    u, v = jnp.split(z, 2, axis=-1)
    v = group_norm(v, ln_g, ln_b, 1, LN_EPS)
    v = v.reshape(b, s // SG_CHUNK, SG_CHUNK, SG_GROUPS, SG_GROUP_DIM)
    causal = jnp.tril(jnp.ones((SG_CHUNK, SG_CHUNK), dtype=bool))
    w_causal = jnp.where(causal, w_s, 0).astype(v.dtype)
    sv = jnp.einsum('gts,bnsgc->bntgc', w_causal, v) + b_s.T[None, None, :, :, None]
    return (u * sv.reshape(b, s, SG_WIDTH)) @ w_out


def retention(x, positions, w_in, gn_g, gn_b, w_out):
    b, s, d = x.shape
    nc = s // RET_CHUNK
    q, k, v, g = jnp.split(x @ w_in, [d, 2 * d, 4 * d], axis=-1)
    q = rotary(q.reshape(b, s, RET_HEADS, RET_QK_DIM), positions)
    k = rotary(k.reshape(b, s, RET_HEADS, RET_QK_DIM), positions) * (RET_QK_DIM ** -0.5)
    v = v.reshape(b, s, RET_HEADS, RET_V_DIM)

    log_gamma = jnp.log1p(-jnp.exp2(-5.0 - jnp.arange(RET_HEADS, dtype=jnp.float32)))
    idx = jnp.arange(RET_CHUNK, dtype=jnp.float32)
    rel = idx[:, None] - idx[None, :]
    decay_inner = jnp.where(rel >= 0, jnp.exp(log_gamma[:, None, None] * jnp.maximum(rel, 0.0)), 0.0)
    q_decay = jnp.exp(log_gamma[:, None] * (idx + 1.0))[..., None]
    k_decay = jnp.exp(log_gamma[:, None] * (RET_CHUNK - 1.0 - idx))[..., None]
    chunk_decay = jnp.exp(log_gamma * RET_CHUNK)[:, None, None]

    def chunks(t):
        return t.astype(jnp.float32).reshape(b, nc, RET_CHUNK, RET_HEADS, -1).transpose(1, 0, 3, 2, 4)

    def step(state, inp):
        qn, kn, vn = inp
        inner = jnp.einsum('bhtd,bhsd->bhts', qn, kn) * decay_inner
        out = (jnp.einsum('bhts,bhse->bhte', inner, vn)
               + jnp.einsum('bhtd,bhde->bhte', qn * q_decay, state))
        state = state * chunk_decay + jnp.einsum('bhsd,bhse->bhde', kn * k_decay, vn)
        return state, out

    state0 = jnp.zeros((b, RET_HEADS, RET_QK_DIM, RET_V_DIM), jnp.float32)
    _, o = lax.scan(step, state0, (chunks(q), chunks(k), chunks(v)))
    o = o.transpose(1, 0, 3, 2, 4).reshape(b, s, RET_HEADS * RET_V_DIM).astype(x.dtype)
    o = group_norm(o, gn_g, gn_b, RET_HEADS, NORM_EPS)
    return (jax.nn.silu(g) * o) @ w_out


def rwkv7_time_mix(x, mu, w_rkv, w0, w1, w2, a0, a1, a2, g1, g2, k_k, k_a, r_k, ln_g, ln_b, w_out):
    b, s, d = x.shape
    hshape = (b, s, RWKV_HEADS, RWKV_HEAD_DIM)
    xs = x[None] + (token_shift(x) - x)[None] * mu[:, None, None, :]
    rkv = jnp.einsum('pbsd,pde->pbse', xs[:3], w_rkv)
    r, k, v = rkv[0], rkv[1], rkv[2]
    w_log = -jax.nn.softplus(-(w0 + jnp.tanh(xs[3] @ w1) @ w2)) - RWKV_DECAY_OFFSET
    decay = jnp.exp(-jnp.exp(w_log.astype(jnp.float32)))
    a = jax.nn.sigmoid(a0 + (xs[4] @ a1) @ a2)
    g = jax.nn.sigmoid(xs[5] @ g1) @ g2
    kk = (k * k_k).astype(jnp.float32).reshape(hshape)
    kk = kk / jnp.maximum(jnp.sqrt(jnp.sum(kk * kk, -1, keepdims=True)), 1e-12)
    k = k * (1 + (a - 1) * k_a)

    def time_major(t):
        return t.astype(jnp.float32).reshape(hshape).transpose(1, 0, 2, 3)

    def step(state, inp):
        r_t, w_t, k_t, v_t, kk_t, a_t = inp
        sa = jnp.einsum('bhvk,bhk->bhv', state, -kk_t)
        state = (state * w_t[:, :, None, :]
                 + sa[..., None] * (kk_t * a_t)[:, :, None, :]
                 + v_t[..., None] * k_t[:, :, None, :])
        return state, jnp.einsum('bhvk,bhk->bhv', state, r_t)

    state0 = jnp.zeros((b, RWKV_HEADS, RWKV_HEAD_DIM, RWKV_HEAD_DIM), jnp.float32)
    seq_in = (time_major(r), time_major(decay), time_major(k), time_major(v),
              kk.transpose(1, 0, 2, 3), time_major(a))
    _, o = lax.scan(step, state0, seq_in)
    o = o.transpose(1, 0, 2, 3).reshape(b, s, d).astype(x.dtype)
    o = group_norm(o, ln_g, ln_b, RWKV_HEADS, RWKV_GN_EPS)
    bonus = jnp.sum(r.reshape(hshape) * k.reshape(hshape) * r_k, -1, keepdims=True) * v.reshape(hshape)
    return ((o + bonus.reshape(b, s, d)) * g) @ w_out


def conv_gated_ffn(x, w_up, conv_w, conv_b, w_down):
    s = x.shape[1]
    h = x @ w_up
    hp = jnp.pad(h, ((0, 0), (CONV_WIDTH - 1, 0), (0, 0)))
    h = sum(hp[:, j:j + s] * conv_w[j] for j in range(CONV_WIDTH)) + conv_b
    val, gate = jnp.split(h, 2, axis=-1)
    return (jax.nn.silu(gate) * val) @ w_down


def setup_inputs(seed: int = 0) -> dict:
    key = jax.random.key(seed)
    keys = iter(jax.random.split(key, 48))

    def normal(shape, std):
        return std * jax.random.normal(next(keys), shape, jnp.float32)

    def uniform(shape, lo, hi):
        return jax.random.uniform(next(keys), shape, jnp.float32, lo, hi)

    D, F2 = D_MODEL, 2 * D_FF
    nsg, nret, nrw = N_LAYERS_SG, N_LAYERS_RET, N_LAYERS_RWKV
    start = jax.random.randint(next(keys), (BATCH, 1), 0, POS_OFFSET_MAX, dtype=jnp.int32)
    return {
        'x': normal((BATCH, SEQ, D), 1.0),
        'c': normal((BATCH, D), 1.0),
        'positions': start + jnp.arange(SEQ, dtype=jnp.int32)[None, :],
        'ada_w': normal((DEPTH, D, N_MOD * D), 0.5 * D ** -0.5),
        'ada_b': normal((DEPTH, N_MOD * D), 0.02),
        'ffn_w_up': normal((DEPTH, D, F2), D ** -0.5),
        'ffn_conv_w': normal((DEPTH, CONV_WIDTH, F2), CONV_WIDTH ** -0.5),
        'ffn_conv_b': normal((DEPTH, F2), 0.02),
        'ffn_w_down': normal((DEPTH, D_FF, D), D_FF ** -0.5),
        'sg_w_in': normal((nsg, D, 2 * SG_WIDTH), D ** -0.5),
        'sg_ln_g': 1.0 + normal((nsg, SG_WIDTH), 0.02),
        'sg_ln_b': normal((nsg, SG_WIDTH), 0.02),
        'sg_w_s': normal((nsg, SG_GROUPS, SG_CHUNK, SG_CHUNK), SG_CHUNK ** -0.5),
        'sg_b_s': 1.0 + normal((nsg, SG_GROUPS, SG_CHUNK), 0.02),
        'sg_w_out': normal((nsg, SG_WIDTH, D), SG_WIDTH ** -0.5),
        'ret_w_in': normal((nret, D, 6 * D), D ** -0.5),
        'ret_gn_g': 1.0 + normal((nret, 2 * D), 0.02),
        'ret_gn_b': normal((nret, 2 * D), 0.02),
        'ret_w_out': normal((nret, 2 * D, D), (2 * D) ** -0.5),
        'rwkv_mu': uniform((nrw, 6, D), 0.0, 1.0),
        'rwkv_w_rkv': normal((nrw, 3, D, D), D ** -0.5),
        'rwkv_w0': uniform((nrw, D), -4.0, 1.0),
        'rwkv_w1': normal((nrw, D, RWKV_LORA_DECAY), D ** -0.5),
        'rwkv_w2': normal((nrw, RWKV_LORA_DECAY, D), 0.5 * RWKV_LORA_DECAY ** -0.5),
        'rwkv_a0': normal((nrw, D), 0.1),
        'rwkv_a1': normal((nrw, D, RWKV_LORA_AAA), D ** -0.5),
        'rwkv_a2': normal((nrw, RWKV_LORA_AAA, D), 0.5 * RWKV_LORA_AAA ** -0.5),
        'rwkv_g1': normal((nrw, D, RWKV_LORA_GATE), D ** -0.5),
        'rwkv_g2': normal((nrw, RWKV_LORA_GATE, D), RWKV_LORA_GATE ** -0.5),
        'rwkv_k_k': 0.85 + normal((nrw, D), 0.02),
        'rwkv_k_a': 1.0 + normal((nrw, D), 0.02),
        'rwkv_r_k': normal((nrw, RWKV_HEADS, RWKV_HEAD_DIM), 0.1),
        'rwkv_ln_g': 1.0 + normal((nrw, D), 0.02),
        'rwkv_ln_b': normal((nrw, D), 0.02),
        'rwkv_w_out': normal((nrw, D, D), D ** -0.5),
        'final_norm_g': 1.0 + normal((D,), 0.02),
    }


def reference(x, c, positions, ada_w, ada_b, ffn_w_up, ffn_conv_w, ffn_conv_b, ffn_w_down,
              sg_w_in, sg_ln_g, sg_ln_b, sg_w_s, sg_b_s, sg_w_out,
              ret_w_in, ret_gn_g, ret_gn_b, ret_w_out,
              rwkv_mu, rwkv_w_rkv, rwkv_w0, rwkv_w1, rwkv_w2, rwkv_a0, rwkv_a1, rwkv_a2,
              rwkv_g1, rwkv_g2, rwkv_k_k, rwkv_k_a, rwkv_r_k, rwkv_ln_g, rwkv_ln_b, rwkv_w_out,
              final_norm_g):
    cond = jax.nn.silu(c)
    h = x
    for layer in range(DEPTH):
        mod = cond @ ada_w[layer] + ada_b[layer]
        sh_t, sc_t, g_t, sh_c, sc_c, g_c = [m[:, None, :] for m in jnp.split(mod, N_MOD, axis=-1)]
        xm = rms_norm(h) * (1 + sc_t) + sh_t
        kind, j = layer % N_MIXERS, layer // N_MIXERS
        if kind == 0:
            y = chunked_spatial_gating(xm, sg_w_in[j], sg_ln_g[j], sg_ln_b[j], sg_w_s[j], sg_b_s[j], sg_w_out[j])
        elif kind == 1:
            y = retention(xm, positions, ret_w_in[j], ret_gn_g[j], ret_gn_b[j], ret_w_out[j])
        else:
            y = rwkv7_time_mix(xm, rwkv_mu[j], rwkv_w_rkv[j], rwkv_w0[j], rwkv_w1[j], rwkv_w2[j],
                               rwkv_a0[j], rwkv_a1[j], rwkv_a2[j], rwkv_g1[j], rwkv_g2[j],
                               rwkv_k_k[j], rwkv_k_a[j], rwkv_r_k[j], rwkv_ln_g[j], rwkv_ln_b[j], rwkv_w_out[j])
        h = h + g_t * y
        xm = rms_norm(h) * (1 + sc_c) + sh_c
        h = h + g_c * conv_gated_ffn(xm, ffn_w_up[layer], ffn_conv_w[layer], ffn_conv_b[layer], ffn_w_down[layer])
    return rms_norm(h) * final_norm_g
```

```python
import functools

import jax
import jax.numpy as jnp
from jax import lax
from jax.experimental import pallas as pl
from jax.experimental.pallas import tpu as pltpu

F32 = jnp.float32
BF16 = jnp.bfloat16

NORM_EPS = 1e-6
LN_EPS = 1e-5
N_MOD = 6
N_MIXERS = 3

SG_CHUNK = 128
SG_GROUPS = 16

RET_HEADS = 8
RET_CHUNK = 128
ROPE_BASE = 10000.0

RWKV_HEAD_DIM = 64
RWKV_GN_EPS = RWKV_HEAD_DIM * 1e-5
RWKV_DECAY_OFFSET = 0.5
RWKV_CHUNK = 64
RWKV_LANES = 256

CONV_WIDTH = 3

V7X_LANES = 128
V7X_SUBLANES = 8
V7X_VMEM_LIMIT = 56 * 1024 * 1024


def _params(n_axes, vmem=V7X_VMEM_LIMIT):
    return pltpu.CompilerParams(dimension_semantics=("arbitrary",) * n_axes,
                                vmem_limit_bytes=vmem)


def _dot(a, b):
    return jnp.dot(a, b, preferred_element_type=F32)


def _dot_nt(a, b):
    return lax.dot_general(a, b, (((1,), (1,)), ((), ())), preferred_element_type=F32)


def _dot_tn(a, b):
    return lax.dot_general(a, b, (((0,), (0,)), ((), ())), preferred_element_type=F32)


def _rms_mod(h, sc, sh):
    ms = jnp.mean(h * h, axis=-1, keepdims=True)
    return (h * lax.rsqrt(ms + NORM_EPS)) * (1.0 + sc) + sh


def _mod_kernel(c_ref, w_ref, b_ref, o_ref):
    cond = jax.nn.silu(c_ref[...])
    o_ref[...] = _dot(cond.astype(BF16), w_ref[...].astype(BF16)) + b_ref[...]


def _modulation(c, ada_w, ada_b, tn=1024):
    depth, d, n = ada_w.shape
    b = c.shape[0]
    rows = -(-b // V7X_SUBLANES) * V7X_SUBLANES
    c_pad = jnp.pad(c, ((0, rows - b), (0, 0)))
    out = pl.pallas_call(
        _mod_kernel,
        grid=(depth, n // tn),
        in_specs=[
            pl.BlockSpec((rows, d), lambda l, j: (0, 0)),
            pl.BlockSpec((None, d, tn), lambda l, j: (l, 0, j)),
            pl.BlockSpec((None, 1, tn), lambda l, j: (l, 0, j)),
        ],
        out_specs=pl.BlockSpec((None, rows, tn), lambda l, j: (l, 0, j)),
        out_shape=jax.ShapeDtypeStruct((depth, rows, n), F32),
        compiler_params=_params(2),
        name="adaln_mod",
    )(c_pad, ada_w, ada_b.reshape(depth, 1, n))
    return out[:, :b].reshape(depth * b * N_MOD, 1, d)


def _mod_spec(d, layer, k, batch, tiles_per_batch):
    base = layer * batch * N_MOD + k
    return pl.BlockSpec((None, 1, d),
                        lambda i, *_: (base + (i // tiles_per_batch) * N_MOD, 0, 0))


def _norm_matmul_kernel(h_ref, sc_ref, sh_ref, w_ref, *rest, epilogue):
    if epilogue == "rope":
        cos_ref, sin_ref, o_ref, xm_ref = rest
    else:
        o_ref, xm_ref = rest

    @pl.when(pl.program_id(1) == 0)
    def _():
        xm_ref[...] = _rms_mod(h_ref[...], sc_ref[...], sh_ref[...]).astype(BF16)

    y = _dot(xm_ref[...], w_ref[...])
    if epilogue == "gelu":
        y = jax.nn.gelu(y)
    elif epilogue == "rope":
        cos, sin = cos_ref[...], sin_ref[...]
        half = cos.shape[-1]
        parts = []
        for hd in range(y.shape[-1] // (2 * half)):
            x1 = y[:, 2 * hd * half:(2 * hd + 1) * half]
            x2 = y[:, (2 * hd + 1) * half:(2 * hd + 2) * half]
            parts += [x1 * cos - x2 * sin, x2 * cos + x1 * sin]
        y = jnp.concatenate(parts, axis=-1)
    o_ref[...] = y.astype(o_ref.dtype)


def _norm_matmul(h, mod, layer, k_scale, k_shift, w, batch, *, epilogue="none",
                 out_dtype=F32, cos=None, sin=None, tm=512, tn=512):
    t, d = h.shape
    n = w.shape[1]
    tm = min(tm, t // batch)
    tpb = (t // batch) // tm
    in_specs = [
        pl.BlockSpec((tm, d), lambda i, j: (i, 0)),
        _mod_spec(d, layer, k_scale, batch, tpb),
        _mod_spec(d, layer, k_shift, batch, tpb),
        pl.BlockSpec((d, tn), lambda i, j: (0, j)),
    ]
    args = [h, mod, mod, w]
    if epilogue == "rope":
        half = cos.shape[-1]
        in_specs += [pl.BlockSpec((tm, half), lambda i, j: (i, 0))] * 2
        args += [cos, sin]
    return pl.pallas_call(
        functools.partial(_norm_matmul_kernel, epilogue=epilogue),
        grid=(t // tm, n // tn),
        in_specs=in_specs,
        out_specs=pl.BlockSpec((tm, tn), lambda i, j: (i, j)),
        out_shape=jax.ShapeDtypeStruct((t, n), out_dtype),
        scratch_shapes=[pltpu.VMEM((tm, d), BF16)],
        compiler_params=_params(2),
        name="norm_matmul_" + epilogue,
    )(*args)


def _proj_residual_kernel(a_ref, w_ref, h_ref, g_ref, o_ref):
    o_ref[...] = h_ref[...] + g_ref[...] * _dot(a_ref[...], w_ref[...])


def _proj_residual(a, w, h, mod, layer, k_gate, batch, tm=512, tn=512):
    t, kdim = a.shape
    d = w.shape[1]
    tm = min(tm, t // batch)
    tpb = (t // batch) // tm
    base = layer * batch * N_MOD + k_gate
    return pl.pallas_call(
        _proj_residual_kernel,
        grid=(t // tm, d // tn),
        in_specs=[
            pl.BlockSpec((tm, kdim), lambda i, j: (i, 0)),
            pl.BlockSpec((kdim, tn), lambda i, j: (0, j)),
            pl.BlockSpec((tm, tn), lambda i, j: (i, j)),
            pl.BlockSpec((None, 1, tn), lambda i, j: (base + (i // tpb) * N_MOD, 0, j)),
        ],
        out_specs=pl.BlockSpec((tm, tn), lambda i, j: (i, j)),
        out_shape=jax.ShapeDtypeStruct((t, d), F32),
        compiler_params=_params(2),
        name="proj_residual",
    )(a, w, h, mod)


def _sg_core_kernel(u_ref, v_ref, h_ref, g_ref, lng_ref, lnb_ref, ws_ref, bs_ref, wo_ref,
                    o_ref, gated_ref):
    tm, width = v_ref.shape
    gdim = width // SG_GROUPS
    v = v_ref[...]
    mu = jnp.mean(v, axis=-1, keepdims=True)
    var = jnp.mean(jnp.square(v - mu), axis=-1, keepdims=True)
    vn = (((v - mu) * lax.rsqrt(var + LN_EPS)) * lng_ref[...] + lnb_ref[...]).astype(BF16)
    row = lax.broadcasted_iota(jnp.int32, (SG_CHUNK, SG_CHUNK), 0)
    col = lax.broadcasted_iota(jnp.int32, (SG_CHUNK, SG_CHUNK), 1)
    causal = row >= col
    for g in range(SG_GROUPS):
        w_causal = jnp.where(causal, ws_ref[g], 0.0).astype(BF16)
        bias = bs_ref[:, g:g + 1]
        cols = slice(g * gdim, (g + 1) * gdim)
        for c in range(tm // SG_CHUNK):
            rows = slice(c * SG_CHUNK, (c + 1) * SG_CHUNK)
            sv = _dot(w_causal, vn[rows, cols]) + bias
            gated_ref[rows, cols] = (u_ref[rows, cols] * sv).astype(BF16)
    o_ref[...] = h_ref[...] + g_ref[...] * _dot(gated_ref[...], wo_ref[...])


def _sg_core(z, h, mod, layer, batch, ln_g, ln_b, w_s, b_s, w_out, tm=256):
    t, d = h.shape
    width = z.shape[1] // 2
    tm = min(tm, t // batch)
    tpb = (t // batch) // tm
    return pl.pallas_call(
        _sg_core_kernel,
        grid=(t // tm,),
        in_specs=[
            pl.BlockSpec((tm, width), lambda i: (i, 0)),
            pl.BlockSpec((tm, width), lambda i: (i, 1)),
            pl.BlockSpec((tm, d), lambda i: (i, 0)),
            _mod_spec(d, layer, 2, batch, tpb),
            pl.BlockSpec((1, width), lambda i: (0, 0)),
            pl.BlockSpec((1, width), lambda i: (0, 0)),
            pl.BlockSpec((SG_GROUPS, SG_CHUNK, SG_CHUNK), lambda i: (0, 0, 0)),
            pl.BlockSpec((SG_CHUNK, SG_GROUPS), lambda i: (0, 0)),
            pl.BlockSpec((width, d), lambda i: (0, 0)),
        ],
        out_specs=pl.BlockSpec((tm, d), lambda i: (i, 0)),
        out_shape=jax.ShapeDtypeStruct((t, d), F32),
        scratch_shapes=[pltpu.VMEM((tm, width), BF16)],
        compiler_params=_params(1),
        name="sg_core",
    )(z, z, h, mod, ln_g.reshape(1, width), ln_b.reshape(1, width), w_s, b_s.T, w_out)


def _ffn_kernel(h_ref, sc_ref, sh_ref, g_ref, wv_ref, wg_ref, cwv_ref, cwg_ref, cbv_ref, cbg_ref,
                wd_ref, *rest, tiles_per_batch, final_norm):
    if final_norm:
        fg_ref, o_ref, xm_ref, hbuf_ref, halo_ref = rest
    else:
        o_ref, xm_ref, hbuf_ref, halo_ref = rest
    i, j = pl.program_id(0), pl.program_id(1)
    tm = h_ref.shape[0]
    tf = wv_ref.shape[1]
    pad = V7X_SUBLANES

    @pl.when(j == 0)
    def _():
        xm_ref[...] = _rms_mod(h_ref[...], sc_ref[...], sh_ref[...]).astype(BF16)
        o_ref[...] = jnp.zeros_like(o_ref)

    seq_start = (i % tiles_per_batch) == 0
    hbuf_ref[0:pad, :] = jnp.where(seq_start, 0.0, halo_ref[j])
    xm = xm_ref[...]
    hbuf_ref[pad:, 0:tf] = _dot(xm, wv_ref[...])
    hbuf_ref[pad:, tf:] = _dot(xm, wg_ref[...])
    halo_ref[j] = hbuf_ref[tm:, :]

    def conv(cols, cw_ref, cb_ref):
        acc = hbuf_ref[pad - 2:pad - 2 + tm, cols] * cw_ref[0:1, :]
        acc = acc + hbuf_ref[pad - 1:pad - 1 + tm, cols] * cw_ref[1:2, :]
        acc = acc + hbuf_ref[pad:pad + tm, cols] * cw_ref[2:3, :]
        return acc + cb_ref[...]

    val = conv(slice(0, tf), cwv_ref, cbv_ref)
    gate = conv(slice(tf, 2 * tf), cwg_ref, cbg_ref)
    act = (jax.nn.silu(gate) * val).astype(BF16)
    o_ref[...] += _dot(act, wd_ref[...])

    @pl.when(j == pl.num_programs(1) - 1)
    def _():
        out = h_ref[...] + g_ref[...] * o_ref[...]
        if final_norm:
            ms = jnp.mean(out * out, axis=-1, keepdims=True)
            out = (out * lax.rsqrt(ms + NORM_EPS)) * fg_ref[...]
        o_ref[...] = out


def _conv_ffn(h, mod, layer, batch, w_up, conv_w, conv_b, w_down, final_gain=None, tm=512, tf=512):
    t, d = h.shape
    fp = w_down.shape[0]
    nf = fp // tf
    tm = min(tm, t // batch)
    tpb = (t // batch) // tm
    final_norm = final_gain is not None
    in_specs = [
        pl.BlockSpec((tm, d), lambda i, j: (i, 0)),
        _mod_spec(d, layer, 4, batch, tpb),
        _mod_spec(d, layer, 3, batch, tpb),
        _mod_spec(d, layer, 5, batch, tpb),
        pl.BlockSpec((d, tf), lambda i, j: (0, j)),
        pl.BlockSpec((d, tf), lambda i, j: (0, nf + j)),
        pl.BlockSpec((CONV_WIDTH, tf), lambda i, j: (0, j)),
        pl.BlockSpec((CONV_WIDTH, tf), lambda i, j: (0, nf + j)),
        pl.BlockSpec((1, tf), lambda i, j: (0, j)),
        pl.BlockSpec((1, tf), lambda i, j: (0, nf + j)),
        pl.BlockSpec((tf, d), lambda i, j: (j, 0)),
    ]
    args = [h, mod, mod, mod, w_up, w_up, conv_w, conv_w, conv_b, conv_b, w_down]
    if final_norm:
        in_specs.append(pl.BlockSpec((1, d), lambda i, j: (0, 0)))
        args.append(final_gain.reshape(1, d))
    return pl.pallas_call(
        functools.partial(_ffn_kernel, tiles_per_batch=tpb, final_norm=final_norm),
        grid=(t // tm, nf),
        in_specs=in_specs,
        out_specs=pl.BlockSpec((tm, d), lambda i, j: (i, 0)),
        out_shape=jax.ShapeDtypeStruct((t, d), F32),
        scratch_shapes=[
            pltpu.VMEM((tm, d), BF16),
            pltpu.VMEM((tm + V7X_SUBLANES, 2 * tf), F32),
            pltpu.VMEM((nf, V7X_SUBLANES, 2 * tf), F32),
        ],
        compiler_params=_params(2),
        name="conv_ffn",
    )(*args)


def _rope_table_kernel(pos_ref, freq_ref, cos_ref, sin_ref):
    ang = pos_ref[...].astype(F32) * freq_ref[...]
    cos_ref[...] = jnp.cos(ang)
    sin_ref[...] = jnp.sin(ang)


def _rope_tables(positions, half, tm=1024):
    t = positions.size
    inv_freq = ROPE_BASE ** (-jnp.arange(half, dtype=F32) / half)
    tm = min(tm, t)
    return pl.pallas_call(
        _rope_table_kernel,
        grid=(t // tm,),
        in_specs=[pl.BlockSpec((tm, 1), lambda i: (i, 0)),
                  pl.BlockSpec((1, half), lambda i: (0, 0))],
        out_specs=[pl.BlockSpec((tm, half), lambda i: (i, 0))] * 2,
        out_shape=[jax.ShapeDtypeStruct((t, half), F32)] * 2,
        compiler_params=_params(1),
        name="rope_tables",
    )(positions.reshape(t, 1), inv_freq.reshape(1, half))


def _ret_core_kernel(q_ref, k_ref, v_ref, g_ref, di_ref, qd_ref, kd_ref, cd_ref, gg_ref, gb_ref,
                     o_ref, state_ref, *, k_scale):
    state_ref[...] = jnp.zeros_like(state_ref)
    decay_inner = di_ref[...]
    q_decay, k_decay, chunk_decay = qd_ref[...], kd_ref[...], cd_ref[...]
    gain, bias = gg_ref[...], gb_ref[...]

    def body(c, carry):
        rows = pl.ds(pl.multiple_of(c * RET_CHUNK, RET_CHUNK), RET_CHUNK)
        q = q_ref[rows, :]
        k = k_ref[rows, :] * k_scale
        v = v_ref[rows, :]
        inner = _dot_nt(q.astype(BF16), k.astype(BF16)) * decay_inner
        state = state_ref[...]
        out = _dot(inner.astype(BF16), v) + _dot((q * q_decay).astype(BF16), state.astype(BF16))
        state_ref[...] = state * chunk_decay + _dot_tn((k * k_decay).astype(BF16), v)
        mu = jnp.mean(out, axis=-1, keepdims=True)
        var = jnp.mean(jnp.square(out - mu), axis=-1, keepdims=True)
        y = ((out - mu) * lax.rsqrt(var + NORM_EPS)) * gain + bias
        o_ref[rows, :] = (jax.nn.silu(g_ref[rows, :]) * y).astype(BF16)
        return carry

    lax.fori_loop(0, q_ref.shape[0] // RET_CHUNK, body, 0)


def _ret_core(qk, v, g, gn_g, gn_b, batch):
    t, two_d = qk.shape
    d = two_d // 2
    s = t // batch
    dk = d // RET_HEADS
    dv = v.shape[1] // RET_HEADS
    log_gamma = jnp.log1p(-jnp.exp2(-5.0 - jnp.arange(RET_HEADS, dtype=F32)))
    idx = jnp.arange(RET_CHUNK, dtype=F32)
    rel = idx[:, None] - idx[None, :]
    decay_inner = jnp.where(rel >= 0, jnp.exp(log_gamma[:, None, None] * jnp.maximum(rel, 0.0)), 0.0)
    q_decay = jnp.exp(log_gamma[:, None] * (idx + 1.0))[..., None]
    k_decay = jnp.exp(log_gamma[:, None] * (RET_CHUNK - 1.0 - idx))[..., None]
    chunk_decay = jnp.exp(log_gamma * RET_CHUNK)[:, None, None]
    return pl.pallas_call(
        functools.partial(_ret_core_kernel, k_scale=dk ** -0.5),
        grid=(batch, RET_HEADS),
        in_specs=[
            pl.BlockSpec((s, dk), lambda b, h: (b, h)),
            pl.BlockSpec((s, dk), lambda b, h: (b, RET_HEADS + h)),
            pl.BlockSpec((s, dv), lambda b, h: (b, h)),
            pl.BlockSpec((s, dv), lambda b, h: (b, h)),
            pl.BlockSpec((None, RET_CHUNK, RET_CHUNK), lambda b, h: (h, 0, 0)),
            pl.BlockSpec((None, RET_CHUNK, 1), lambda b, h: (h, 0, 0)),
            pl.BlockSpec((None, RET_CHUNK, 1), lambda b, h: (h, 0, 0)),
            pl.BlockSpec((None, 1, 1), lambda b, h: (h, 0, 0)),
            pl.BlockSpec((1, dv), lambda b, h: (0, h)),
            pl.BlockSpec((1, dv), lambda b, h: (0, h)),
        ],
        out_specs=pl.BlockSpec((s, dv), lambda b, h: (b, h)),
        out_shape=jax.ShapeDtypeStruct((t, v.shape[1]), BF16),
        scratch_shapes=[pltpu.VMEM((dk, dv), F32)],
        compiler_params=_params(2),
        name="ret_core",
    )(qk, qk, v, g, decay_inner, q_decay, k_decay, chunk_decay,
      gn_g.reshape(1, -1), gn_b.reshape(1, -1))


def _shifted_inputs(h_ref, hprev_ref, sc_ref, sh_ref, seq_start):
    sc, sh = sc_ref[...], sh_ref[...]
    xm = _rms_mod(h_ref[...], sc, sh)
    prev = _rms_mod(hprev_ref[...], sc, sh)
    last = prev[V7X_SUBLANES - 1:V7X_SUBLANES, :]
    last = jnp.where(seq_start, 0.0, last)
    row = lax.broadcasted_iota(jnp.int32, xm.shape, 0)
    shifted = jnp.where(row == 0, last, pltpu.roll(xm, 1, 0))
    return xm, shifted - xm


def _rwkv_rkv_kernel(h_ref, hprev_ref, sc_ref, sh_ref, mu_ref, w_ref, o_ref, xm_ref, dx_ref, xs_ref,
                     *, tiles_per_batch, tiles_per_proj):
    i, j = pl.program_id(0), pl.program_id(1)

    @pl.when(j == 0)
    def _():
        xm, dx = _shifted_inputs(h_ref, hprev_ref, sc_ref, sh_ref, (i % tiles_per_batch) == 0)
        xm_ref[...] = xm
        dx_ref[...] = dx

    @pl.when(j % tiles_per_proj == 0)
    def _():
        xs_ref[...] = (xm_ref[...] + dx_ref[...] * mu_ref[...]).astype(BF16)

    o_ref[...] = _dot(xs_ref[...], w_ref[...])


def _prev_rows_spec(tm, d):
    blocks = tm // V7X_SUBLANES
    return pl.BlockSpec((V7X_SUBLANES, d), lambda i, *_: (jnp.maximum(i * blocks - 1, 0), 0))


def _rwkv_rkv(h, mod, layer, batch, mu, w_rkv, tm=512, tn=512):
    t, d = h.shape
    tm = min(tm, t // batch)
    tpb = (t // batch) // tm
    tpp = d // tn
    return pl.pallas_call(
        functools.partial(_rwkv_rkv_kernel, tiles_per_batch=tpb, tiles_per_proj=tpp),
        grid=(t // tm, 3 * tpp),
        in_specs=[
            pl.BlockSpec((tm, d), lambda i, j: (i, 0)),
            _prev_rows_spec(tm, d),
            _mod_spec(d, layer, 1, batch, tpb),
            _mod_spec(d, layer, 0, batch, tpb),
            pl.BlockSpec((None, 1, d), lambda i, j: (j // tpp, 0, 0)),
            pl.BlockSpec((None, d, tn), lambda i, j: (j // tpp, 0, j % tpp)),
        ],
        out_specs=pl.BlockSpec((tm, tn), lambda i, j: (i, j)),
        out_shape=jax.ShapeDtypeStruct((t, 3 * d), F32),
        scratch_shapes=[pltpu.VMEM((tm, d), F32), pltpu.VMEM((tm, d), F32), pltpu.VMEM((tm, d), BF16)],
        compiler_params=_params(2),
        name="rwkv_rkv",
    )(h, h, mod, mod, mu.reshape(-1, 1, d), w_rkv)


def _rwkv_lora_kernel(h_ref, hprev_ref, sc_ref, sh_ref, mu_ref, w0_ref, w1_ref, w2_ref,
                      a0_ref, a1_ref, a2_ref, g1_ref, g2_ref, lw_ref, a_ref, g_ref, *, tiles_per_batch):
    i = pl.program_id(0)
    xm, dx = _shifted_inputs(h_ref, hprev_ref, sc_ref, sh_ref, (i % tiles_per_batch) == 0)

    def mixed(p):
        return (xm + dx * mu_ref[p]).astype(BF16)

    lora_w = _dot(jnp.tanh(_dot(mixed(3), w1_ref[...])).astype(BF16), w2_ref[...])
    w_log = -jax.nn.softplus(-(w0_ref[...] + lora_w)) - RWKV_DECAY_OFFSET
    lw_ref[...] = -jnp.exp(w_log)
    lora_a = _dot(_dot(mixed(4), a1_ref[...]).astype(BF16), a2_ref[...])
    a_ref[...] = jax.nn.sigmoid(a0_ref[...] + lora_a)
    g_ref[...] = _dot(jax.nn.sigmoid(_dot(mixed(5), g1_ref[...])).astype(BF16), g2_ref[...])


def _rwkv_lora(h, mod, layer, batch, mu, w0, w1, w2, a0, a1, a2, g1, g2, tm=256):
    t, d = h.shape
    tm = min(tm, t // batch)
    tpb = (t // batch) // tm
    full = lambda a: pl.BlockSpec(a.shape, lambda i: (0,) * a.ndim)
    mu3 = mu.reshape(-1, 1, d)
    w0, a0 = w0.reshape(1, d), a0.reshape(1, d)
    consts = [mu3, w0, w1, w2, a0, a1, a2, g1, g2]
    return pl.pallas_call(
        functools.partial(_rwkv_lora_kernel, tiles_per_batch=tpb),
        grid=(t // tm,),
        in_specs=[
            pl.BlockSpec((tm, d), lambda i: (i, 0)),
            _prev_rows_spec(tm, d),
            _mod_spec(d, layer, 1, batch, tpb),
            _mod_spec(d, layer, 0, batch, tpb),
        ] + [full(a) for a in consts],
        out_specs=[pl.BlockSpec((tm, d), lambda i: (i, 0))] * 3,
        out_shape=[jax.ShapeDtypeStruct((t, d), F32)] * 3,
        compiler_params=_params(1),
        name="rwkv_lora",
    )(h, h, mod, mod, *consts)


def _split_dot(x, ones, terms):
    acc = None
    for _ in range(terms):
        piece = x.astype(BF16)
        part = _dot(piece, ones)
        acc = part if acc is None else acc + part
        x = x - piece.astype(F32)
    return acc


def _rwkv_core_kernel(r_ref, k_ref, v_ref, lw_ref, a_ref, g_ref, kk_ref, ka_ref, rk_ref,
                      lng_ref, lnb_ref, o_ref, state_ref):
    c_len, w = RWKV_CHUNK, RWKV_LANES
    heads = w // RWKV_HEAD_DIM
    n = heads * c_len

    @pl.when(pl.program_id(2) == 0)
    def _():
        state_ref[...] = jnp.zeros_like(state_ref)

    rb = lax.broadcasted_iota(jnp.int32, (n, w), 0)
    lb = lax.broadcasted_iota(jnp.int32, (n, w), 1)
    head_match = (rb // c_len) == (lb // RWKV_HEAD_DIM)
    ones_bd = jnp.where((rb // RWKV_HEAD_DIM) == (lb // RWKV_HEAD_DIM), 1.0, 0.0).astype(BF16)
    tr = lax.broadcasted_iota(jnp.int32, (c_len, n), 0)
    ts = lax.broadcasted_iota(jnp.int32, (c_len, n), 1) % c_len
    strict = tr > ts
    incl = tr >= ts
    eye = jnp.where(tr == ts, 1.0, 0.0)
    cr = lax.broadcasted_iota(jnp.int32, (c_len, c_len), 0)
    cc = lax.broadcasted_iota(jnp.int32, (c_len, c_len), 1)
    tril_ones = jnp.where(cr >= cc, 1.0, 0.0).astype(BF16)
    k_k, k_a, r_k = kk_ref[...], ka_ref[...], rk_ref[...]
    ln_g, ln_b = lng_ref[...], lnb_ref[...]

    def expand(x):
        xb = x.astype(BF16)
        return jnp.where(head_match, jnp.concatenate([xb] * heads, axis=0), jnp.zeros((), BF16))

    def seg_sum(x):
        return _split_dot(x, ones_bd, 2)

    def body(c, carry):
        rows = pl.ds(pl.multiple_of(c * c_len, c_len), c_len)
        r, k, v = r_ref[rows, :], k_ref[rows, :], v_ref[rows, :]
        lw, a, g = lw_ref[rows, :], a_ref[rows, :], g_ref[rows, :]

        kk = k * k_k
        kk = kk / jnp.maximum(jnp.sqrt(seg_sum(kk * kk)), 1e-12)
        kp = k * (1.0 + (a - 1.0) * k_a)
        bonus = seg_sum(r * kp * r_k) * v

        cum = _dot(tril_ones, lw.astype(BF16))
        rem = lw - lw.astype(BF16).astype(F32)
        cum = cum + _dot(tril_ones, rem.astype(BF16))
        rem = rem - rem.astype(BF16).astype(F32)
        cum = cum + _dot(tril_ones, rem.astype(BF16))
        cum_end = cum[c_len - 1:c_len, :]
        e_pos, e_neg = jnp.exp(cum), jnp.exp(-cum)
        e_tail = jnp.exp(cum_end - cum)
        ba = kk * a
        r_t = (r * e_pos).astype(BF16)
        a_t = (-kk * jnp.exp(cum - lw)).astype(BF16)
        b_t, k_t = ba * e_neg, kp * e_neg
        b_h, k_h = ba * e_tail, kp * e_tail
        vb = v.astype(BF16)

        gram = _dot_nt(jnp.concatenate([a_t, r_t], axis=0),
                       jnp.concatenate([expand(b_t), expand(k_t)], axis=0))
        ab = jnp.where(strict, gram[:c_len, :n], 0.0)
        ak = jnp.where(strict, gram[:c_len, n:], 0.0)
        rbm = jnp.where(incl, gram[c_len:, :n], 0.0)
        rkm = jnp.where(incl, gram[c_len:, n:], 0.0)

        tinv = eye + ab
        p = _dot(ab.astype(BF16), expand(ab))
        for _ in range(c_len.bit_length() - 3):
            both = _dot(jnp.concatenate([p, tinv], axis=0).astype(BF16), expand(p))
            p, tinv = both[:c_len], tinv + both[c_len:]
        tinv = tinv + _dot(tinv.astype(BF16), expand(p))
        tb = tinv.astype(BF16)

        ev = expand(v)
        akv_rkv = _dot(jnp.concatenate([ak, rkm], axis=0).astype(BF16), ev)
        wv = _dot(tb, expand(akv_rkv[:c_len]))
        rkv = akv_rkv[c_len:]

        state = state_ref[...]
        sb = state.astype(BF16)
        ag_rg = _dot_nt(jnp.concatenate([a_t, r_t], axis=0), sb)
        u = _dot(tb, expand(ag_rg[:c_len])) + wv
        o = ag_rg[c_len:] + _dot(rbm.astype(BF16), expand(u)) + rkv
        upd = _dot_tn(jnp.concatenate([u.astype(BF16), vb], axis=0),
                      jnp.concatenate([b_h, k_h], axis=0).astype(BF16))
        state_ref[...] = state * jnp.exp(cum_end) + jnp.where(
            (rb // RWKV_HEAD_DIM) == (lb // RWKV_HEAD_DIM), upd, 0.0)

        mean = seg_sum(o) * (1.0 / RWKV_HEAD_DIM)
        cen = o - mean
        var = seg_sum(cen * cen) * (1.0 / RWKV_HEAD_DIM)
        y = (cen * lax.rsqrt(var + RWKV_GN_EPS)) * ln_g + ln_b
        o_ref[rows, :] = ((y + bonus) * g).astype(BF16)
        return carry

    lax.fori_loop(0, r_ref.shape[0] // c_len, body, 0)


def _rwkv_core(rkv, lw, a, g, k_k, k_a, r_k, ln_g, ln_b, batch, rows=512):
    t, d = lw.shape
    s = t // batch
    rows = min(rows, s)
    w = RWKV_LANES
    nq = d // w
    spb = s // rows
    seq = lambda off: pl.BlockSpec((rows, w), lambda b, q, i: (b * spb + i, off + q))
    par = pl.BlockSpec((1, w), lambda b, q, i: (0, q))
    vec = lambda x: x.reshape(1, d)
    return pl.pallas_call(
        _rwkv_core_kernel,
        grid=(batch, nq, spb),
        in_specs=[seq(0), seq(nq), seq(2 * nq), seq(0), seq(0), seq(0), par, par, par, par, par],
        out_specs=seq(0),
        out_shape=jax.ShapeDtypeStruct((t, d), BF16),
        scratch_shapes=[pltpu.VMEM((w, w), F32)],
        compiler_params=_params(3),
        name="rwkv_core",
    )(rkv, rkv, rkv, lw, a, g, vec(k_k), vec(k_a), vec(r_k), vec(ln_g), vec(ln_b))


def _pad_cols(x, n):
    return jnp.pad(x, [(0, 0)] * (x.ndim - 1) + [(0, n - x.shape[-1])])


def _pad_rows(x, n):
    return jnp.pad(x, [(0, n - x.shape[0])] + [(0, 0)] * (x.ndim - 1))


def _round_up(x, m):
    return -(-x // m) * m


def kernel(x, c, positions, ada_w, ada_b, ffn_w_up, ffn_conv_w, ffn_conv_b, ffn_w_down, sg_w_in, sg_ln_g, sg_ln_b, sg_w_s, sg_b_s, sg_w_out, ret_w_in, ret_gn_g, ret_gn_b, ret_w_out, rwkv_mu, rwkv_w_rkv, rwkv_w0, rwkv_w1, rwkv_w2, rwkv_a0, rwkv_a1, rwkv_a2, rwkv_g1, rwkv_g2, rwkv_k_k, rwkv_k_a, rwkv_r_k, rwkv_ln_g, rwkv_ln_b, rwkv_w_out, final_norm_g):
    batch, seq, d = x.shape
    depth = ada_w.shape[0]
    t = batch * seq
    d_ff = ffn_w_down.shape[1]
    ffn_tile = 512
    fp = _round_up(d_ff, ffn_tile)

    mod = _modulation(c, ada_w, ada_b)
    h = x.reshape(t, d)
    cos = sin = None

    for layer in range(depth):
        kind, j = layer % N_MIXERS, layer // N_MIXERS
        if kind == 0:
            z = _norm_matmul(h, mod, layer, 1, 0, sg_w_in[j].astype(BF16), batch, epilogue="gelu")
            h = _sg_core(z, h, mod, layer, batch, sg_ln_g[j], sg_ln_b[j], sg_w_s[j], sg_b_s[j],
                         sg_w_out[j].astype(BF16))
        elif kind == 1:
            if cos is None:
                cos, sin = _rope_tables(positions, d // RET_HEADS // 2)
            w_in = ret_w_in[j].astype(BF16)
            qk = _norm_matmul(h, mod, layer, 1, 0, w_in[:, :2 * d], batch, epilogue="rope", cos=cos, sin=sin)
            v = _norm_matmul(h, mod, layer, 1, 0, w_in[:, 2 * d:4 * d], batch, out_dtype=BF16)
            g = _norm_matmul(h, mod, layer, 1, 0, w_in[:, 4 * d:], batch)
            o = _ret_core(qk, v, g, ret_gn_g[j], ret_gn_b[j], batch)
            h = _proj_residual(o, ret_w_out[j].astype(BF16), h, mod, layer, 2, batch)
        else:
            rkv = _rwkv_rkv(h, mod, layer, batch, rwkv_mu[j], rwkv_w_rkv[j].astype(BF16))
            lora = _round_up(rwkv_w1.shape[-1], V7X_LANES)
            lora_a = _round_up(rwkv_a1.shape[-1], V7X_LANES)
            lw, a, g = _rwkv_lora(
                h, mod, layer, batch, rwkv_mu[j], rwkv_w0[j],
                _pad_cols(rwkv_w1[j], lora).astype(BF16), _pad_rows(rwkv_w2[j], lora).astype(BF16),
                rwkv_a0[j],
                _pad_cols(rwkv_a1[j], lora_a).astype(BF16), _pad_rows(rwkv_a2[j], lora_a).astype(BF16),
                rwkv_g1[j].astype(BF16), rwkv_g2[j].astype(BF16))
            o = _rwkv_core(rkv, lw, a, g, rwkv_k_k[j], rwkv_k_a[j], rwkv_r_k[j],
                           rwkv_ln_g[j], rwkv_ln_b[j], batch)
            h = _proj_residual(o, rwkv_w_out[j].astype(BF16), h, mod, layer, 2, batch)

        w_up = ffn_w_up[layer]
        w_up = jnp.concatenate([_pad_cols(w_up[:, :d_ff], fp), _pad_cols(w_up[:, d_ff:], fp)], axis=1)
        cw = ffn_conv_w[layer]
        cw = jnp.concatenate([_pad_cols(cw[:, :d_ff], fp), _pad_cols(cw[:, d_ff:], fp)], axis=1)
        cb = ffn_conv_b[layer]
        cb = jnp.concatenate([_pad_cols(cb[:d_ff], fp), _pad_cols(cb[d_ff:], fp)]).reshape(1, 2 * fp)
        h = _conv_ffn(h, mod, layer, batch, w_up.astype(BF16), cw, cb,
                      _pad_rows(ffn_w_down[layer], fp).astype(BF16),
                      final_gain=final_norm_g if layer == depth - 1 else None, tf=ffn_tile)

    return h.reshape(batch, seq, d)
```

```python
import functools

import jax
import jax.numpy as jnp
from jax import lax
from jax.experimental import pallas as pl
from jax.experimental.pallas import tpu as pltpu

F32 = jnp.float32
BF16 = jnp.bfloat16

NORM_EPS = 1e-6
LN_EPS = 1e-5
N_MOD = 6
N_MIXERS = 3

SG_CHUNK = 128
SG_GROUPS = 16

RET_HEADS = 8
RET_CHUNK = 128
ROPE_BASE = 10000.0

RWKV_HEAD_DIM = 64
RWKV_GN_EPS = RWKV_HEAD_DIM * 1e-5
RWKV_DECAY_OFFSET = 0.5
RWKV_CHUNK = 64
RWKV_LANES = 256

CONV_WIDTH = 3

V7X_LANES = 128
V7X_SUBLANES = 8
V7X_VMEM_LIMIT = 56 * 1024 * 1024


def _params(n_axes, vmem=V7X_VMEM_LIMIT):
    return pltpu.CompilerParams(dimension_semantics=("arbitrary",) * n_axes,
                                vmem_limit_bytes=vmem)


def _dot(a, b):
    return jnp.dot(a, b, preferred_element_type=F32)


def _dot_nt(a, b):
    return lax.dot_general(a, b, (((1,), (1,)), ((), ())), preferred_element_type=F32)


def _dot_tn(a, b):
    return lax.dot_general(a, b, (((0,), (0,)), ((), ())), preferred_element_type=F32)


def _rms_mod(h, sc, sh):
    ms = jnp.mean(h * h, axis=-1, keepdims=True)
    return (h * lax.rsqrt(ms + NORM_EPS)) * (1.0 + sc) + sh


def _mod_kernel(c_ref, w_ref, b_ref, o_ref):
    cond = jax.nn.silu(c_ref[...])
    o_ref[...] = _dot(cond.astype(BF16), w_ref[...].astype(BF16)) + b_ref[...]


def _modulation(c, ada_w, ada_b, tn=1024):
    depth, d, n = ada_w.shape
    b = c.shape[0]
    rows = -(-b // V7X_SUBLANES) * V7X_SUBLANES
    c_pad = jnp.pad(c, ((0, rows - b), (0, 0)))
    out = pl.pallas_call(
        _mod_kernel,
        grid=(depth, n // tn),
        in_specs=[
            pl.BlockSpec((rows, d), lambda l, j: (0, 0)),
            pl.BlockSpec((None, d, tn), lambda l, j: (l, 0, j)),
            pl.BlockSpec((None, 1, tn), lambda l, j: (l, 0, j)),
        ],
        out_specs=pl.BlockSpec((None, rows, tn), lambda l, j: (l, 0, j)),
        out_shape=jax.ShapeDtypeStruct((depth, rows, n), F32),
        compiler_params=_params(2),
        name="adaln_mod",
    )(c_pad, ada_w, ada_b.reshape(depth, 1, n))
    return out[:, :b].reshape(depth * b * N_MOD, 1, d)


def _mod_spec(d, layer, k, batch, tiles_per_batch):
    base = layer * batch * N_MOD + k
    return pl.BlockSpec((None, 1, d),
                        lambda i, *_: (base + (i // tiles_per_batch) * N_MOD, 0, 0))


def _norm_matmul_kernel(h_ref, sc_ref, sh_ref, w_ref, *rest, epilogue):
    if epilogue == "rope":
        cos_ref, sin_ref, o_ref, xm_ref = rest
    else:
        o_ref, xm_ref = rest

    @pl.when(pl.program_id(1) == 0)
    def _():
        xm_ref[...] = _rms_mod(h_ref[...], sc_ref[...], sh_ref[...]).astype(BF16)

    y = _dot(xm_ref[...], w_ref[...])
    if epilogue == "gelu":
        y = jax.nn.gelu(y)
    elif epilogue == "rope":
        cos, sin = cos_ref[...], sin_ref[...]
        half = cos.shape[-1]
        parts = []
        for hd in range(y.shape[-1] // (2 * half)):
            x1 = y[:, 2 * hd * half:(2 * hd + 1) * half]
            x2 = y[:, (2 * hd + 1) * half:(2 * hd + 2) * half]
            parts += [x1 * cos - x2 * sin, x2 * cos + x1 * sin]
        y = jnp.concatenate(parts, axis=-1)
    o_ref[...] = y.astype(o_ref.dtype)


def _norm_matmul(h, mod, layer, k_scale, k_shift, w, batch, *, epilogue="none",
                 out_dtype=F32, cos=None, sin=None, tm=512, tn=512):
    t, d = h.shape
    n = w.shape[1]
    tm = min(tm, t // batch)
    tpb = (t // batch) // tm
    in_specs = [
        pl.BlockSpec((tm, d), lambda i, j: (i, 0)),
        _mod_spec(d, layer, k_scale, batch, tpb),
        _mod_spec(d, layer, k_shift, batch, tpb),
        pl.BlockSpec((d, tn), lambda i, j: (0, j)),
    ]
    args = [h, mod, mod, w]
    if epilogue == "rope":
        half = cos.shape[-1]
        in_specs += [pl.BlockSpec((tm, half), lambda i, j: (i, 0))] * 2
        args += [cos, sin]
    return pl.pallas_call(
        functools.partial(_norm_matmul_kernel, epilogue=epilogue),
        grid=(t // tm, n // tn),
        in_specs=in_specs,
        out_specs=pl.BlockSpec((tm, tn), lambda i, j: (i, j)),
        out_shape=jax.ShapeDtypeStruct((t, n), out_dtype),
        scratch_shapes=[pltpu.VMEM((tm, d), BF16)],
        compiler_params=_params(2),
        name="norm_matmul_" + epilogue,
    )(*args)


def _proj_residual_kernel(a_ref, w_ref, h_ref, g_ref, o_ref):
    o_ref[...] = h_ref[...] + g_ref[...] * _dot(a_ref[...], w_ref[...])


def _proj_residual(a, w, h, mod, layer, k_gate, batch, tm=512, tn=512):
    t, kdim = a.shape
    d = w.shape[1]
    tm = min(tm, t // batch)
    tpb = (t // batch) // tm
    base = layer * batch * N_MOD + k_gate
    return pl.pallas_call(
        _proj_residual_kernel,
        grid=(t // tm, d // tn),
        in_specs=[
            pl.BlockSpec((tm, kdim), lambda i, j: (i, 0)),
            pl.BlockSpec((kdim, tn), lambda i, j: (0, j)),
            pl.BlockSpec((tm, tn), lambda i, j: (i, j)),
            pl.BlockSpec((None, 1, tn), lambda i, j: (base + (i // tpb) * N_MOD, 0, j)),
        ],
        out_specs=pl.BlockSpec((tm, tn), lambda i, j: (i, j)),
        out_shape=jax.ShapeDtypeStruct((t, d), F32),
        compiler_params=_params(2),
        name="proj_residual",
    )(a, w, h, mod)


def _sg_core_kernel(u_ref, v_ref, h_ref, g_ref, lng_ref, lnb_ref, ws_ref, bs_ref, wo_ref,
                    o_ref, gated_ref):
    tm, width = v_ref.shape
    gdim = width // SG_GROUPS
    v = v_ref[...]
    mu = jnp.mean(v, axis=-1, keepdims=True)
    var = jnp.mean(jnp.square(v - mu), axis=-1, keepdims=True)
    vn = (((v - mu) * lax.rsqrt(var + LN_EPS)) * lng_ref[...] + lnb_ref[...]).astype(BF16)
    row = lax.broadcasted_iota(jnp.int32, (SG_CHUNK, SG_CHUNK), 0)
    col = lax.broadcasted_iota(jnp.int32, (SG_CHUNK, SG_CHUNK), 1)
    causal = row >= col
    for g in range(SG_GROUPS):
        w_causal = jnp.where(causal, ws_ref[g], 0.0).astype(BF16)
        bias = bs_ref[:, g:g + 1]
        cols = slice(g * gdim, (g + 1) * gdim)
        for c in range(tm // SG_CHUNK):
            rows = slice(c * SG_CHUNK, (c + 1) * SG_CHUNK)
            sv = _dot(w_causal, vn[rows, cols]) + bias
            gated_ref[rows, cols] = (u_ref[rows, cols] * sv).astype(BF16)
    o_ref[...] = h_ref[...] + g_ref[...] * _dot(gated_ref[...], wo_ref[...])


def _sg_core(z, h, mod, layer, batch, ln_g, ln_b, w_s, b_s, w_out, tm=256):
    t, d = h.shape
    width = z.shape[1] // 2
    tm = min(tm, t // batch)
    tpb = (t // batch) // tm
    return pl.pallas_call(
        _sg_core_kernel,
        grid=(t // tm,),
        in_specs=[
            pl.BlockSpec((tm, width), lambda i: (i, 0)),
            pl.BlockSpec((tm, width), lambda i: (i, 1)),
            pl.BlockSpec((tm, d), lambda i: (i, 0)),
            _mod_spec(d, layer, 2, batch, tpb),
            pl.BlockSpec((1, width), lambda i: (0, 0)),
            pl.BlockSpec((1, width), lambda i: (0, 0)),
            pl.BlockSpec((SG_GROUPS, SG_CHUNK, SG_CHUNK), lambda i: (0, 0, 0)),
            pl.BlockSpec((SG_CHUNK, SG_GROUPS), lambda i: (0, 0)),
            pl.BlockSpec((width, d), lambda i: (0, 0)),
        ],
        out_specs=pl.BlockSpec((tm, d), lambda i: (i, 0)),
        out_shape=jax.ShapeDtypeStruct((t, d), F32),
        scratch_shapes=[pltpu.VMEM((tm, width), BF16)],
        compiler_params=_params(1),
        name="sg_core",
    )(z, z, h, mod, ln_g.reshape(1, width), ln_b.reshape(1, width), w_s, b_s.T, w_out)


def _ffn_kernel(h_ref, sc_ref, sh_ref, g_ref, wv_ref, wg_ref, cwv_ref, cwg_ref, cbv_ref, cbg_ref,
                wd_ref, *rest, tiles_per_batch, final_norm):
    if final_norm:
        fg_ref, o_ref, xm_ref, hbuf_ref, halo_ref = rest
    else:
        o_ref, xm_ref, hbuf_ref, halo_ref = rest
    i, j = pl.program_id(0), pl.program_id(1)
    tm = h_ref.shape[0]
    tf = wv_ref.shape[1]
    pad = V7X_SUBLANES

    @pl.when(j == 0)
    def _():
        xm_ref[...] = _rms_mod(h_ref[...], sc_ref[...], sh_ref[...]).astype(BF16)
        o_ref[...] = jnp.zeros_like(o_ref)

    seq_start = (i % tiles_per_batch) == 0
    hbuf_ref[0:pad, :] = jnp.where(seq_start, 0.0, halo_ref[j])
    xm = xm_ref[...]
    hbuf_ref[pad:, 0:tf] = _dot(xm, wv_ref[...])
    hbuf_ref[pad:, tf:] = _dot(xm, wg_ref[...])
    halo_ref[j] = hbuf_ref[tm:, :]

    def conv(cols, cw_ref, cb_ref):
        acc = hbuf_ref[pad - 2:pad - 2 + tm, cols] * cw_ref[0:1, :]
        acc = acc + hbuf_ref[pad - 1:pad - 1 + tm, cols] * cw_ref[1:2, :]
        acc = acc + hbuf_ref[pad:pad + tm, cols] * cw_ref[2:3, :]
        return acc + cb_ref[...]

    val = conv(slice(0, tf), cwv_ref, cbv_ref)
    gate = conv(slice(tf, 2 * tf), cwg_ref, cbg_ref)
    act = (jax.nn.silu(gate) * val).astype(BF16)
    o_ref[...] += _dot(act, wd_ref[...])

    @pl.when(j == pl.num_programs(1) - 1)
    def _():
        out = h_ref[...] + g_ref[...] * o_ref[...]
        if final_norm:
            ms = jnp.mean(out * out, axis=-1, keepdims=True)
            out = (out * lax.rsqrt(ms + NORM_EPS)) * fg_ref[...]
        o_ref[...] = out


def _conv_ffn(h, mod, layer, batch, w_up, conv_w, conv_b, w_down, final_gain=None, tm=512, tf=512):
    t, d = h.shape
    fp = w_down.shape[0]
    nf = fp // tf
    tm = min(tm, t // batch)
    tpb = (t // batch) // tm
    final_norm = final_gain is not None
    in_specs = [
        pl.BlockSpec((tm, d), lambda i, j: (i, 0)),
        _mod_spec(d, layer, 4, batch, tpb),
        _mod_spec(d, layer, 3, batch, tpb),
        _mod_spec(d, layer, 5, batch, tpb),
        pl.BlockSpec((d, tf), lambda i, j: (0, j)),
        pl.BlockSpec((d, tf), lambda i, j: (0, nf + j)),
        pl.BlockSpec((CONV_WIDTH, tf), lambda i, j: (0, j)),
        pl.BlockSpec((CONV_WIDTH, tf), lambda i, j: (0, nf + j)),
        pl.BlockSpec((1, tf), lambda i, j: (0, j)),
        pl.BlockSpec((1, tf), lambda i, j: (0, nf + j)),
        pl.BlockSpec((tf, d), lambda i, j: (j, 0)),
    ]
    args = [h, mod, mod, mod, w_up, w_up, conv_w, conv_w, conv_b, conv_b, w_down]
    if final_norm:
        in_specs.append(pl.BlockSpec((1, d), lambda i, j: (0, 0)))
        args.append(final_gain.reshape(1, d))
    return pl.pallas_call(
        functools.partial(_ffn_kernel, tiles_per_batch=tpb, final_norm=final_norm),
        grid=(t // tm, nf),
        in_specs=in_specs,
        out_specs=pl.BlockSpec((tm, d), lambda i, j: (i, 0)),
        out_shape=jax.ShapeDtypeStruct((t, d), F32),
        scratch_shapes=[
            pltpu.VMEM((tm, d), BF16),
            pltpu.VMEM((tm + V7X_SUBLANES, 2 * tf), F32),
            pltpu.VMEM((nf, V7X_SUBLANES, 2 * tf), F32),
        ],
        compiler_params=_params(2),
        name="conv_ffn",
    )(*args)


def _rope_table_kernel(pos_ref, freq_ref, cos_ref, sin_ref):
    ang = pos_ref[...].astype(F32) * freq_ref[...]
    cos_ref[...] = jnp.cos(ang)
    sin_ref[...] = jnp.sin(ang)


def _rope_tables(positions, half, tm=1024):
    t = positions.size
    inv_freq = ROPE_BASE ** (-jnp.arange(half, dtype=F32) / half)
    tm = min(tm, t)
    return pl.pallas_call(
        _rope_table_kernel,
        grid=(t // tm,),
        in_specs=[pl.BlockSpec((tm, 1), lambda i: (i, 0)),
                  pl.BlockSpec((1, half), lambda i: (0, 0))],
        out_specs=[pl.BlockSpec((tm, half), lambda i: (i, 0))] * 2,
        out_shape=[jax.ShapeDtypeStruct((t, half), F32)] * 2,
        compiler_params=_params(1),
        name="rope_tables",
    )(positions.reshape(t, 1), inv_freq.reshape(1, half))


def _ret_core_kernel(q_ref, k_ref, v_ref, g_ref, di_ref, qd_ref, kd_ref, cd_ref, gg_ref, gb_ref,
                     o_ref, state_ref, *, k_scale):
    state_ref[...] = jnp.zeros_like(state_ref)
    decay_inner = di_ref[...]
    q_decay, k_decay, chunk_decay = qd_ref[...], kd_ref[...], cd_ref[...]
    gain, bias = gg_ref[...], gb_ref[...]

    def body(c, carry):
        rows = pl.ds(pl.multiple_of(c * RET_CHUNK, RET_CHUNK), RET_CHUNK)
        q = q_ref[rows, :]
        k = k_ref[rows, :] * k_scale
        v = v_ref[rows, :]
        inner = _dot_nt(q.astype(BF16), k.astype(BF16)) * decay_inner
        state = state_ref[...]
        out = _dot(inner.astype(BF16), v) + _dot((q * q_decay).astype(BF16), state.astype(BF16))
        state_ref[...] = state * chunk_decay + _dot_tn((k * k_decay).astype(BF16), v)
        mu = jnp.mean(out, axis=-1, keepdims=True)
        var = jnp.mean(jnp.square(out - mu), axis=-1, keepdims=True)
        y = ((out - mu) * lax.rsqrt(var + NORM_EPS)) * gain + bias
        o_ref[rows, :] = (jax.nn.silu(g_ref[rows, :]) * y).astype(BF16)
        return carry

    lax.fori_loop(0, q_ref.shape[0] // RET_CHUNK, body, 0)


def _ret_core(qk, v, g, gn_g, gn_b, batch):
    t, two_d = qk.shape
    d = two_d // 2
    s = t // batch
    dk = d // RET_HEADS
    dv = v.shape[1] // RET_HEADS
    log_gamma = jnp.log1p(-jnp.exp2(-5.0 - jnp.arange(RET_HEADS, dtype=F32)))
    idx = jnp.arange(RET_CHUNK, dtype=F32)
    rel = idx[:, None] - idx[None, :]
    decay_inner = jnp.where(rel >= 0, jnp.exp(log_gamma[:, None, None] * jnp.maximum(rel, 0.0)), 0.0)
    q_decay = jnp.exp(log_gamma[:, None] * (idx + 1.0))[..., None]
    k_decay = jnp.exp(log_gamma[:, None] * (RET_CHUNK - 1.0 - idx))[..., None]
    chunk_decay = jnp.exp(log_gamma * RET_CHUNK)[:, None, None]
    return pl.pallas_call(
        functools.partial(_ret_core_kernel, k_scale=dk ** -0.5),
        grid=(batch, RET_HEADS),
        in_specs=[
            pl.BlockSpec((s, dk), lambda b, h: (b, h)),
            pl.BlockSpec((s, dk), lambda b, h: (b, RET_HEADS + h)),
            pl.BlockSpec((s, dv), lambda b, h: (b, h)),
            pl.BlockSpec((s, dv), lambda b, h: (b, h)),
            pl.BlockSpec((None, RET_CHUNK, RET_CHUNK), lambda b, h: (h, 0, 0)),
            pl.BlockSpec((None, RET_CHUNK, 1), lambda b, h: (h, 0, 0)),
            pl.BlockSpec((None, RET_CHUNK, 1), lambda b, h: (h, 0, 0)),
            pl.BlockSpec((None, 1, 1), lambda b, h: (h, 0, 0)),
            pl.BlockSpec((1, dv), lambda b, h: (0, h)),
            pl.BlockSpec((1, dv), lambda b, h: (0, h)),
        ],
        out_specs=pl.BlockSpec((s, dv), lambda b, h: (b, h)),
        out_shape=jax.ShapeDtypeStruct((t, v.shape[1]), BF16),
        scratch_shapes=[pltpu.VMEM((dk, dv), F32)],
        compiler_params=_params(2),
        name="ret_core",
    )(qk, qk, v, g, decay_inner, q_decay, k_decay, chunk_decay,
      gn_g.reshape(1, -1), gn_b.reshape(1, -1))


def _shifted_inputs(h_ref, hprev_ref, sc_ref, sh_ref, seq_start):
    sc, sh = sc_ref[...], sh_ref[...]
    xm = _rms_mod(h_ref[...], sc, sh)
    prev = _rms_mod(hprev_ref[...], sc, sh)
    last = prev[V7X_SUBLANES - 1:V7X_SUBLANES, :]
    last = jnp.where(seq_start, 0.0, last)
    row = lax.broadcasted_iota(jnp.int32, xm.shape, 0)
    shifted = jnp.where(row == 0, last, pltpu.roll(xm, 1, 0))
    return xm, shifted - xm


def _rwkv_rkv_kernel(h_ref, hprev_ref, sc_ref, sh_ref, mu_ref, w_ref, o_ref, xm_ref, dx_ref, xs_ref,
                     *, tiles_per_batch, tiles_per_proj):
    i, j = pl.program_id(0), pl.program_id(1)

    @pl.when(j == 0)
    def _():
        xm, dx = _shifted_inputs(h_ref, hprev_ref, sc_ref, sh_ref, (i % tiles_per_batch) == 0)
        xm_ref[...] = xm
        dx_ref[...] = dx

    @pl.when(j % tiles_per_proj == 0)
    def _():
        xs_ref[...] = (xm_ref[...] + dx_ref[...] * mu_ref[...]).astype(BF16)

    o_ref[...] = _dot(xs_ref[...], w_ref[...])


def _prev_rows_spec(tm, d):
    blocks = tm // V7X_SUBLANES
    return pl.BlockSpec((V7X_SUBLANES, d), lambda i, *_: (jnp.maximum(i * blocks - 1, 0), 0))


def _rwkv_rkv(h, mod, layer, batch, mu, w_rkv, tm=512, tn=512):
    t, d = h.shape
    tm = min(tm, t // batch)
    tpb = (t // batch) // tm
    tpp = d // tn
    return pl.pallas_call(
        functools.partial(_rwkv_rkv_kernel, tiles_per_batch=tpb, tiles_per_proj=tpp),
        grid=(t // tm, 3 * tpp),
        in_specs=[
            pl.BlockSpec((tm, d), lambda i, j: (i, 0)),
            _prev_rows_spec(tm, d),
            _mod_spec(d, layer, 1, batch, tpb),
            _mod_spec(d, layer, 0, batch, tpb),
            pl.BlockSpec((None, 1, d), lambda i, j: (j // tpp, 0, 0)),
            pl.BlockSpec((None, d, tn), lambda i, j: (j // tpp, 0, j % tpp)),
        ],
        out_specs=pl.BlockSpec((tm, tn), lambda i, j: (i, j)),
        out_shape=jax.ShapeDtypeStruct((t, 3 * d), F32),
        scratch_shapes=[pltpu.VMEM((tm, d), F32), pltpu.VMEM((tm, d), F32), pltpu.VMEM((tm, d), BF16)],
        compiler_params=_params(2),
        name="rwkv_rkv",
    )(h, h, mod, mod, mu.reshape(-1, 1, d), w_rkv)


def _rwkv_lora_kernel(h_ref, hprev_ref, sc_ref, sh_ref, mu_ref, w0_ref, w1_ref, w2_ref,
                      a0_ref, a1_ref, a2_ref, g1_ref, g2_ref, lw_ref, a_ref, g_ref, *, tiles_per_batch):
    i = pl.program_id(0)
    xm, dx = _shifted_inputs(h_ref, hprev_ref, sc_ref, sh_ref, (i % tiles_per_batch) == 0)

    def mixed(p):
        return (xm + dx * mu_ref[p]).astype(BF16)

    lora_w = _dot(jnp.tanh(_dot(mixed(3), w1_ref[...])).astype(BF16), w2_ref[...])
    w_log = -jax.nn.softplus(-(w0_ref[...] + lora_w)) - RWKV_DECAY_OFFSET
    lw_ref[...] = -jnp.exp(w_log)
    lora_a = _dot(_dot(mixed(4), a1_ref[...]).astype(BF16), a2_ref[...])
    a_ref[...] = jax.nn.sigmoid(a0_ref[...] + lora_a)
    g_ref[...] = _dot(jax.nn.sigmoid(_dot(mixed(5), g1_ref[...])).astype(BF16), g2_ref[...])


def _rwkv_lora(h, mod, layer, batch, mu, w0, w1, w2, a0, a1, a2, g1, g2, tm=256):
    t, d = h.shape
    tm = min(tm, t // batch)
    tpb = (t // batch) // tm
    full = lambda a: pl.BlockSpec(a.shape, lambda i: (0,) * a.ndim)
    mu3 = mu.reshape(-1, 1, d)
    w0, a0 = w0.reshape(1, d), a0.reshape(1, d)
    consts = [mu3, w0, w1, w2, a0, a1, a2, g1, g2]
    return pl.pallas_call(
        functools.partial(_rwkv_lora_kernel, tiles_per_batch=tpb),
        grid=(t // tm,),
        in_specs=[
            pl.BlockSpec((tm, d), lambda i: (i, 0)),
            _prev_rows_spec(tm, d),
            _mod_spec(d, layer, 1, batch, tpb),
            _mod_spec(d, layer, 0, batch, tpb),
        ] + [full(a) for a in consts],
        out_specs=[pl.BlockSpec((tm, d), lambda i: (i, 0))] * 3,
        out_shape=[jax.ShapeDtypeStruct((t, d), F32)] * 3,
        compiler_params=_params(1),
        name="rwkv_lora",
    )(h, h, mod, mod, *consts)


def _split_dot(x, ones, terms):
    acc = None
    for _ in range(terms):
        piece = x.astype(BF16)
        part = _dot(piece, ones)
        acc = part if acc is None else acc + part
        x = x - piece.astype(F32)
    return acc


def _split_dot_left(ones, x, terms):
    acc = None
    for _ in range(terms):
        piece = x.astype(BF16)
        part = _dot(ones, piece)
        acc = part if acc is None else acc + part
        x = x - piece.astype(F32)
    return acc


def _rwkv_core_kernel(r_ref, k_ref, v_ref, lw_ref, a_ref, g_ref, kk_ref, ka_ref, rk_ref,
                      lng_ref, lnb_ref, o_ref, state_ref):
    c_len, w = RWKV_CHUNK, RWKV_LANES
    heads = w // RWKV_HEAD_DIM
    n = heads * c_len

    @pl.when(pl.program_id(2) == 0)
    def _():
        state_ref[...] = jnp.zeros_like(state_ref)

    rb = lax.broadcasted_iota(jnp.int32, (n, w), 0)
    lb = lax.broadcasted_iota(jnp.int32, (n, w), 1)
    head_match = (rb // c_len) == (lb // RWKV_HEAD_DIM)
    ones_bd = jnp.where((rb // RWKV_HEAD_DIM) == (lb // RWKV_HEAD_DIM), 1.0, 0.0).astype(BF16)
    tr = lax.broadcasted_iota(jnp.int32, (c_len, n), 0)
    ts = lax.broadcasted_iota(jnp.int32, (c_len, n), 1) % c_len
    strict = tr > ts
    incl = tr >= ts
    eye = jnp.where(tr == ts, 1.0, 0.0)
    cr = lax.broadcasted_iota(jnp.int32, (c_len, c_len), 0)
    cc = lax.broadcasted_iota(jnp.int32, (c_len, c_len), 1)
    tril_ones = jnp.where(cr >= cc, 1.0, 0.0).astype(BF16)

    def expand(x):
        xb = x.astype(BF16)
        return jnp.where(head_match, jnp.concatenate([xb] * heads, axis=0), jnp.zeros((), BF16))

    def seg_sum(x):
        return _split_dot(x, ones_bd, 2)

    def chunk(rows, grp):
        lanes = slice(grp * w, (grp + 1) * w)
        r, k, v = r_ref[rows, lanes], k_ref[rows, lanes], v_ref[rows, lanes]
        lw, a, g = lw_ref[rows, lanes], a_ref[rows, lanes], g_ref[rows, lanes]
        k_k, k_a, r_k = kk_ref[:, lanes], ka_ref[:, lanes], rk_ref[:, lanes]
        ln_g, ln_b = lng_ref[:, lanes], lnb_ref[:, lanes]

        kk = k * k_k
        kk_sq = seg_sum(kk * kk)
        kp = k * (1.0 + (a - 1.0) * k_a)
        rk_sum = seg_sum(r * kp * r_k)
        cum = _split_dot_left(tril_ones, lw, 3)
        yield
        kk = kk / jnp.maximum(jnp.sqrt(kk_sq), 1e-12)
        bonus = rk_sum * v
        cum_end = cum[c_len - 1:c_len, :]
        e_pos, e_neg = jnp.exp(cum), jnp.exp(-cum)
        e_tail = jnp.exp(cum_end - cum)
        ba = kk * a
        r_t = (r * e_pos).astype(BF16)
        a_t = (-kk * jnp.exp(cum - lw)).astype(BF16)
        b_t, k_t = ba * e_neg, kp * e_neg
        b_h, k_h = ba * e_tail, kp * e_tail
        vb = v.astype(BF16)

        gram = _dot_nt(jnp.concatenate([a_t, r_t], axis=0),
                       jnp.concatenate([expand(b_t), expand(k_t)], axis=0))
        yield
        ab = jnp.where(strict, gram[:c_len, :n], 0.0)
        ak = jnp.where(strict, gram[:c_len, n:], 0.0)
        rbm = jnp.where(incl, gram[c_len:, :n], 0.0)
        rkm = jnp.where(incl, gram[c_len:, n:], 0.0)

        tinv = eye + ab
        p = _dot(ab.astype(BF16), expand(ab))
        akv_rkv = _dot(jnp.concatenate([ak, rkm], axis=0).astype(BF16), expand(v))
        yield
        for _ in range(c_len.bit_length() - 3):
            both = _dot(jnp.concatenate([p, tinv], axis=0).astype(BF16), expand(p))
            yield
            p, tinv = both[:c_len], tinv + both[c_len:]
        last = _dot(tinv.astype(BF16), expand(p))
        state = state_ref[grp]
        ag_rg = _dot_nt(jnp.concatenate([a_t, r_t], axis=0), state.astype(BF16))
        yield
        tb = (tinv + last).astype(BF16)
        u = _dot(tb, expand(akv_rkv[:c_len] + ag_rg[:c_len]))
        yield
        o = _dot(rbm.astype(BF16), expand(u))
        upd = _dot_tn(jnp.concatenate([u.astype(BF16), vb], axis=0),
                      jnp.concatenate([b_h, k_h], axis=0).astype(BF16))
        yield
        o = ag_rg[c_len:] + o + akv_rkv[c_len:]
        state_ref[grp] = state * jnp.exp(cum_end) + jnp.where(
            (rb // RWKV_HEAD_DIM) == (lb // RWKV_HEAD_DIM), upd, 0.0)
        mean = seg_sum(o) * (1.0 / RWKV_HEAD_DIM)
        yield
        cen = o - mean
        var = seg_sum(cen * cen) * (1.0 / RWKV_HEAD_DIM)
        yield
        y = (cen * lax.rsqrt(var + RWKV_GN_EPS)) * ln_g + ln_b
        o_ref[rows, lanes] = ((y + bonus) * g).astype(BF16)

    def body(c, carry):
        rows = pl.ds(pl.multiple_of(c * c_len, c_len), c_len)
        pending = [chunk(rows, grp) for grp in range(state_ref.shape[0])]
        while pending:
            pending = [gen for gen in pending if next(gen, True) is None]
        return carry

    lax.fori_loop(0, r_ref.shape[0] // c_len, body, 0)


def _rwkv_core(rkv, lw, a, g, k_k, k_a, r_k, ln_g, ln_b, batch, rows=256, groups=8):
    t, d = lw.shape
    s = t // batch
    rows = min(rows, s)
    w = RWKV_LANES * groups
    nq = d // w
    spb = s // rows
    seq = lambda off: pl.BlockSpec((rows, w), lambda b, q, i: (b * spb + i, off + q))
    par = pl.BlockSpec((1, w), lambda b, q, i: (0, q))
    vec = lambda x: x.reshape(1, d)
    return pl.pallas_call(
        _rwkv_core_kernel,
        grid=(batch, nq, spb),
        in_specs=[seq(0), seq(nq), seq(2 * nq), seq(0), seq(0), seq(0), par, par, par, par, par],
        out_specs=seq(0),
        out_shape=jax.ShapeDtypeStruct((t, d), BF16),
        scratch_shapes=[pltpu.VMEM((groups, RWKV_LANES, RWKV_LANES), F32)],
        compiler_params=_params(3),
        name="rwkv_core",
    )(rkv, rkv, rkv, lw, a, g, vec(k_k), vec(k_a), vec(r_k), vec(ln_g), vec(ln_b))


def _pad_cols(x, n):
    return jnp.pad(x, [(0, 0)] * (x.ndim - 1) + [(0, n - x.shape[-1])])


def _pad_rows(x, n):
    return jnp.pad(x, [(0, n - x.shape[0])] + [(0, 0)] * (x.ndim - 1))


def _round_up(x, m):
    return -(-x // m) * m


def kernel(x, c, positions, ada_w, ada_b, ffn_w_up, ffn_conv_w, ffn_conv_b, ffn_w_down, sg_w_in, sg_ln_g, sg_ln_b, sg_w_s, sg_b_s, sg_w_out, ret_w_in, ret_gn_g, ret_gn_b, ret_w_out, rwkv_mu, rwkv_w_rkv, rwkv_w0, rwkv_w1, rwkv_w2, rwkv_a0, rwkv_a1, rwkv_a2, rwkv_g1, rwkv_g2, rwkv_k_k, rwkv_k_a, rwkv_r_k, rwkv_ln_g, rwkv_ln_b, rwkv_w_out, final_norm_g):
    batch, seq, d = x.shape
    depth = ada_w.shape[0]
    t = batch * seq
    d_ff = ffn_w_down.shape[1]
    ffn_tile = 512
    fp = _round_up(d_ff, ffn_tile)

    mod = _modulation(c, ada_w, ada_b)
    h = x.reshape(t, d)
    cos = sin = None

    for layer in range(depth):
        kind, j = layer % N_MIXERS, layer // N_MIXERS
        if kind == 0:
            z = _norm_matmul(h, mod, layer, 1, 0, sg_w_in[j].astype(BF16), batch, epilogue="gelu")
            h = _sg_core(z, h, mod, layer, batch, sg_ln_g[j], sg_ln_b[j], sg_w_s[j], sg_b_s[j],
                         sg_w_out[j].astype(BF16))
        elif kind == 1:
            if cos is None:
                cos, sin = _rope_tables(positions, d // RET_HEADS // 2)
            w_in = ret_w_in[j].astype(BF16)
            qk = _norm_matmul(h, mod, layer, 1, 0, w_in[:, :2 * d], batch, epilogue="rope", cos=cos, sin=sin)
            v = _norm_matmul(h, mod, layer, 1, 0, w_in[:, 2 * d:4 * d], batch, out_dtype=BF16)
            g = _norm_matmul(h, mod, layer, 1, 0, w_in[:, 4 * d:], batch)
            o = _ret_core(qk, v, g, ret_gn_g[j], ret_gn_b[j], batch)
            h = _proj_residual(o, ret_w_out[j].astype(BF16), h, mod, layer, 2, batch)
        else:
            rkv = _rwkv_rkv(h, mod, layer, batch, rwkv_mu[j], rwkv_w_rkv[j].astype(BF16))
            lora = _round_up(rwkv_w1.shape[-1], V7X_LANES)
            lora_a = _round_up(rwkv_a1.shape[-1], V7X_LANES)
            lw, a, g = _rwkv_lora(
                h, mod, layer, batch, rwkv_mu[j], rwkv_w0[j],
                _pad_cols(rwkv_w1[j], lora).astype(BF16), _pad_rows(rwkv_w2[j], lora).astype(BF16),
                rwkv_a0[j],
                _pad_cols(rwkv_a1[j], lora_a).astype(BF16), _pad_rows(rwkv_a2[j], lora_a).astype(BF16),
                rwkv_g1[j].astype(BF16), rwkv_g2[j].astype(BF16))
            o = _rwkv_core(rkv, lw, a, g, rwkv_k_k[j], rwkv_k_a[j], rwkv_r_k[j],
                           rwkv_ln_g[j], rwkv_ln_b[j], batch)
            h = _proj_residual(o, rwkv_w_out[j].astype(BF16), h, mod, layer, 2, batch)

        w_up = ffn_w_up[layer]
        w_up = jnp.concatenate([_pad_cols(w_up[:, :d_ff], fp), _pad_cols(w_up[:, d_ff:], fp)], axis=1)
        cw = ffn_conv_w[layer]
        cw = jnp.concatenate([_pad_cols(cw[:, :d_ff], fp), _pad_cols(cw[:, d_ff:], fp)], axis=1)
        cb = ffn_conv_b[layer]
        cb = jnp.concatenate([_pad_cols(cb[:d_ff], fp), _pad_cols(cb[d_ff:], fp)]).reshape(1, 2 * fp)
        h = _conv_ffn(h, mod, layer, batch, w_up.astype(BF16), cw, cb,
                      _pad_rows(ffn_w_down[layer], fp).astype(BF16),
                      final_gain=final_norm_g if layer == depth - 1 else None, tf=ffn_tile)

    return h.reshape(batch, seq, d)
```

```python
import functools

import jax
import jax.numpy as jnp
from jax import lax
from jax.experimental import pallas as pl
from jax.experimental.pallas import tpu as pltpu

F32 = jnp.float32
BF16 = jnp.bfloat16

NORM_EPS = 1e-6
LN_EPS = 1e-5
N_MOD = 6
N_MIXERS = 3

SG_CHUNK = 128
SG_GROUPS = 16

RET_HEADS = 8
RET_CHUNK = 128
ROPE_BASE = 10000.0

RWKV_HEAD_DIM = 64
RWKV_GN_EPS = RWKV_HEAD_DIM * 1e-5
RWKV_DECAY_OFFSET = 0.5
RWKV_CHUNK = 64
RWKV_LANES = 256

CONV_WIDTH = 3
FFN_BLOCK = 128
FFN_SUBCHUNKS = 4
MATMUL_SUBTILE = 512
WEIGHT_BLOCK_ELEMS = 2048 * 1024

V7X_LANES = 128
V7X_SUBLANES = 8
V7X_VMEM_LIMIT = 56 * 1024 * 1024


def _params(n_axes, vmem=V7X_VMEM_LIMIT):
    return pltpu.CompilerParams(dimension_semantics=("arbitrary",) * n_axes,
                                vmem_limit_bytes=vmem)


def _dot(a, b):
    return jnp.dot(a, b, preferred_element_type=F32)


def _dot_nt(a, b):
    return lax.dot_general(a, b, (((1,), (1,)), ((), ())), preferred_element_type=F32)


def _dot_tn(a, b):
    return lax.dot_general(a, b, (((0,), (0,)), ((), ())), preferred_element_type=F32)


def _rms_mod(h, sc, sh):
    ms = jnp.mean(h * h, axis=-1, keepdims=True)
    return (h * lax.rsqrt(ms + NORM_EPS)) * (1.0 + sc) + sh


def _mod_kernel(c_ref, w_ref, b_ref, o_ref):
    cond = jax.nn.silu(c_ref[...])
    o_ref[...] = _dot(cond.astype(BF16), w_ref[...].astype(BF16)) + b_ref[...]


def _modulation(c, ada_w, ada_b, tn=1024):
    depth, d, n = ada_w.shape
    b = c.shape[0]
    rows = -(-b // V7X_SUBLANES) * V7X_SUBLANES
    c_pad = jnp.pad(c, ((0, rows - b), (0, 0)))
    out = pl.pallas_call(
        _mod_kernel,
        grid=(depth, n // tn),
        in_specs=[
            pl.BlockSpec((rows, d), lambda l, j: (0, 0)),
            pl.BlockSpec((None, d, tn), lambda l, j: (l, 0, j)),
            pl.BlockSpec((None, 1, tn), lambda l, j: (l, 0, j)),
        ],
        out_specs=pl.BlockSpec((None, rows, tn), lambda l, j: (l, 0, j)),
        out_shape=jax.ShapeDtypeStruct((depth, rows, n), F32),
        compiler_params=_params(2),
        name="adaln_mod",
    )(c_pad, ada_w, ada_b.reshape(depth, 1, n))
    return out[:, :b].reshape(depth * b * N_MOD, 1, d)


def _mod_spec(d, layer, k, batch, tiles_per_batch):
    base = layer * batch * N_MOD + k
    return pl.BlockSpec((None, 1, d),
                        lambda i, *_: (base + (i // tiles_per_batch) * N_MOD, 0, 0))


def _norm_matmul_kernel(h_ref, sc_ref, sh_ref, w_ref, *rest, epilogue):
    if epilogue == "rope":
        cos_ref, sin_ref, o_ref, xm_ref = rest
    else:
        o_ref, xm_ref = rest

    @pl.when(pl.program_id(1) == 0)
    def _():
        xm_ref[...] = _rms_mod(h_ref[...], sc_ref[...], sh_ref[...]).astype(BF16)

    xm = xm_ref[...]
    tn = w_ref.shape[1]
    sub = min(tn, MATMUL_SUBTILE)
    for c in range(tn // sub):
        cols = slice(c * sub, (c + 1) * sub)
        y = _dot(xm, w_ref[:, cols])
        if epilogue == "gelu":
            y = jax.nn.gelu(y)
        elif epilogue == "rope":
            cos, sin = cos_ref[...], sin_ref[...]
            half = cos.shape[-1]
            parts = []
            for hd in range(sub // (2 * half)):
                x1 = y[:, 2 * hd * half:(2 * hd + 1) * half]
                x2 = y[:, (2 * hd + 1) * half:(2 * hd + 2) * half]
                parts += [x1 * cos - x2 * sin, x2 * cos + x1 * sin]
            y = jnp.concatenate(parts, axis=-1)
        o_ref[:, cols] = y.astype(o_ref.dtype)


def _norm_matmul(h, mod, layer, k_scale, k_shift, w, batch, *, n, col_offset=0, epilogue="none",
                 out_dtype=F32, cos=None, sin=None, tm=1024, tn=1024):
    t, d = h.shape
    tm = min(tm, t // batch)
    tpb = (t // batch) // tm
    off = col_offset // tn
    in_specs = [
        pl.BlockSpec((tm, d), lambda i, j: (i, 0)),
        _mod_spec(d, layer, k_scale, batch, tpb),
        _mod_spec(d, layer, k_shift, batch, tpb),
        pl.BlockSpec((d, tn), lambda i, j: (0, off + j)),
    ]
    args = [h, mod, mod, w]
    if epilogue == "rope":
        half = cos.shape[-1]
        in_specs += [pl.BlockSpec((tm, half), lambda i, j: (i, 0))] * 2
        args += [cos, sin]
    return pl.pallas_call(
        functools.partial(_norm_matmul_kernel, epilogue=epilogue),
        grid=(t // tm, n // tn),
        in_specs=in_specs,
        out_specs=pl.BlockSpec((tm, tn), lambda i, j: (i, j)),
        out_shape=jax.ShapeDtypeStruct((t, n), out_dtype),
        scratch_shapes=[pltpu.VMEM((tm, d), BF16)],
        compiler_params=_params(2),
        name="norm_matmul_" + epilogue,
    )(*args)


def _proj_residual_kernel(a_ref, w_ref, h_ref, g_ref, o_ref):
    a = a_ref[...]
    tn = w_ref.shape[1]
    sub = min(tn, MATMUL_SUBTILE)
    for c in range(tn // sub):
        cols = slice(c * sub, (c + 1) * sub)
        o_ref[:, cols] = h_ref[:, cols] + g_ref[:, cols] * _dot(a, w_ref[:, cols])


def _proj_residual(a, w, h, mod, layer, k_gate, batch, tm=1024, tn=1024):
    t, kdim = a.shape
    d = w.shape[1]
    tn = min(tn, WEIGHT_BLOCK_ELEMS // kdim)
    tm = min(tm, t // batch)
    tpb = (t // batch) // tm
    base = layer * batch * N_MOD + k_gate
    return pl.pallas_call(
        _proj_residual_kernel,
        grid=(t // tm, d // tn),
        in_specs=[
            pl.BlockSpec((tm, kdim), lambda i, j: (i, 0)),
            pl.BlockSpec((kdim, tn), lambda i, j: (0, j)),
            pl.BlockSpec((tm, tn), lambda i, j: (i, j)),
            pl.BlockSpec((None, 1, tn), lambda i, j: (base + (i // tpb) * N_MOD, 0, j)),
        ],
        out_specs=pl.BlockSpec((tm, tn), lambda i, j: (i, j)),
        out_shape=jax.ShapeDtypeStruct((t, d), F32),
        compiler_params=_params(2),
        name="proj_residual",
    )(a, w, h, mod)


def _sg_core_kernel(u_ref, v_ref, h_ref, g_ref, lng_ref, lnb_ref, ws_ref, bs_ref, wo_ref,
                    o_ref, gated_ref):
    tm, width = v_ref.shape
    gdim = width // SG_GROUPS
    v = v_ref[...]
    mu = jnp.mean(v, axis=-1, keepdims=True)
    var = jnp.mean(jnp.square(v - mu), axis=-1, keepdims=True)
    vn = (((v - mu) * lax.rsqrt(var + LN_EPS)) * lng_ref[...] + lnb_ref[...]).astype(BF16)
    row = lax.broadcasted_iota(jnp.int32, (SG_CHUNK, SG_CHUNK), 0)
    col = lax.broadcasted_iota(jnp.int32, (SG_CHUNK, SG_CHUNK), 1)
    causal = row >= col
    for g in range(SG_GROUPS):
        w_causal = jnp.where(causal, ws_ref[g], 0.0).astype(BF16)
        bias = bs_ref[:, g:g + 1]
        cols = slice(g * gdim, (g + 1) * gdim)
        for c in range(tm // SG_CHUNK):
            rows = slice(c * SG_CHUNK, (c + 1) * SG_CHUNK)
            sv = _dot(w_causal, vn[rows, cols]) + bias
            gated_ref[rows, cols] = (u_ref[rows, cols] * sv).astype(BF16)
    o_ref[...] = h_ref[...] + g_ref[...] * _dot(gated_ref[...], wo_ref[...])


def _sg_core(z, h, mod, layer, batch, ln_g, ln_b, w_s, b_s, w_out, tm=256):
    t, d = h.shape
    width = z.shape[1] // 2
    tm = min(tm, t // batch)
    tpb = (t // batch) // tm
    return pl.pallas_call(
        _sg_core_kernel,
        grid=(t // tm,),
        in_specs=[
            pl.BlockSpec((tm, width), lambda i: (i, 0)),
            pl.BlockSpec((tm, width), lambda i: (i, 1)),
            pl.BlockSpec((tm, d), lambda i: (i, 0)),
            _mod_spec(d, layer, 2, batch, tpb),
            pl.BlockSpec((1, width), lambda i: (0, 0)),
            pl.BlockSpec((1, width), lambda i: (0, 0)),
            pl.BlockSpec((SG_GROUPS, SG_CHUNK, SG_CHUNK), lambda i: (0, 0, 0)),
            pl.BlockSpec((SG_CHUNK, SG_GROUPS), lambda i: (0, 0)),
            pl.BlockSpec((width, d), lambda i: (0, 0)),
        ],
        out_specs=pl.BlockSpec((tm, d), lambda i: (i, 0)),
        out_shape=jax.ShapeDtypeStruct((t, d), F32),
        scratch_shapes=[pltpu.VMEM((tm, width), BF16)],
        compiler_params=_params(1),
        name="sg_core",
    )(z, z, h, mod, ln_g.reshape(1, width), ln_b.reshape(1, width), w_s, b_s.T, w_out)


def _ffn_kernel(h_ref, sc_ref, sh_ref, g_ref, wu_ref, cw_ref, cb_ref, wd_ref, *rest,
                tiles_per_batch, final_norm):
    if final_norm:
        fg_ref, o_ref, xm_ref, halo_ref = rest[:4]
    else:
        o_ref, xm_ref, halo_ref = rest[:3]
    hbufs = rest[-FFN_SUBCHUNKS:]
    i, j = pl.program_id(0), pl.program_id(1)
    tm = h_ref.shape[0]
    pad = V7X_SUBLANES
    width = 2 * FFN_BLOCK

    @pl.when(j == 0)
    def _():
        xm_ref[...] = _rms_mod(h_ref[...], sc_ref[...], sh_ref[...]).astype(BF16)
        o_ref[...] = jnp.zeros_like(o_ref)

    seq_start = (i % tiles_per_batch) == 0
    xm = xm_ref[...]
    for k, hb in enumerate(hbufs):
        hb[0:pad, :] = jnp.where(seq_start, 0.0, halo_ref[j, k])
        hb[pad:, :] = _dot(xm, wu_ref[:, k * width:(k + 1) * width])
        halo_ref[j, k] = hb[tm:, :]

    acts = []
    for k, hb in enumerate(hbufs):
        cols = slice(k * width, (k + 1) * width)
        cw = cw_ref[:, cols]
        conv = hb[pad - 2:pad - 2 + tm, :] * cw[0:1, :]
        conv = conv + hb[pad - 1:pad - 1 + tm, :] * cw[1:2, :]
        conv = conv + hb[pad:pad + tm, :] * cw[2:3, :]
        conv = conv + cb_ref[:, cols]
        acts.append((jax.nn.silu(conv[:, FFN_BLOCK:]) * conv[:, :FFN_BLOCK]).astype(BF16))
        if k % 2 == 1:
            rows = slice((k - 1) * FFN_BLOCK, (k + 1) * FFN_BLOCK)
            o_ref[...] += _dot(jnp.concatenate(acts[-2:], axis=1), wd_ref[rows, :])

    @pl.when(j == pl.num_programs(1) - 1)
    def _():
        out = h_ref[...] + g_ref[...] * o_ref[...]
        if final_norm:
            ms = jnp.mean(out * out, axis=-1, keepdims=True)
            out = (out * lax.rsqrt(ms + NORM_EPS)) * fg_ref[...]
        o_ref[...] = out


def _conv_ffn(h, mod, layer, batch, w_up, conv_w, conv_b, w_down, final_gain=None, tm=512):
    t, d = h.shape
    fp = w_down.shape[0]
    tf = FFN_SUBCHUNKS * FFN_BLOCK
    nf = fp // tf
    tm = min(tm, t // batch)
    tpb = (t // batch) // tm
    final_norm = final_gain is not None
    in_specs = [
        pl.BlockSpec((tm, d), lambda i, j: (i, 0)),
        _mod_spec(d, layer, 4, batch, tpb),
        _mod_spec(d, layer, 3, batch, tpb),
        _mod_spec(d, layer, 5, batch, tpb),
        pl.BlockSpec((d, 2 * tf), lambda i, j: (0, j)),
        pl.BlockSpec((CONV_WIDTH, 2 * tf), lambda i, j: (0, j)),
        pl.BlockSpec((1, 2 * tf), lambda i, j: (0, j)),
        pl.BlockSpec((tf, d), lambda i, j: (j, 0)),
    ]
    args = [h, mod, mod, mod, w_up, conv_w, conv_b, w_down]
    if final_norm:
        in_specs.append(pl.BlockSpec((1, d), lambda i, j: (0, 0)))
        args.append(final_gain.reshape(1, d))
    return pl.pallas_call(
        functools.partial(_ffn_kernel, tiles_per_batch=tpb, final_norm=final_norm),
        grid=(t // tm, nf),
        in_specs=in_specs,
        out_specs=pl.BlockSpec((tm, d), lambda i, j: (i, 0)),
        out_shape=jax.ShapeDtypeStruct((t, d), F32),
        scratch_shapes=[
            pltpu.VMEM((tm, d), BF16),
            pltpu.VMEM((nf, FFN_SUBCHUNKS, V7X_SUBLANES, 2 * FFN_BLOCK), F32),
        ] + [pltpu.VMEM((tm + V7X_SUBLANES, 2 * FFN_BLOCK), F32)] * FFN_SUBCHUNKS,
        compiler_params=_params(2),
        name="conv_ffn",
    )(*args)


def _rope_table_kernel(pos_ref, freq_ref, cos_ref, sin_ref):
    ang = pos_ref[...].astype(F32) * freq_ref[...]
    cos_ref[...] = jnp.cos(ang)
    sin_ref[...] = jnp.sin(ang)


def _rope_tables(positions, half, tm=1024):
    t = positions.size
    inv_freq = ROPE_BASE ** (-jnp.arange(half, dtype=F32) / half)
    tm = min(tm, t)
    return pl.pallas_call(
        _rope_table_kernel,
        grid=(t // tm,),
        in_specs=[pl.BlockSpec((tm, 1), lambda i: (i, 0)),
                  pl.BlockSpec((1, half), lambda i: (0, 0))],
        out_specs=[pl.BlockSpec((tm, half), lambda i: (i, 0))] * 2,
        out_shape=[jax.ShapeDtypeStruct((t, half), F32)] * 2,
        compiler_params=_params(1),
        name="rope_tables",
    )(positions.reshape(t, 1), inv_freq.reshape(1, half))


def _ret_core_kernel(q_ref, k_ref, v_ref, g_ref, di_ref, qd_ref, kd_ref, cd_ref, gg_ref, gb_ref,
                     o_ref, state_ref, *, k_scale):
    state_ref[...] = jnp.zeros_like(state_ref)
    decay_inner = di_ref[...]
    q_decay, k_decay, chunk_decay = qd_ref[...], kd_ref[...], cd_ref[...]
    gain, bias = gg_ref[...], gb_ref[...]

    def body(c, carry):
        rows = pl.ds(pl.multiple_of(c * RET_CHUNK, RET_CHUNK), RET_CHUNK)
        q = q_ref[rows, :]
        k = k_ref[rows, :] * k_scale
        v = v_ref[rows, :]
        inner = _dot_nt(q.astype(BF16), k.astype(BF16)) * decay_inner
        state = state_ref[...]
        out = _dot(inner.astype(BF16), v) + _dot((q * q_decay).astype(BF16), state.astype(BF16))
        state_ref[...] = state * chunk_decay + _dot_tn((k * k_decay).astype(BF16), v)
        mu = jnp.mean(out, axis=-1, keepdims=True)
        var = jnp.mean(jnp.square(out - mu), axis=-1, keepdims=True)
        y = ((out - mu) * lax.rsqrt(var + NORM_EPS)) * gain + bias
        o_ref[rows, :] = (jax.nn.silu(g_ref[rows, :]) * y).astype(BF16)
        return carry

    lax.fori_loop(0, q_ref.shape[0] // RET_CHUNK, body, 0)


def _ret_core(qk, v, g, gn_g, gn_b, batch):
    t, two_d = qk.shape
    d = two_d // 2
    s = t // batch
    dk = d // RET_HEADS
    dv = v.shape[1] // RET_HEADS
    log_gamma = jnp.log1p(-jnp.exp2(-5.0 - jnp.arange(RET_HEADS, dtype=F32)))
    idx = jnp.arange(RET_CHUNK, dtype=F32)
    rel = idx[:, None] - idx[None, :]
    decay_inner = jnp.where(rel >= 0, jnp.exp(log_gamma[:, None, None] * jnp.maximum(rel, 0.0)), 0.0)
    q_decay = jnp.exp(log_gamma[:, None] * (idx + 1.0))[..., None]
    k_decay = jnp.exp(log_gamma[:, None] * (RET_CHUNK - 1.0 - idx))[..., None]
    chunk_decay = jnp.exp(log_gamma * RET_CHUNK)[:, None, None]
    return pl.pallas_call(
        functools.partial(_ret_core_kernel, k_scale=dk ** -0.5),
        grid=(batch, RET_HEADS),
        in_specs=[
            pl.BlockSpec((s, dk), lambda b, h: (b, h)),
            pl.BlockSpec((s, dk), lambda b, h: (b, RET_HEADS + h)),
            pl.BlockSpec((s, dv), lambda b, h: (b, h)),
            pl.BlockSpec((s, dv), lambda b, h: (b, h)),
            pl.BlockSpec((None, RET_CHUNK, RET_CHUNK), lambda b, h: (h, 0, 0)),
            pl.BlockSpec((None, RET_CHUNK, 1), lambda b, h: (h, 0, 0)),
            pl.BlockSpec((None, RET_CHUNK, 1), lambda b, h: (h, 0, 0)),
            pl.BlockSpec((None, 1, 1), lambda b, h: (h, 0, 0)),
            pl.BlockSpec((1, dv), lambda b, h: (0, h)),
            pl.BlockSpec((1, dv), lambda b, h: (0, h)),
        ],
        out_specs=pl.BlockSpec((s, dv), lambda b, h: (b, h)),
        out_shape=jax.ShapeDtypeStruct((t, v.shape[1]), BF16),
        scratch_shapes=[pltpu.VMEM((dk, dv), F32)],
        compiler_params=_params(2),
        name="ret_core",
    )(qk, qk, v, g, decay_inner, q_decay, k_decay, chunk_decay,
      gn_g.reshape(1, -1), gn_b.reshape(1, -1))


def _shifted_inputs(h_ref, hprev_ref, sc_ref, sh_ref, seq_start):
    sc, sh = sc_ref[...], sh_ref[...]
    xm = _rms_mod(h_ref[...], sc, sh)
    prev = _rms_mod(hprev_ref[...], sc, sh)
    last = prev[V7X_SUBLANES - 1:V7X_SUBLANES, :]
    last = jnp.where(seq_start, 0.0, last)
    row = lax.broadcasted_iota(jnp.int32, xm.shape, 0)
    shifted = jnp.where(row == 0, last, pltpu.roll(xm, 1, 0))
    return xm, shifted - xm


def _rwkv_rkv_kernel(h_ref, hprev_ref, sc_ref, sh_ref, mu_ref, w_ref, o_ref, xm_ref, dx_ref, xs_ref,
                     *, tiles_per_batch, tiles_per_proj):
    i, j = pl.program_id(0), pl.program_id(1)

    @pl.when(j == 0)
    def _():
        xm, dx = _shifted_inputs(h_ref, hprev_ref, sc_ref, sh_ref, (i % tiles_per_batch) == 0)
        xm_ref[...] = xm
        dx_ref[...] = dx

    @pl.when(j % tiles_per_proj == 0)
    def _():
        xs_ref[...] = (xm_ref[...] + dx_ref[...] * mu_ref[...]).astype(BF16)

    xs = xs_ref[...]
    tn = w_ref.shape[1]
    sub = min(tn, MATMUL_SUBTILE)
    for c in range(tn // sub):
        cols = slice(c * sub, (c + 1) * sub)
        o_ref[:, cols] = _dot(xs, w_ref[:, cols])


def _prev_rows_spec(tm, d):
    blocks = tm // V7X_SUBLANES
    return pl.BlockSpec((V7X_SUBLANES, d), lambda i, *_: (jnp.maximum(i * blocks - 1, 0), 0))


def _rwkv_rkv(h, mod, layer, batch, mu, w_rkv, tm=512, tn=2048):
    t, d = h.shape
    tm = min(tm, t // batch)
    tpb = (t // batch) // tm
    tpp = d // tn
    return pl.pallas_call(
        functools.partial(_rwkv_rkv_kernel, tiles_per_batch=tpb, tiles_per_proj=tpp),
        grid=(t // tm, 3 * tpp),
        in_specs=[
            pl.BlockSpec((tm, d), lambda i, j: (i, 0)),
            _prev_rows_spec(tm, d),
            _mod_spec(d, layer, 1, batch, tpb),
            _mod_spec(d, layer, 0, batch, tpb),
            pl.BlockSpec((None, 1, d), lambda i, j: (j // tpp, 0, 0)),
            pl.BlockSpec((None, d, tn), lambda i, j: (j // tpp, 0, j % tpp)),
        ],
        out_specs=pl.BlockSpec((tm, tn), lambda i, j: (i, j)),
        out_shape=jax.ShapeDtypeStruct((t, 3 * d), F32),
        scratch_shapes=[pltpu.VMEM((tm, d), F32), pltpu.VMEM((tm, d), F32), pltpu.VMEM((tm, d), BF16)],
        compiler_params=_params(2),
        name="rwkv_rkv",
    )(h, h, mod, mod, mu.reshape(-1, 1, d), w_rkv)


def _rwkv_lora_kernel(h_ref, hprev_ref, sc_ref, sh_ref, mu_ref, w0_ref, w1_ref, w2_ref,
                      a0_ref, a1_ref, a2_ref, g1_ref, g2_ref, lw_ref, a_ref, g_ref, *, tiles_per_batch):
    i = pl.program_id(0)
    xm, dx = _shifted_inputs(h_ref, hprev_ref, sc_ref, sh_ref, (i % tiles_per_batch) == 0)

    def mixed(p):
        return (xm + dx * mu_ref[p]).astype(BF16)

    lora_w = _dot(jnp.tanh(_dot(mixed(3), w1_ref[...])).astype(BF16), w2_ref[...])
    w_log = -jax.nn.softplus(-(w0_ref[...] + lora_w)) - RWKV_DECAY_OFFSET
    lw_ref[...] = -jnp.exp(w_log)
    lora_a = _dot(_dot(mixed(4), a1_ref[...]).astype(BF16), a2_ref[...])
    a_ref[...] = jax.nn.sigmoid(a0_ref[...] + lora_a)
    g_ref[...] = _dot(jax.nn.sigmoid(_dot(mixed(5), g1_ref[...])).astype(BF16), g2_ref[...])


def _rwkv_lora(h, mod, layer, batch, mu, w0, w1, w2, a0, a1, a2, g1, g2, tm=256):
    t, d = h.shape
    tm = min(tm, t // batch)
    tpb = (t // batch) // tm
    full = lambda a: pl.BlockSpec(a.shape, lambda i: (0,) * a.ndim)
    mu3 = mu.reshape(-1, 1, d)
    w0, a0 = w0.reshape(1, d), a0.reshape(1, d)
    consts = [mu3, w0, w1, w2, a0, a1, a2, g1, g2]
    return pl.pallas_call(
        functools.partial(_rwkv_lora_kernel, tiles_per_batch=tpb),
        grid=(t // tm,),
        in_specs=[
            pl.BlockSpec((tm, d), lambda i: (i, 0)),
            _prev_rows_spec(tm, d),
            _mod_spec(d, layer, 1, batch, tpb),
            _mod_spec(d, layer, 0, batch, tpb),
        ] + [full(a) for a in consts],
        out_specs=[pl.BlockSpec((tm, d), lambda i: (i, 0))] * 3,
        out_shape=[jax.ShapeDtypeStruct((t, d), F32)] * 3,
        compiler_params=_params(1),
        name="rwkv_lora",
    )(h, h, mod, mod, *consts)


def _split_dot(x, ones, terms):
    acc = None
    for _ in range(terms):
        piece = x.astype(BF16)
        part = _dot(piece, ones)
        acc = part if acc is None else acc + part
        x = x - piece.astype(F32)
    return acc


def _split_dot_left(ones, x, terms):
    acc = None
    for _ in range(terms):
        piece = x.astype(BF16)
        part = _dot(ones, piece)
        acc = part if acc is None else acc + part
        x = x - piece.astype(F32)
    return acc


def _rwkv_core_kernel(r_ref, k_ref, v_ref, lw_ref, a_ref, g_ref, kk_ref, ka_ref, rk_ref,
                      lng_ref, lnb_ref, o_ref, state_ref):
    c_len, w = RWKV_CHUNK, RWKV_LANES
    heads = w // RWKV_HEAD_DIM
    n = heads * c_len

    @pl.when(pl.program_id(2) == 0)
    def _():
        state_ref[...] = jnp.zeros_like(state_ref)

    rb = lax.broadcasted_iota(jnp.int32, (n, w), 0)
    lb = lax.broadcasted_iota(jnp.int32, (n, w), 1)
    head_match = (rb // c_len) == (lb // RWKV_HEAD_DIM)
    ones_bd = jnp.where((rb // RWKV_HEAD_DIM) == (lb // RWKV_HEAD_DIM), 1.0, 0.0).astype(BF16)
    tr = lax.broadcasted_iota(jnp.int32, (c_len, n), 0)
    ts = lax.broadcasted_iota(jnp.int32, (c_len, n), 1) % c_len
    strict = tr > ts
    incl = tr >= ts
    eye = jnp.where(tr == ts, 1.0, 0.0)
    cr = lax.broadcasted_iota(jnp.int32, (c_len, c_len), 0)
    cc = lax.broadcasted_iota(jnp.int32, (c_len, c_len), 1)
    tril_ones = jnp.where(cr >= cc, 1.0, 0.0).astype(BF16)

    def expand(x):
        xb = x.astype(BF16)
        return jnp.where(head_match, jnp.concatenate([xb] * heads, axis=0), jnp.zeros((), BF16))

    def seg_sum(x):
        return _split_dot(x, ones_bd, 2)

    def chunk(rows, grp):
        lanes = slice(grp * w, (grp + 1) * w)
        r, k, v = r_ref[rows, lanes], k_ref[rows, lanes], v_ref[rows, lanes]
        lw, a, g = lw_ref[rows, lanes], a_ref[rows, lanes], g_ref[rows, lanes]
        k_k, k_a, r_k = kk_ref[:, lanes], ka_ref[:, lanes], rk_ref[:, lanes]
        ln_g, ln_b = lng_ref[:, lanes], lnb_ref[:, lanes]

        kk = k * k_k
        kk_sq = seg_sum(kk * kk)
        kp = k * (1.0 + (a - 1.0) * k_a)
        rk_sum = seg_sum(r * kp * r_k)
        cum = _split_dot_left(tril_ones, lw, 3)
        yield
        kk = kk / jnp.maximum(jnp.sqrt(kk_sq), 1e-12)
        bonus = rk_sum * v
        cum_end = cum[c_len - 1:c_len, :]
        e_pos, e_neg = jnp.exp(cum), jnp.exp(-cum)
        e_tail = jnp.exp(cum_end - cum)
        ba = kk * a
        r_t = (r * e_pos).astype(BF16)
        a_t = (-kk * jnp.exp(cum - lw)).astype(BF16)
        b_t, k_t = ba * e_neg, kp * e_neg
        b_h, k_h = ba * e_tail, kp * e_tail
        vb = v.astype(BF16)

        gram = _dot_nt(jnp.concatenate([a_t, r_t], axis=0),
                       jnp.concatenate([expand(b_t), expand(k_t)], axis=0))
        yield
        ab = jnp.where(strict, gram[:c_len, :n], 0.0)
        ak = jnp.where(strict, gram[:c_len, n:], 0.0)
        rbm = jnp.where(incl, gram[c_len:, :n], 0.0)
        rkm = jnp.where(incl, gram[c_len:, n:], 0.0)

        tinv = eye + ab
        p = _dot(ab.astype(BF16), expand(ab))
        akv_rkv = _dot(jnp.concatenate([ak, rkm], axis=0).astype(BF16), expand(v))
        yield
        for _ in range(c_len.bit_length() - 3):
            both = _dot(jnp.concatenate([p, tinv], axis=0).astype(BF16), expand(p))
            yield
            p, tinv = both[:c_len], tinv + both[c_len:]
        last = _dot(tinv.astype(BF16), expand(p))
        state = state_ref[grp]
        ag_rg = _dot_nt(jnp.concatenate([a_t, r_t], axis=0), state.astype(BF16))
        yield
        tb = (tinv + last).astype(BF16)
        u = _dot(tb, expand(akv_rkv[:c_len] + ag_rg[:c_len]))
        yield
        o = _dot(rbm.astype(BF16), expand(u))
        upd = _dot_tn(jnp.concatenate([u.astype(BF16), vb], axis=0),
                      jnp.concatenate([b_h, k_h], axis=0).astype(BF16))
        yield
        o = ag_rg[c_len:] + o + akv_rkv[c_len:]
        state_ref[grp] = state * jnp.exp(cum_end) + jnp.where(
            (rb // RWKV_HEAD_DIM) == (lb // RWKV_HEAD_DIM), upd, 0.0)
        mean = seg_sum(o) * (1.0 / RWKV_HEAD_DIM)
        yield
        cen = o - mean
        var = seg_sum(cen * cen) * (1.0 / RWKV_HEAD_DIM)
        yield
        y = (cen * lax.rsqrt(var + RWKV_GN_EPS)) * ln_g + ln_b
        o_ref[rows, lanes] = ((y + bonus) * g).astype(BF16)

    def body(c, carry):
        rows = pl.ds(pl.multiple_of(c * c_len, c_len), c_len)
        pending = [chunk(rows, grp) for grp in range(state_ref.shape[0])]
        while pending:
            pending = [gen for gen in pending if next(gen, True) is None]
        return carry

    lax.fori_loop(0, r_ref.shape[0] // c_len, body, 0)


def _rwkv_core(rkv, lw, a, g, k_k, k_a, r_k, ln_g, ln_b, batch, rows=256, groups=8):
    t, d = lw.shape
    s = t // batch
    rows = min(rows, s)
    w = RWKV_LANES * groups
    nq = d // w
    spb = s // rows
    seq = lambda off: pl.BlockSpec((rows, w), lambda b, q, i: (b * spb + i, off + q))
    par = pl.BlockSpec((1, w), lambda b, q, i: (0, q))
    vec = lambda x: x.reshape(1, d)
    return pl.pallas_call(
        _rwkv_core_kernel,
        grid=(batch, nq, spb),
        in_specs=[seq(0), seq(nq), seq(2 * nq), seq(0), seq(0), seq(0), par, par, par, par, par],
        out_specs=seq(0),
        out_shape=jax.ShapeDtypeStruct((t, d), BF16),
        scratch_shapes=[pltpu.VMEM((groups, RWKV_LANES, RWKV_LANES), F32)],
        compiler_params=_params(3),
        name="rwkv_core",
    )(rkv, rkv, rkv, lw, a, g, vec(k_k), vec(k_a), vec(r_k), vec(ln_g), vec(ln_b))


def _pad_cols(x, n):
    return jnp.pad(x, [(0, 0)] * (x.ndim - 1) + [(0, n - x.shape[-1])])


def _pad_rows(x, n):
    return jnp.pad(x, [(0, n - x.shape[0])] + [(0, 0)] * (x.ndim - 1))


def _round_up(x, m):
    return -(-x // m) * m


def _interleave_val_gate(x, d_ff, fp):
    rows = x.shape[0]
    val = _pad_cols(x[:, :d_ff], fp).reshape(rows, fp // FFN_BLOCK, 1, FFN_BLOCK)
    gate = _pad_cols(x[:, d_ff:], fp).reshape(rows, fp // FFN_BLOCK, 1, FFN_BLOCK)
    return jnp.concatenate([val, gate], axis=2).reshape(rows, 2 * fp)


def kernel(x, c, positions, ada_w, ada_b, ffn_w_up, ffn_conv_w, ffn_conv_b, ffn_w_down, sg_w_in, sg_ln_g, sg_ln_b, sg_w_s, sg_b_s, sg_w_out, ret_w_in, ret_gn_g, ret_gn_b, ret_w_out, rwkv_mu, rwkv_w_rkv, rwkv_w0, rwkv_w1, rwkv_w2, rwkv_a0, rwkv_a1, rwkv_a2, rwkv_g1, rwkv_g2, rwkv_k_k, rwkv_k_a, rwkv_r_k, rwkv_ln_g, rwkv_ln_b, rwkv_w_out, final_norm_g):
    batch, seq, d = x.shape
    depth = ada_w.shape[0]
    t = batch * seq
    d_ff = ffn_w_down.shape[1]
    fp = _round_up(d_ff, FFN_SUBCHUNKS * FFN_BLOCK)

    mod = _modulation(c, ada_w, ada_b)
    h = x.reshape(t, d)
    cos = sin = None

    for layer in range(depth):
        kind, j = layer % N_MIXERS, layer // N_MIXERS
        if kind == 0:
            z = _norm_matmul(h, mod, layer, 1, 0, sg_w_in[j].astype(BF16), batch,
                             n=sg_w_in.shape[-1], epilogue="gelu")
            h = _sg_core(z, h, mod, layer, batch, sg_ln_g[j], sg_ln_b[j], sg_w_s[j], sg_b_s[j],
                         sg_w_out[j].astype(BF16))
        elif kind == 1:
            if cos is None:
                cos, sin = _rope_tables(positions, d // RET_HEADS // 2)
            w_in = ret_w_in[j].astype(BF16)
            qk = _norm_matmul(h, mod, layer, 1, 0, w_in, batch, n=2 * d, epilogue="rope", cos=cos, sin=sin)
            v = _norm_matmul(h, mod, layer, 1, 0, w_in, batch, n=2 * d, col_offset=2 * d, out_dtype=BF16)
            g = _norm_matmul(h, mod, layer, 1, 0, w_in, batch, n=2 * d, col_offset=4 * d)
            o = _ret_core(qk, v, g, ret_gn_g[j], ret_gn_b[j], batch)
            h = _proj_residual(o, ret_w_out[j].astype(BF16), h, mod, layer, 2, batch)
        else:
            rkv = _rwkv_rkv(h, mod, layer, batch, rwkv_mu[j], rwkv_w_rkv[j].astype(BF16))
            lora = _round_up(rwkv_w1.shape[-1], V7X_LANES)
            lora_a = _round_up(rwkv_a1.shape[-1], V7X_LANES)
            lw, a, g = _rwkv_lora(
                h, mod, layer, batch, rwkv_mu[j], rwkv_w0[j],
                _pad_cols(rwkv_w1[j], lora).astype(BF16), _pad_rows(rwkv_w2[j], lora).astype(BF16),
                rwkv_a0[j],
                _pad_cols(rwkv_a1[j], lora_a).astype(BF16), _pad_rows(rwkv_a2[j], lora_a).astype(BF16),
                rwkv_g1[j].astype(BF16), rwkv_g2[j].astype(BF16))
            o = _rwkv_core(rkv, lw, a, g, rwkv_k_k[j], rwkv_k_a[j], rwkv_r_k[j],
                           rwkv_ln_g[j], rwkv_ln_b[j], batch)
            h = _proj_residual(o, rwkv_w_out[j].astype(BF16), h, mod, layer, 2, batch)

        h = _conv_ffn(h, mod, layer, batch,
                      _interleave_val_gate(ffn_w_up[layer].astype(BF16), d_ff, fp),
                      _interleave_val_gate(ffn_conv_w[layer], d_ff, fp),
                      _interleave_val_gate(ffn_conv_b[layer].reshape(1, -1), d_ff, fp),
                      _pad_rows(ffn_w_down[layer].astype(BF16), fp),
                      final_gain=final_norm_g if layer == depth - 1 else None)

    return h.reshape(batch, seq, d)
```

```python
import functools

import jax
import jax.numpy as jnp
from jax import lax
from jax.experimental import pallas as pl
from jax.experimental.pallas import tpu as pltpu

F32 = jnp.float32
BF16 = jnp.bfloat16

NORM_EPS = 1e-6
LN_EPS = 1e-5
N_MOD = 6
N_MIXERS = 3

SG_CHUNK = 128
SG_GROUPS = 16

RET_HEADS = 8
RET_CHUNK = 128
ROPE_BASE = 10000.0

RWKV_HEAD_DIM = 64
RWKV_GN_EPS = RWKV_HEAD_DIM * 1e-5
RWKV_DECAY_OFFSET = 0.5
RWKV_CHUNK = 64
RWKV_LANES = 256

CONV_WIDTH = 3
FFN_SUB = 256
FFN_SUBCHUNKS = 2
MATMUL_SUBTILE = 512
WEIGHT_BLOCK_ELEMS = 2048 * 1024

V7X_LANES = 128
V7X_SUBLANES = 8
V7X_VMEM_LIMIT = 56 * 1024 * 1024


def _params(n_axes, vmem=V7X_VMEM_LIMIT):
    return pltpu.CompilerParams(dimension_semantics=("arbitrary",) * n_axes,
                                vmem_limit_bytes=vmem)


def _dot(a, b):
    return jnp.dot(a, b, preferred_element_type=F32)


def _dot_nt(a, b):
    return lax.dot_general(a, b, (((1,), (1,)), ((), ())), preferred_element_type=F32)


def _dot_tn(a, b):
    return lax.dot_general(a, b, (((0,), (0,)), ((), ())), preferred_element_type=F32)


def _rms_mod(h, sc, sh):
    ms = jnp.mean(h * h, axis=-1, keepdims=True)
    return (h * lax.rsqrt(ms + NORM_EPS)) * (1.0 + sc) + sh


def _mod_kernel(c_ref, w_ref, b_ref, o_ref):
    cond = jax.nn.silu(c_ref[...])
    o_ref[...] = _dot(cond.astype(BF16), w_ref[...].astype(BF16)) + b_ref[...]


def _modulation(c, ada_w, ada_b, tn=1024):
    depth, d, n = ada_w.shape
    b = c.shape[0]
    rows = -(-b // V7X_SUBLANES) * V7X_SUBLANES
    c_pad = jnp.pad(c, ((0, rows - b), (0, 0)))
    out = pl.pallas_call(
        _mod_kernel,
        grid=(depth, n // tn),
        in_specs=[
            pl.BlockSpec((rows, d), lambda l, j: (0, 0)),
            pl.BlockSpec((None, d, tn), lambda l, j: (l, 0, j)),
            pl.BlockSpec((None, 1, tn), lambda l, j: (l, 0, j)),
        ],
        out_specs=pl.BlockSpec((None, rows, tn), lambda l, j: (l, 0, j)),
        out_shape=jax.ShapeDtypeStruct((depth, rows, n), F32),
        compiler_params=_params(2),
        name="adaln_mod",
    )(c_pad, ada_w, ada_b.reshape(depth, 1, n))
    return out[:, :b].reshape(depth * b * N_MOD, 1, d)


def _mod_spec(d, layer, k, batch, tiles_per_batch):
    base = layer * batch * N_MOD + k
    return pl.BlockSpec((None, 1, d),
                        lambda i, *_: (base + (i // tiles_per_batch) * N_MOD, 0, 0))


def _norm_matmul_kernel(h_ref, sc_ref, sh_ref, w_ref, *rest, epilogue):
    if epilogue == "rope":
        cos_ref, sin_ref, o_ref, xm_ref = rest
    else:
        o_ref, xm_ref = rest

    @pl.when(pl.program_id(1) == 0)
    def _():
        xm_ref[...] = _rms_mod(h_ref[...], sc_ref[...], sh_ref[...]).astype(BF16)

    xm = xm_ref[...]
    tn = w_ref.shape[1]
    sub = min(tn, MATMUL_SUBTILE)
    for c in range(tn // sub):
        cols = slice(c * sub, (c + 1) * sub)
        y = _dot(xm, w_ref[:, cols])
        if epilogue == "gelu":
            y = jax.nn.gelu(y)
        elif epilogue == "rope":
            cos, sin = cos_ref[...], sin_ref[...]
            half = cos.shape[-1]
            parts = []
            for hd in range(sub // (2 * half)):
                x1 = y[:, 2 * hd * half:(2 * hd + 1) * half]
                x2 = y[:, (2 * hd + 1) * half:(2 * hd + 2) * half]
                parts += [x1 * cos - x2 * sin, x2 * cos + x1 * sin]
            y = jnp.concatenate(parts, axis=-1)
        o_ref[:, cols] = y.astype(o_ref.dtype)


def _norm_matmul(h, mod, layer, k_scale, k_shift, w, batch, *, n, col_offset=0, epilogue="none",
                 out_dtype=F32, cos=None, sin=None, tm=1024, tn=1024):
    t, d = h.shape
    tm = min(tm, t // batch)
    tpb = (t // batch) // tm
    off = col_offset // tn
    in_specs = [
        pl.BlockSpec((tm, d), lambda i, j: (i, 0)),
        _mod_spec(d, layer, k_scale, batch, tpb),
        _mod_spec(d, layer, k_shift, batch, tpb),
        pl.BlockSpec((d, tn), lambda i, j: (0, off + j)),
    ]
    args = [h, mod, mod, w]
    if epilogue == "rope":
        half = cos.shape[-1]
        in_specs += [pl.BlockSpec((tm, half), lambda i, j: (i, 0))] * 2
        args += [cos, sin]
    return pl.pallas_call(
        functools.partial(_norm_matmul_kernel, epilogue=epilogue),
        grid=(t // tm, n // tn),
        in_specs=in_specs,
        out_specs=pl.BlockSpec((tm, tn), lambda i, j: (i, j)),
        out_shape=jax.ShapeDtypeStruct((t, n), out_dtype),
        scratch_shapes=[pltpu.VMEM((tm, d), BF16)],
        compiler_params=_params(2),
        name="norm_matmul_" + epilogue,
    )(*args)


def _proj_residual_kernel(a_ref, w_ref, h_ref, g_ref, o_ref):
    a = a_ref[...]
    tn = w_ref.shape[1]
    sub = min(tn, MATMUL_SUBTILE)
    for c in range(tn // sub):
        cols = slice(c * sub, (c + 1) * sub)
        o_ref[:, cols] = h_ref[:, cols] + g_ref[:, cols] * _dot(a, w_ref[:, cols])


def _proj_residual(a, w, h, mod, layer, k_gate, batch, tm=1024, tn=1024):
    t, kdim = a.shape
    d = w.shape[1]
    tn = min(tn, WEIGHT_BLOCK_ELEMS // kdim)
    tm = min(tm, t // batch)
    tpb = (t // batch) // tm
    base = layer * batch * N_MOD + k_gate
    return pl.pallas_call(
        _proj_residual_kernel,
        grid=(t // tm, d // tn),
        in_specs=[
            pl.BlockSpec((tm, kdim), lambda i, j: (i, 0)),
            pl.BlockSpec((kdim, tn), lambda i, j: (0, j)),
            pl.BlockSpec((tm, tn), lambda i, j: (i, j)),
            pl.BlockSpec((None, 1, tn), lambda i, j: (base + (i // tpb) * N_MOD, 0, j)),
        ],
        out_specs=pl.BlockSpec((tm, tn), lambda i, j: (i, j)),
        out_shape=jax.ShapeDtypeStruct((t, d), F32),
        compiler_params=_params(2),
        name="proj_residual",
    )(a, w, h, mod)


def _sg_core_kernel(u_ref, v_ref, h_ref, g_ref, lng_ref, lnb_ref, ws_ref, bs_ref, wo_ref,
                    o_ref, gated_ref):
    tm, width = v_ref.shape
    gdim = width // SG_GROUPS
    v = v_ref[...]
    mu = jnp.mean(v, axis=-1, keepdims=True)
    var = jnp.mean(jnp.square(v - mu), axis=-1, keepdims=True)
    vn = (((v - mu) * lax.rsqrt(var + LN_EPS)) * lng_ref[...] + lnb_ref[...]).astype(BF16)
    row = lax.broadcasted_iota(jnp.int32, (SG_CHUNK, SG_CHUNK), 0)
    col = lax.broadcasted_iota(jnp.int32, (SG_CHUNK, SG_CHUNK), 1)
    causal = row >= col
    for g in range(SG_GROUPS):
        w_causal = jnp.where(causal, ws_ref[g], 0.0).astype(BF16)
        bias = bs_ref[:, g:g + 1]
        cols = slice(g * gdim, (g + 1) * gdim)
        for c in range(tm // SG_CHUNK):
            rows = slice(c * SG_CHUNK, (c + 1) * SG_CHUNK)
            sv = _dot(w_causal, vn[rows, cols]) + bias
            gated_ref[rows, cols] = (u_ref[rows, cols] * sv).astype(BF16)
    o_ref[...] = h_ref[...] + g_ref[...] * _dot(gated_ref[...], wo_ref[...])


def _sg_core(z, h, mod, layer, batch, ln_g, ln_b, w_s, b_s, w_out, tm=256):
    t, d = h.shape
    width = z.shape[1] // 2
    tm = min(tm, t // batch)
    tpb = (t // batch) // tm
    return pl.pallas_call(
        _sg_core_kernel,
        grid=(t // tm,),
        in_specs=[
            pl.BlockSpec((tm, width), lambda i: (i, 0)),
            pl.BlockSpec((tm, width), lambda i: (i, 1)),
            pl.BlockSpec((tm, d), lambda i: (i, 0)),
            _mod_spec(d, layer, 2, batch, tpb),
            pl.BlockSpec((1, width), lambda i: (0, 0)),
            pl.BlockSpec((1, width), lambda i: (0, 0)),
            pl.BlockSpec((SG_GROUPS, SG_CHUNK, SG_CHUNK), lambda i: (0, 0, 0)),
            pl.BlockSpec((SG_CHUNK, SG_GROUPS), lambda i: (0, 0)),
            pl.BlockSpec((width, d), lambda i: (0, 0)),
        ],
        out_specs=pl.BlockSpec((tm, d), lambda i: (i, 0)),
        out_shape=jax.ShapeDtypeStruct((t, d), F32),
        scratch_shapes=[pltpu.VMEM((tm, width), BF16)],
        compiler_params=_params(1),
        name="sg_core",
    )(z, z, h, mod, ln_g.reshape(1, width), ln_b.reshape(1, width), w_s, b_s.T, w_out)


def _ffn_kernel(h_ref, sc_ref, sh_ref, g_ref, wv_ref, wg_ref, cwv_ref, cwg_ref, cbv_ref, cbg_ref,
                wd_ref, *rest, tiles_per_batch, final_norm):
    if final_norm:
        fg_ref, o_ref, xm_ref, halo_ref = rest[:4]
    else:
        o_ref, xm_ref, halo_ref = rest[:3]
    hbufs = rest[-FFN_SUBCHUNKS:]
    i, j = pl.program_id(0), pl.program_id(1)
    tm = h_ref.shape[0]
    pad = V7X_SUBLANES
    sub = FFN_SUB

    @pl.when(j == 0)
    def _():
        xm_ref[...] = _rms_mod(h_ref[...], sc_ref[...], sh_ref[...]).astype(BF16)
        o_ref[...] = jnp.zeros_like(o_ref)

    seq_start = (i % tiles_per_batch) == 0
    xm = xm_ref[...]
    for k, hb in enumerate(hbufs):
        cols = slice(k * sub, (k + 1) * sub)
        hb[0:pad, :] = jnp.where(seq_start, 0.0, halo_ref[j, k])
        hb[pad:, 0:sub] = _dot(xm, wv_ref[:, cols])
        hb[pad:, sub:] = _dot(xm, wg_ref[:, cols])
        halo_ref[j, k] = hb[tm:, :]

    def conv(hb, lanes, cw, cb):
        acc = hb[pad - 2:pad - 2 + tm, lanes] * cw[0:1, :]
        acc = acc + hb[pad - 1:pad - 1 + tm, lanes] * cw[1:2, :]
        acc = acc + hb[pad:pad + tm, lanes] * cw[2:3, :]
        return acc + cb

    for k, hb in enumerate(hbufs):
        cols = slice(k * sub, (k + 1) * sub)
        val = conv(hb, slice(0, sub), cwv_ref[:, cols], cbv_ref[:, cols])
        gate = conv(hb, slice(sub, 2 * sub), cwg_ref[:, cols], cbg_ref[:, cols])
        act = (jax.nn.silu(gate) * val).astype(BF16)
        o_ref[...] += _dot(act, wd_ref[cols, :])

    @pl.when(j == pl.num_programs(1) - 1)
    def _():
        out = h_ref[...] + g_ref[...] * o_ref[...]
        if final_norm:
            ms = jnp.mean(out * out, axis=-1, keepdims=True)
            out = (out * lax.rsqrt(ms + NORM_EPS)) * fg_ref[...]
        o_ref[...] = out


def _conv_ffn(h, mod, layer, batch, w_up, conv_w, conv_b, w_down, final_gain=None, tm=512):
    t, d = h.shape
    fp = w_down.shape[0]
    tf = FFN_SUBCHUNKS * FFN_SUB
    nf = fp // tf
    tm = min(tm, t // batch)
    tpb = (t // batch) // tm
    final_norm = final_gain is not None
    in_specs = [
        pl.BlockSpec((tm, d), lambda i, j: (i, 0)),
        _mod_spec(d, layer, 4, batch, tpb),
        _mod_spec(d, layer, 3, batch, tpb),
        _mod_spec(d, layer, 5, batch, tpb),
        pl.BlockSpec((d, tf), lambda i, j: (0, j)),
        pl.BlockSpec((d, tf), lambda i, j: (0, nf + j)),
        pl.BlockSpec((CONV_WIDTH, tf), lambda i, j: (0, j)),
        pl.BlockSpec((CONV_WIDTH, tf), lambda i, j: (0, nf + j)),
        pl.BlockSpec((1, tf), lambda i, j: (0, j)),
        pl.BlockSpec((1, tf), lambda i, j: (0, nf + j)),
        pl.BlockSpec((tf, d), lambda i, j: (j, 0)),
    ]
    args = [h, mod, mod, mod, w_up, w_up, conv_w, conv_w, conv_b, conv_b, w_down]
    if final_norm:
        in_specs.append(pl.BlockSpec((1, d), lambda i, j: (0, 0)))
        args.append(final_gain.reshape(1, d))
    return pl.pallas_call(
        functools.partial(_ffn_kernel, tiles_per_batch=tpb, final_norm=final_norm),
        grid=(t // tm, nf),
        in_specs=in_specs,
        out_specs=pl.BlockSpec((tm, d), lambda i, j: (i, 0)),
        out_shape=jax.ShapeDtypeStruct((t, d), F32),
        scratch_shapes=[
            pltpu.VMEM((tm, d), BF16),
            pltpu.VMEM((nf, FFN_SUBCHUNKS, V7X_SUBLANES, 2 * FFN_SUB), F32),
        ] + [pltpu.VMEM((tm + V7X_SUBLANES, 2 * FFN_SUB), F32)] * FFN_SUBCHUNKS,
        compiler_params=_params(2),
        name="conv_ffn",
    )(*args)


def _rope_table_kernel(pos_ref, freq_ref, cos_ref, sin_ref):
    ang = pos_ref[...].astype(F32) * freq_ref[...]
    cos_ref[...] = jnp.cos(ang)
    sin_ref[...] = jnp.sin(ang)


def _rope_tables(positions, half, tm=1024):
    t = positions.size
    inv_freq = ROPE_BASE ** (-jnp.arange(half, dtype=F32) / half)
    tm = min(tm, t)
    return pl.pallas_call(
        _rope_table_kernel,
        grid=(t // tm,),
        in_specs=[pl.BlockSpec((tm, 1), lambda i: (i, 0)),
                  pl.BlockSpec((1, half), lambda i: (0, 0))],
        out_specs=[pl.BlockSpec((tm, half), lambda i: (i, 0))] * 2,
        out_shape=[jax.ShapeDtypeStruct((t, half), F32)] * 2,
        compiler_params=_params(1),
        name="rope_tables",
    )(positions.reshape(t, 1), inv_freq.reshape(1, half))


def _ret_core_kernel(q_ref, k_ref, v_ref, g_ref, di_ref, qd_ref, kd_ref, cd_ref, gg_ref, gb_ref,
                     o_ref, state_ref, *, k_scale):
    n_heads, dk, dv = state_ref.shape

    @pl.when(pl.program_id(2) == 0)
    def _():
        state_ref[...] = jnp.zeros_like(state_ref)

    def chunk(rows, hd):
        kcols = slice(hd * dk, (hd + 1) * dk)
        vcols = slice(hd * dv, (hd + 1) * dv)
        q = q_ref[rows, kcols]
        k = k_ref[rows, kcols] * k_scale
        v = v_ref[rows, vcols]
        state = state_ref[hd]
        inner = _dot_nt(q.astype(BF16), k.astype(BF16))
        cross = _dot((q * qd_ref[hd]).astype(BF16), state.astype(BF16))
        update = _dot_tn((k * kd_ref[hd]).astype(BF16), v)
        yield
        state_ref[hd] = state * cd_ref[hd] + update
        out = _dot((inner * di_ref[hd]).astype(BF16), v)
        yield
        out = out + cross
        mu = jnp.mean(out, axis=-1, keepdims=True)
        var = jnp.mean(jnp.square(out - mu), axis=-1, keepdims=True)
        y = ((out - mu) * lax.rsqrt(var + NORM_EPS)) * gg_ref[:, vcols] + gb_ref[:, vcols]
        o_ref[rows, vcols] = (jax.nn.silu(g_ref[rows, vcols]) * y).astype(BF16)

    def body(c, carry):
        rows = pl.ds(pl.multiple_of(c * RET_CHUNK, RET_CHUNK), RET_CHUNK)
        pending = [chunk(rows, hd) for hd in range(n_heads)]
        while pending:
            pending = [gen for gen in pending if next(gen, True) is None]
        return carry

    lax.fori_loop(0, q_ref.shape[0] // RET_CHUNK, body, 0)


def _ret_core(qk, v, g, gn_g, gn_b, batch, rows=512, group=4):
    t, two_d = qk.shape
    d = two_d // 2
    s = t // batch
    rows = min(rows, s)
    spb = s // rows
    n_groups = RET_HEADS // group
    dk = d // RET_HEADS
    dv = v.shape[1] // RET_HEADS
    log_gamma = jnp.log1p(-jnp.exp2(-5.0 - jnp.arange(RET_HEADS, dtype=F32)))
    idx = jnp.arange(RET_CHUNK, dtype=F32)
    rel = idx[:, None] - idx[None, :]
    decay_inner = jnp.where(rel >= 0, jnp.exp(log_gamma[:, None, None] * jnp.maximum(rel, 0.0)), 0.0)
    q_decay = jnp.exp(log_gamma[:, None] * (idx + 1.0))[..., None]
    k_decay = jnp.exp(log_gamma[:, None] * (RET_CHUNK - 1.0 - idx))[..., None]
    chunk_decay = jnp.exp(log_gamma * RET_CHUNK)[:, None, None]
    return pl.pallas_call(
        functools.partial(_ret_core_kernel, k_scale=dk ** -0.5),
        grid=(batch, n_groups, spb),
        in_specs=[
            pl.BlockSpec((rows, group * dk), lambda b, h, i: (b * spb + i, h)),
            pl.BlockSpec((rows, group * dk), lambda b, h, i: (b * spb + i, n_groups + h)),
            pl.BlockSpec((rows, group * dv), lambda b, h, i: (b * spb + i, h)),
            pl.BlockSpec((rows, group * dv), lambda b, h, i: (b * spb + i, h)),
            pl.BlockSpec((group, RET_CHUNK, RET_CHUNK), lambda b, h, i: (h, 0, 0)),
            pl.BlockSpec((group, RET_CHUNK, 1), lambda b, h, i: (h, 0, 0)),
            pl.BlockSpec((group, RET_CHUNK, 1), lambda b, h, i: (h, 0, 0)),
            pl.BlockSpec((group, 1, 1), lambda b, h, i: (h, 0, 0)),
            pl.BlockSpec((1, group * dv), lambda b, h, i: (0, h)),
            pl.BlockSpec((1, group * dv), lambda b, h, i: (0, h)),
        ],
        out_specs=pl.BlockSpec((rows, group * dv), lambda b, h, i: (b * spb + i, h)),
        out_shape=jax.ShapeDtypeStruct((t, v.shape[1]), BF16),
        scratch_shapes=[pltpu.VMEM((group, dk, dv), F32)],
        compiler_params=_params(3),
        name="ret_core",
    )(qk, qk, v, g, decay_inner, q_decay, k_decay, chunk_decay,
      gn_g.reshape(1, -1), gn_b.reshape(1, -1))


def _shifted_inputs(h_ref, hprev_ref, sc_ref, sh_ref, seq_start):
    sc, sh = sc_ref[...], sh_ref[...]
    xm = _rms_mod(h_ref[...], sc, sh)
    prev = _rms_mod(hprev_ref[...], sc, sh)
    last = prev[V7X_SUBLANES - 1:V7X_SUBLANES, :]
    last = jnp.where(seq_start, 0.0, last)
    row = lax.broadcasted_iota(jnp.int32, xm.shape, 0)
    shifted = jnp.where(row == 0, last, pltpu.roll(xm, 1, 0))
    return xm, shifted - xm


def _rwkv_rkv_kernel(h_ref, hprev_ref, sc_ref, sh_ref, mu_ref, w_ref, o_ref, xm_ref, dx_ref, xs_ref,
                     *, tiles_per_batch, tiles_per_proj):
    i, j = pl.program_id(0), pl.program_id(1)

    @pl.when(j == 0)
    def _():
        xm, dx = _shifted_inputs(h_ref, hprev_ref, sc_ref, sh_ref, (i % tiles_per_batch) == 0)
        xm_ref[...] = xm
        dx_ref[...] = dx

    @pl.when(j % tiles_per_proj == 0)
    def _():
        xs_ref[...] = (xm_ref[...] + dx_ref[...] * mu_ref[...]).astype(BF16)

    xs = xs_ref[...]
    tn = w_ref.shape[1]
    sub = min(tn, MATMUL_SUBTILE)
    for c in range(tn // sub):
        cols = slice(c * sub, (c + 1) * sub)
        o_ref[:, cols] = _dot(xs, w_ref[:, cols])


def _prev_rows_spec(tm, d):
    blocks = tm // V7X_SUBLANES
    return pl.BlockSpec((V7X_SUBLANES, d), lambda i, *_: (jnp.maximum(i * blocks - 1, 0), 0))


def _rwkv_rkv(h, mod, layer, batch, mu, w_rkv, tm=512, tn=2048):
    t, d = h.shape
    tm = min(tm, t // batch)
    tpb = (t // batch) // tm
    tpp = d // tn
    return pl.pallas_call(
        functools.partial(_rwkv_rkv_kernel, tiles_per_batch=tpb, tiles_per_proj=tpp),
        grid=(t // tm, 3 * tpp),
        in_specs=[
            pl.BlockSpec((tm, d), lambda i, j: (i, 0)),
            _prev_rows_spec(tm, d),
            _mod_spec(d, layer, 1, batch, tpb),
            _mod_spec(d, layer, 0, batch, tpb),
            pl.BlockSpec((None, 1, d), lambda i, j: (j // tpp, 0, 0)),
            pl.BlockSpec((None, d, tn), lambda i, j: (j // tpp, 0, j % tpp)),
        ],
        out_specs=pl.BlockSpec((tm, tn), lambda i, j: (i, j)),
        out_shape=jax.ShapeDtypeStruct((t, 3 * d), F32),
        scratch_shapes=[pltpu.VMEM((tm, d), F32), pltpu.VMEM((tm, d), F32), pltpu.VMEM((tm, d), BF16)],
        compiler_params=_params(2),
        name="rwkv_rkv",
    )(h, h, mod, mod, mu.reshape(-1, 1, d), w_rkv)


def _rwkv_lora_kernel(h_ref, hprev_ref, sc_ref, sh_ref, mu_ref, w0_ref, w1_ref, w2_ref,
                      a0_ref, a1_ref, a2_ref, g1_ref, g2_ref, lw_ref, a_ref, g_ref, *, tiles_per_batch):
    i = pl.program_id(0)
    xm, dx = _shifted_inputs(h_ref, hprev_ref, sc_ref, sh_ref, (i % tiles_per_batch) == 0)

    def mixed(p):
        return (xm + dx * mu_ref[p]).astype(BF16)

    lora_w = _dot(jnp.tanh(_dot(mixed(3), w1_ref[...])).astype(BF16), w2_ref[...])
    w_log = -jax.nn.softplus(-(w0_ref[...] + lora_w)) - RWKV_DECAY_OFFSET
    lw_ref[...] = -jnp.exp(w_log)
    lora_a = _dot(_dot(mixed(4), a1_ref[...]).astype(BF16), a2_ref[...])
    a_ref[...] = jax.nn.sigmoid(a0_ref[...] + lora_a)
    g_ref[...] = _dot(jax.nn.sigmoid(_dot(mixed(5), g1_ref[...])).astype(BF16), g2_ref[...])


def _rwkv_lora(h, mod, layer, batch, mu, w0, w1, w2, a0, a1, a2, g1, g2, tm=256):
    t, d = h.shape
    tm = min(tm, t // batch)
    tpb = (t // batch) // tm
    full = lambda a: pl.BlockSpec(a.shape, lambda i: (0,) * a.ndim)
    mu3 = mu.reshape(-1, 1, d)
    w0, a0 = w0.reshape(1, d), a0.reshape(1, d)
    consts = [mu3, w0, w1, w2, a0, a1, a2, g1, g2]
    return pl.pallas_call(
        functools.partial(_rwkv_lora_kernel, tiles_per_batch=tpb),
        grid=(t // tm,),
        in_specs=[
            pl.BlockSpec((tm, d), lambda i: (i, 0)),
            _prev_rows_spec(tm, d),
            _mod_spec(d, layer, 1, batch, tpb),
            _mod_spec(d, layer, 0, batch, tpb),
        ] + [full(a) for a in consts],
        out_specs=[pl.BlockSpec((tm, d), lambda i: (i, 0))] * 3,
        out_shape=[jax.ShapeDtypeStruct((t, d), F32)] * 3,
        compiler_params=_params(1),
        name="rwkv_lora",
    )(h, h, mod, mod, *consts)


def _split_dot(x, ones, terms):
    acc = None
    for _ in range(terms):
        piece = x.astype(BF16)
        part = _dot(piece, ones)
        acc = part if acc is None else acc + part
        x = x - piece.astype(F32)
    return acc


def _split_dot_left(ones, x, terms):
    acc = None
    for _ in range(terms):
        piece = x.astype(BF16)
        part = _dot(ones, piece)
        acc = part if acc is None else acc + part
        x = x - piece.astype(F32)
    return acc


def _rwkv_core_kernel(r_ref, k_ref, v_ref, lw_ref, a_ref, g_ref, kk_ref, ka_ref, rk_ref,
                      lng_ref, lnb_ref, o_ref, state_ref):
    c_len, w = RWKV_CHUNK, RWKV_LANES
    heads = w // RWKV_HEAD_DIM
    n = heads * c_len

    @pl.when(pl.program_id(2) == 0)
    def _():
        state_ref[...] = jnp.zeros_like(state_ref)

    rb = lax.broadcasted_iota(jnp.int32, (n, w), 0)
    lb = lax.broadcasted_iota(jnp.int32, (n, w), 1)
    head_match = (rb // c_len) == (lb // RWKV_HEAD_DIM)
    ones_bd = jnp.where((rb // RWKV_HEAD_DIM) == (lb // RWKV_HEAD_DIM), 1.0, 0.0).astype(BF16)
    tr = lax.broadcasted_iota(jnp.int32, (c_len, n), 0)
    ts = lax.broadcasted_iota(jnp.int32, (c_len, n), 1) % c_len
    strict = tr > ts
    incl = tr >= ts
    eye = jnp.where(tr == ts, 1.0, 0.0)
    cr = lax.broadcasted_iota(jnp.int32, (c_len, c_len), 0)
    cc = lax.broadcasted_iota(jnp.int32, (c_len, c_len), 1)
    tril_ones = jnp.where(cr >= cc, 1.0, 0.0).astype(BF16)

    def expand(x):
        xb = x.astype(BF16)
        return jnp.where(head_match, jnp.concatenate([xb] * heads, axis=0), jnp.zeros((), BF16))

    def seg_sum(x):
        return _split_dot(x, ones_bd, 2)

    def chunk(rows, grp):
        lanes = slice(grp * w, (grp + 1) * w)
        r, k, v = r_ref[rows, lanes], k_ref[rows, lanes], v_ref[rows, lanes]
        lw, a, g = lw_ref[rows, lanes], a_ref[rows, lanes], g_ref[rows, lanes]
        k_k, k_a, r_k = kk_ref[:, lanes], ka_ref[:, lanes], rk_ref[:, lanes]
        ln_g, ln_b = lng_ref[:, lanes], lnb_ref[:, lanes]

        kk = k * k_k
        kk_sq = seg_sum(kk * kk)
        kp = k * (1.0 + (a - 1.0) * k_a)
        rk_sum = seg_sum(r * kp * r_k)
        cum = _split_dot_left(tril_ones, lw, 3)
        yield
        kk = kk / jnp.maximum(jnp.sqrt(kk_sq), 1e-12)
        bonus = rk_sum * v
        cum_end = cum[c_len - 1:c_len, :]
        e_pos, e_neg = jnp.exp(cum), jnp.exp(-cum)
        e_tail = jnp.exp(cum_end - cum)
        ba = kk * a
        r_t = (r * e_pos).astype(BF16)
        a_t = (-kk * jnp.exp(cum - lw)).astype(BF16)
        b_t, k_t = ba * e_neg, kp * e_neg
        b_h, k_h = ba * e_tail, kp * e_tail
        vb = v.astype(BF16)

        gram = _dot_nt(jnp.concatenate([a_t, r_t], axis=0),
                       jnp.concatenate([expand(b_t), expand(k_t)], axis=0))
        yield
        ab = jnp.where(strict, gram[:c_len, :n], 0.0)
        ak = jnp.where(strict, gram[:c_len, n:], 0.0)
        rbm = jnp.where(incl, gram[c_len:, :n], 0.0)
        rkm = jnp.where(incl, gram[c_len:, n:], 0.0)

        tinv = eye + ab
        p = _dot(ab.astype(BF16), expand(ab))
        akv_rkv = _dot(jnp.concatenate([ak, rkm], axis=0).astype(BF16), expand(v))
        yield
        for _ in range(c_len.bit_length() - 3):
            both = _dot(jnp.concatenate([p, tinv], axis=0).astype(BF16), expand(p))
            yield
            p, tinv = both[:c_len], tinv + both[c_len:]
        last = _dot(tinv.astype(BF16), expand(p))
        state = state_ref[grp]
        ag_rg = _dot_nt(jnp.concatenate([a_t, r_t], axis=0), state.astype(BF16))
        yield
        tb = (tinv + last).astype(BF16)
        u = _dot(tb, expand(akv_rkv[:c_len] + ag_rg[:c_len]))
        yield
        o = _dot(rbm.astype(BF16), expand(u))
        upd = _dot_tn(jnp.concatenate([u.astype(BF16), vb], axis=0),
                      jnp.concatenate([b_h, k_h], axis=0).astype(BF16))
        yield
        o = ag_rg[c_len:] + o + akv_rkv[c_len:]
        state_ref[grp] = state * jnp.exp(cum_end) + jnp.where(
            (rb // RWKV_HEAD_DIM) == (lb // RWKV_HEAD_DIM), upd, 0.0)
        mean = seg_sum(o) * (1.0 / RWKV_HEAD_DIM)
        yield
        cen = o - mean
        var = seg_sum(cen * cen) * (1.0 / RWKV_HEAD_DIM)
        yield
        y = (cen * lax.rsqrt(var + RWKV_GN_EPS)) * ln_g + ln_b
        o_ref[rows, lanes] = ((y + bonus) * g).astype(BF16)

    def body(c, carry):
        rows = pl.ds(pl.multiple_of(c * c_len, c_len), c_len)
        pending = [chunk(rows, grp) for grp in range(state_ref.shape[0])]
        while pending:
            pending = [gen for gen in pending if next(gen, True) is None]
        return carry

    lax.fori_loop(0, r_ref.shape[0] // c_len, body, 0)


def _rwkv_core(rkv, lw, a, g, k_k, k_a, r_k, ln_g, ln_b, batch, rows=256, groups=8):
    t, d = lw.shape
    s = t // batch
    rows = min(rows, s)
    w = RWKV_LANES * groups
    nq = d // w
    spb = s // rows
    seq = lambda off: pl.BlockSpec((rows, w), lambda b, q, i: (b * spb + i, off + q))
    par = pl.BlockSpec((1, w), lambda b, q, i: (0, q))
    vec = lambda x: x.reshape(1, d)
    return pl.pallas_call(
        _rwkv_core_kernel,
        grid=(batch, nq, spb),
        in_specs=[seq(0), seq(nq), seq(2 * nq), seq(0), seq(0), seq(0), par, par, par, par, par],
        out_specs=seq(0),
        out_shape=jax.ShapeDtypeStruct((t, d), BF16),
        scratch_shapes=[pltpu.VMEM((groups, RWKV_LANES, RWKV_LANES), F32)],
        compiler_params=_params(3),
        name="rwkv_core",
    )(rkv, rkv, rkv, lw, a, g, vec(k_k), vec(k_a), vec(r_k), vec(ln_g), vec(ln_b))


def _pad_cols(x, n):
    return jnp.pad(x, [(0, 0)] * (x.ndim - 1) + [(0, n - x.shape[-1])])


def _pad_rows(x, n):
    return jnp.pad(x, [(0, n - x.shape[0])] + [(0, 0)] * (x.ndim - 1))


def _round_up(x, m):
    return -(-x // m) * m


def _pad_val_gate(x, d_ff, fp):
    return jnp.concatenate([_pad_cols(x[:, :d_ff], fp), _pad_cols(x[:, d_ff:], fp)], axis=1)


def kernel(x, c, positions, ada_w, ada_b, ffn_w_up, ffn_conv_w, ffn_conv_b, ffn_w_down, sg_w_in, sg_ln_g, sg_ln_b, sg_w_s, sg_b_s, sg_w_out, ret_w_in, ret_gn_g, ret_gn_b, ret_w_out, rwkv_mu, rwkv_w_rkv, rwkv_w0, rwkv_w1, rwkv_w2, rwkv_a0, rwkv_a1, rwkv_a2, rwkv_g1, rwkv_g2, rwkv_k_k, rwkv_k_a, rwkv_r_k, rwkv_ln_g, rwkv_ln_b, rwkv_w_out, final_norm_g):
    batch, seq, d = x.shape
    depth = ada_w.shape[0]
    t = batch * seq
    d_ff = ffn_w_down.shape[1]
    fp = _round_up(d_ff, FFN_SUBCHUNKS * FFN_SUB)

    mod = _modulation(c, ada_w, ada_b)
    h = x.reshape(t, d)
    cos = sin = None

    for layer in range(depth):
        kind, j = layer % N_MIXERS, layer // N_MIXERS
        if kind == 0:
            z = _norm_matmul(h, mod, layer, 1, 0, sg_w_in[j].astype(BF16), batch,
                             n=sg_w_in.shape[-1], epilogue="gelu")
            h = _sg_core(z, h, mod, layer, batch, sg_ln_g[j], sg_ln_b[j], sg_w_s[j], sg_b_s[j],
                         sg_w_out[j].astype(BF16))
        elif kind == 1:
            if cos is None:
                cos, sin = _rope_tables(positions, d // RET_HEADS // 2)
            w_in = ret_w_in[j].astype(BF16)
            qk = _norm_matmul(h, mod, layer, 1, 0, w_in, batch, n=2 * d, epilogue="rope", cos=cos, sin=sin)
            v = _norm_matmul(h, mod, layer, 1, 0, w_in, batch, n=2 * d, col_offset=2 * d, out_dtype=BF16)
            g = _norm_matmul(h, mod, layer, 1, 0, w_in, batch, n=2 * d, col_offset=4 * d)
            o = _ret_core(qk, v, g, ret_gn_g[j], ret_gn_b[j], batch)
            h = _proj_residual(o, ret_w_out[j].astype(BF16), h, mod, layer, 2, batch)
        else:
            rkv = _rwkv_rkv(h, mod, layer, batch, rwkv_mu[j], rwkv_w_rkv[j].astype(BF16))
            lora = _round_up(rwkv_w1.shape[-1], V7X_LANES)
            lora_a = _round_up(rwkv_a1.shape[-1], V7X_LANES)
            lw, a, g = _rwkv_lora(
                h, mod, layer, batch, rwkv_mu[j], rwkv_w0[j],
                _pad_cols(rwkv_w1[j], lora).astype(BF16), _pad_rows(rwkv_w2[j], lora).astype(BF16),
                rwkv_a0[j],
                _pad_cols(rwkv_a1[j], lora_a).astype(BF16), _pad_rows(rwkv_a2[j], lora_a).astype(BF16),
                rwkv_g1[j].astype(BF16), rwkv_g2[j].astype(BF16))
            o = _rwkv_core(rkv, lw, a, g, rwkv_k_k[j], rwkv_k_a[j], rwkv_r_k[j],
                           rwkv_ln_g[j], rwkv_ln_b[j], batch)
            h = _proj_residual(o, rwkv_w_out[j].astype(BF16), h, mod, layer, 2, batch)

        h = _conv_ffn(h, mod, layer, batch,
                      _pad_val_gate(ffn_w_up[layer].astype(BF16), d_ff, fp),
                      _pad_val_gate(ffn_conv_w[layer], d_ff, fp),
                      _pad_val_gate(ffn_conv_b[layer].reshape(1, -1), d_ff, fp),
                      _pad_rows(ffn_w_down[layer].astype(BF16), fp),
                      final_gain=final_norm_g if layer == depth - 1 else None)

    return h.reshape(batch, seq, d)
```

```python
import functools

import jax
import jax.numpy as jnp
from jax import lax
from jax.experimental import pallas as pl
from jax.experimental.pallas import tpu as pltpu

F32 = jnp.float32
BF16 = jnp.bfloat16

NORM_EPS = 1e-6
LN_EPS = 1e-5
N_MOD = 6
N_MIXERS = 3

SG_CHUNK = 128
SG_GROUPS = 16

RET_HEADS = 8
RET_CHUNK = 128
ROPE_BASE = 10000.0

RWKV_HEAD_DIM = 64
RWKV_GN_EPS = RWKV_HEAD_DIM * 1e-5
RWKV_DECAY_OFFSET = 0.5
RWKV_CHUNK = 64
RWKV_LANES = 256

CONV_WIDTH = 3
FFN_SUB = 256
FFN_SUBCHUNKS = 2
MATMUL_SUBTILE = 512
WEIGHT_BLOCK_ELEMS = 2048 * 1024

V7X_LANES = 128
V7X_SUBLANES = 8
V7X_VMEM_LIMIT = 56 * 1024 * 1024


def _params(n_axes, vmem=V7X_VMEM_LIMIT):
    return pltpu.CompilerParams(dimension_semantics=("arbitrary",) * n_axes,
                                vmem_limit_bytes=vmem)


def _dot(a, b):
    return jnp.dot(a, b, preferred_element_type=F32)


def _dot_nt(a, b):
    return lax.dot_general(a, b, (((1,), (1,)), ((), ())), preferred_element_type=F32)


def _dot_tn(a, b):
    return lax.dot_general(a, b, (((0,), (0,)), ((), ())), preferred_element_type=F32)


def _rms_mod(h, sc, sh):
    ms = jnp.mean(h * h, axis=-1, keepdims=True)
    return (h * lax.rsqrt(ms + NORM_EPS)) * (1.0 + sc) + sh


def _mod_kernel(c_ref, w_ref, b_ref, o_ref):
    cond = jax.nn.silu(c_ref[...])
    o_ref[...] = _dot(cond.astype(BF16), w_ref[...].astype(BF16)) + b_ref[...]


def _modulation(c, ada_w, ada_b, tn=1024):
    depth, d, n = ada_w.shape
    b = c.shape[0]
    rows = -(-b // V7X_SUBLANES) * V7X_SUBLANES
    c_pad = jnp.pad(c, ((0, rows - b), (0, 0)))
    out = pl.pallas_call(
        _mod_kernel,
        grid=(depth, n // tn),
        in_specs=[
            pl.BlockSpec((rows, d), lambda l, j: (0, 0)),
            pl.BlockSpec((None, d, tn), lambda l, j: (l, 0, j)),
            pl.BlockSpec((None, 1, tn), lambda l, j: (l, 0, j)),
        ],
        out_specs=pl.BlockSpec((None, rows, tn), lambda l, j: (l, 0, j)),
        out_shape=jax.ShapeDtypeStruct((depth, rows, n), F32),
        compiler_params=_params(2),
        name="adaln_mod",
    )(c_pad, ada_w, ada_b.reshape(depth, 1, n))
    return out[:, :b].reshape(depth * b * N_MOD, 1, d)


def _mod_spec(d, layer, k, batch, tiles_per_batch):
    base = layer * batch * N_MOD + k
    return pl.BlockSpec((None, 1, d),
                        lambda i, *_: (base + (i // tiles_per_batch) * N_MOD, 0, 0))


def _norm_matmul_kernel(h_ref, sc_ref, sh_ref, w_ref, *rest, epilogue):
    if epilogue == "rope":
        cos_ref, sin_ref, o_ref, xm_ref = rest
    else:
        o_ref, xm_ref = rest

    @pl.when(pl.program_id(1) == 0)
    def _():
        xm_ref[...] = _rms_mod(h_ref[...], sc_ref[...], sh_ref[...]).astype(BF16)

    xm = xm_ref[...]
    tn = w_ref.shape[1]
    sub = min(tn, MATMUL_SUBTILE)
    for c in range(tn // sub):
        cols = slice(c * sub, (c + 1) * sub)
        y = _dot(xm, w_ref[:, cols])
        if epilogue == "gelu":
            y = jax.nn.gelu(y)
        elif epilogue == "rope":
            cos, sin = cos_ref[...], sin_ref[...]
            half = cos.shape[-1]
            parts = []
            for hd in range(sub // (2 * half)):
                x1 = y[:, 2 * hd * half:(2 * hd + 1) * half]
                x2 = y[:, (2 * hd + 1) * half:(2 * hd + 2) * half]
                parts += [x1 * cos - x2 * sin, x2 * cos + x1 * sin]
            y = jnp.concatenate(parts, axis=-1)
        o_ref[:, cols] = y.astype(o_ref.dtype)


def _norm_matmul(h, mod, layer, k_scale, k_shift, w, w_idx, batch, *, n, col_offset=0, epilogue="none",
                 out_dtype=F32, cos=None, sin=None, tm=1024, tn=1024):
    t, d = h.shape
    tm = min(tm, t // batch)
    tpb = (t // batch) // tm
    off = col_offset // tn
    in_specs = [
        pl.BlockSpec((tm, d), lambda i, j: (i, 0)),
        _mod_spec(d, layer, k_scale, batch, tpb),
        _mod_spec(d, layer, k_shift, batch, tpb),
        pl.BlockSpec((None, d, tn), lambda i, j: (w_idx, 0, off + j)),
    ]
    args = [h, mod, mod, w]
    if epilogue == "rope":
        half = cos.shape[-1]
        in_specs += [pl.BlockSpec((tm, half), lambda i, j: (i, 0))] * 2
        args += [cos, sin]
    return pl.pallas_call(
        functools.partial(_norm_matmul_kernel, epilogue=epilogue),
        grid=(t // tm, n // tn),
        in_specs=in_specs,
        out_specs=pl.BlockSpec((tm, tn), lambda i, j: (i, j)),
        out_shape=jax.ShapeDtypeStruct((t, n), out_dtype),
        scratch_shapes=[pltpu.VMEM((tm, d), BF16)],
        compiler_params=_params(2),
        name="norm_matmul_" + epilogue,
    )(*args)


def _proj_residual_kernel(a_ref, w_ref, h_ref, g_ref, o_ref):
    a = a_ref[...]
    tn = w_ref.shape[1]
    sub = min(tn, MATMUL_SUBTILE)
    for c in range(tn // sub):
        cols = slice(c * sub, (c + 1) * sub)
        o_ref[:, cols] = h_ref[:, cols] + g_ref[:, cols] * _dot(a, w_ref[:, cols])


def _proj_residual(a, w, w_idx, h, mod, layer, k_gate, batch, tm=1024, tn=1024):
    t, kdim = a.shape
    d = w.shape[-1]
    tn = min(tn, WEIGHT_BLOCK_ELEMS // kdim)
    tm = min(tm, t // batch)
    tpb = (t // batch) // tm
    base = layer * batch * N_MOD + k_gate
    return pl.pallas_call(
        _proj_residual_kernel,
        grid=(t // tm, d // tn),
        in_specs=[
            pl.BlockSpec((tm, kdim), lambda i, j: (i, 0)),
            pl.BlockSpec((None, kdim, tn), lambda i, j: (w_idx, 0, j)),
            pl.BlockSpec((tm, tn), lambda i, j: (i, j)),
            pl.BlockSpec((None, 1, tn), lambda i, j: (base + (i // tpb) * N_MOD, 0, j)),
        ],
        out_specs=pl.BlockSpec((tm, tn), lambda i, j: (i, j)),
        out_shape=jax.ShapeDtypeStruct((t, d), F32),
        compiler_params=_params(2),
        name="proj_residual",
    )(a, w, h, mod)


def _sg_core_kernel(u_ref, v_ref, h_ref, g_ref, lng_ref, lnb_ref, ws_ref, bs_ref, wo_ref,
                    o_ref, gated_ref):
    tm, width = v_ref.shape
    gdim = width // SG_GROUPS
    v = v_ref[...]
    mu = jnp.mean(v, axis=-1, keepdims=True)
    var = jnp.mean(jnp.square(v - mu), axis=-1, keepdims=True)
    vn = (((v - mu) * lax.rsqrt(var + LN_EPS)) * lng_ref[...] + lnb_ref[...]).astype(BF16)
    row = lax.broadcasted_iota(jnp.int32, (SG_CHUNK, SG_CHUNK), 0)
    col = lax.broadcasted_iota(jnp.int32, (SG_CHUNK, SG_CHUNK), 1)
    causal = row >= col
    for g in range(SG_GROUPS):
        w_causal = jnp.where(causal, ws_ref[g], 0.0).astype(BF16)
        bias = bs_ref[:, g:g + 1]
        cols = slice(g * gdim, (g + 1) * gdim)
        for c in range(tm // SG_CHUNK):
            rows = slice(c * SG_CHUNK, (c + 1) * SG_CHUNK)
            sv = _dot(w_causal, vn[rows, cols]) + bias
            gated_ref[rows, cols] = (u_ref[rows, cols] * sv).astype(BF16)
    o_ref[...] = h_ref[...] + g_ref[...] * _dot(gated_ref[...], wo_ref[...])


def _sg_core(z, h, mod, layer, batch, ln_g, ln_b, w_s, b_s, w_out, w_idx, tm=256):
    t, d = h.shape
    width = z.shape[1] // 2
    tm = min(tm, t // batch)
    tpb = (t // batch) // tm
    return pl.pallas_call(
        _sg_core_kernel,
        grid=(t // tm,),
        in_specs=[
            pl.BlockSpec((tm, width), lambda i: (i, 0)),
            pl.BlockSpec((tm, width), lambda i: (i, 1)),
            pl.BlockSpec((tm, d), lambda i: (i, 0)),
            _mod_spec(d, layer, 2, batch, tpb),
            pl.BlockSpec((1, width), lambda i: (0, 0)),
            pl.BlockSpec((1, width), lambda i: (0, 0)),
            pl.BlockSpec((SG_GROUPS, SG_CHUNK, SG_CHUNK), lambda i: (0, 0, 0)),
            pl.BlockSpec((SG_CHUNK, SG_GROUPS), lambda i: (0, 0)),
            pl.BlockSpec((None, width, d), lambda i: (w_idx, 0, 0)),
        ],
        out_specs=pl.BlockSpec((tm, d), lambda i: (i, 0)),
        out_shape=jax.ShapeDtypeStruct((t, d), F32),
        scratch_shapes=[pltpu.VMEM((tm, width), BF16)],
        compiler_params=_params(1),
        name="sg_core",
    )(z, z, h, mod, ln_g.reshape(1, width), ln_b.reshape(1, width), w_s, b_s.T, w_out)


def _ffn_kernel(h_ref, sc_ref, sh_ref, g_ref, wv_ref, wg_ref, cwv_ref, cwg_ref, cbv_ref, cbg_ref,
                wd_ref, *rest, tiles_per_batch, final_norm):
    if final_norm:
        fg_ref, o_ref, xm_ref, halo_ref = rest[:4]
    else:
        o_ref, xm_ref, halo_ref = rest[:3]
    hbufs = rest[-FFN_SUBCHUNKS:]
    i, j = pl.program_id(0), pl.program_id(1)
    tm = h_ref.shape[0]
    pad = V7X_SUBLANES
    sub = FFN_SUB

    @pl.when(j == 0)
    def _():
        xm_ref[...] = _rms_mod(h_ref[...], sc_ref[...], sh_ref[...]).astype(BF16)
        o_ref[...] = jnp.zeros_like(o_ref)

    seq_start = (i % tiles_per_batch) == 0
    xm = xm_ref[...]
    for k, hb in enumerate(hbufs):
        cols = slice(k * sub, (k + 1) * sub)
        hb[0:pad, :] = jnp.where(seq_start, 0.0, halo_ref[j, k])
        hb[pad:, 0:sub] = _dot(xm, wv_ref[:, cols])
        hb[pad:, sub:] = _dot(xm, wg_ref[:, cols])
        halo_ref[j, k] = hb[tm:, :]

    def conv(hb, lanes, cw, cb):
        acc = hb[pad - 2:pad - 2 + tm, lanes] * cw[0:1, :]
        acc = acc + hb[pad - 1:pad - 1 + tm, lanes] * cw[1:2, :]
        acc = acc + hb[pad:pad + tm, lanes] * cw[2:3, :]
        return acc + cb

    for k, hb in enumerate(hbufs):
        cols = slice(k * sub, (k + 1) * sub)
        val = conv(hb, slice(0, sub), cwv_ref[:, cols], cbv_ref[:, cols])
        gate = conv(hb, slice(sub, 2 * sub), cwg_ref[:, cols], cbg_ref[:, cols])
        act = (jax.nn.silu(gate) * val).astype(BF16)
        o_ref[...] += _dot(act, wd_ref[cols, :])

    @pl.when(j == pl.num_programs(1) - 1)
    def _():
        out = h_ref[...] + g_ref[...] * o_ref[...]
        if final_norm:
            ms = jnp.mean(out * out, axis=-1, keepdims=True)
            out = (out * lax.rsqrt(ms + NORM_EPS)) * fg_ref[...]
        o_ref[...] = out


def _conv_ffn(h, mod, layer, batch, w_up, conv_w, conv_b, w_down, final_gain=None, tm=512):
    t, d = h.shape
    fp = w_down.shape[1]
    tf = FFN_SUBCHUNKS * FFN_SUB
    nf = fp // tf
    tm = min(tm, t // batch)
    tpb = (t // batch) // tm
    final_norm = final_gain is not None
    in_specs = [
        pl.BlockSpec((tm, d), lambda i, j: (i, 0)),
        _mod_spec(d, layer, 4, batch, tpb),
        _mod_spec(d, layer, 3, batch, tpb),
        _mod_spec(d, layer, 5, batch, tpb),
        pl.BlockSpec((None, d, tf), lambda i, j: (layer, 0, j)),
        pl.BlockSpec((None, d, tf), lambda i, j: (layer, 0, nf + j)),
        pl.BlockSpec((None, CONV_WIDTH, tf), lambda i, j: (layer, 0, j)),
        pl.BlockSpec((None, CONV_WIDTH, tf), lambda i, j: (layer, 0, nf + j)),
        pl.BlockSpec((None, 1, tf), lambda i, j: (layer, 0, j)),
        pl.BlockSpec((None, 1, tf), lambda i, j: (layer, 0, nf + j)),
        pl.BlockSpec((None, tf, d), lambda i, j: (layer, j, 0)),
    ]
    args = [h, mod, mod, mod, w_up, w_up, conv_w, conv_w, conv_b, conv_b, w_down]
    if final_norm:
        in_specs.append(pl.BlockSpec((1, d), lambda i, j: (0, 0)))
        args.append(final_gain.reshape(1, d))
    return pl.pallas_call(
        functools.partial(_ffn_kernel, tiles_per_batch=tpb, final_norm=final_norm),
        grid=(t // tm, nf),
        in_specs=in_specs,
        out_specs=pl.BlockSpec((tm, d), lambda i, j: (i, 0)),
        out_shape=jax.ShapeDtypeStruct((t, d), F32),
        scratch_shapes=[
            pltpu.VMEM((tm, d), BF16),
            pltpu.VMEM((nf, FFN_SUBCHUNKS, V7X_SUBLANES, 2 * FFN_SUB), F32),
        ] + [pltpu.VMEM((tm + V7X_SUBLANES, 2 * FFN_SUB), F32)] * FFN_SUBCHUNKS,
        compiler_params=_params(2),
        name="conv_ffn",
    )(*args)


def _rope_table_kernel(pos_ref, freq_ref, cos_ref, sin_ref):
    ang = pos_ref[...].astype(F32) * freq_ref[...]
    cos_ref[...] = jnp.cos(ang)
    sin_ref[...] = jnp.sin(ang)


def _rope_tables(positions, half, tm=1024):
    t = positions.size
    inv_freq = ROPE_BASE ** (-jnp.arange(half, dtype=F32) / half)
    tm = min(tm, t)
    return pl.pallas_call(
        _rope_table_kernel,
        grid=(t // tm,),
        in_specs=[pl.BlockSpec((tm, 1), lambda i: (i, 0)),
                  pl.BlockSpec((1, half), lambda i: (0, 0))],
        out_specs=[pl.BlockSpec((tm, half), lambda i: (i, 0))] * 2,
        out_shape=[jax.ShapeDtypeStruct((t, half), F32)] * 2,
        compiler_params=_params(1),
        name="rope_tables",
    )(positions.reshape(t, 1), inv_freq.reshape(1, half))


def _ret_core_kernel(q_ref, k_ref, v_ref, g_ref, di_ref, qd_ref, kd_ref, cd_ref, gg_ref, gb_ref,
                     o_ref, state_ref, *, k_scale):
    n_heads, dk, dv = state_ref.shape

    @pl.when(pl.program_id(2) == 0)
    def _():
        state_ref[...] = jnp.zeros_like(state_ref)

    def chunk(rows, hd):
        kcols = slice(hd * dk, (hd + 1) * dk)
        vcols = slice(hd * dv, (hd + 1) * dv)
        q = q_ref[rows, kcols]
        k = k_ref[rows, kcols] * k_scale
        v = v_ref[rows, vcols]
        state = state_ref[hd]
        inner = _dot_nt(q.astype(BF16), k.astype(BF16))
        cross = _dot((q * qd_ref[hd]).astype(BF16), state.astype(BF16))
        update = _dot_tn((k * kd_ref[hd]).astype(BF16), v)
        yield
        state_ref[hd] = state * cd_ref[hd] + update
        out = _dot((inner * di_ref[hd]).astype(BF16), v)
        yield
        out = out + cross
        mu = jnp.mean(out, axis=-1, keepdims=True)
        var = jnp.mean(jnp.square(out - mu), axis=-1, keepdims=True)
        y = ((out - mu) * lax.rsqrt(var + NORM_EPS)) * gg_ref[:, vcols] + gb_ref[:, vcols]
        o_ref[rows, vcols] = (jax.nn.silu(g_ref[rows, vcols]) * y).astype(BF16)

    def body(c, carry):
        rows = pl.ds(pl.multiple_of(c * RET_CHUNK, RET_CHUNK), RET_CHUNK)
        pending = [chunk(rows, hd) for hd in range(n_heads)]
        while pending:
            pending = [gen for gen in pending if next(gen, True) is None]
        return carry

    lax.fori_loop(0, q_ref.shape[0] // RET_CHUNK, body, 0)


def _ret_core(qk, v, g, gn_g, gn_b, batch, rows=512, group=4):
    t, two_d = qk.shape
    d = two_d // 2
    s = t // batch
    rows = min(rows, s)
    spb = s // rows
    n_groups = RET_HEADS // group
    dk = d // RET_HEADS
    dv = v.shape[1] // RET_HEADS
    log_gamma = jnp.log1p(-jnp.exp2(-5.0 - jnp.arange(RET_HEADS, dtype=F32)))
    idx = jnp.arange(RET_CHUNK, dtype=F32)
    rel = idx[:, None] - idx[None, :]
    decay_inner = jnp.where(rel >= 0, jnp.exp(log_gamma[:, None, None] * jnp.maximum(rel, 0.0)), 0.0)
    q_decay = jnp.exp(log_gamma[:, None] * (idx + 1.0))[..., None]
    k_decay = jnp.exp(log_gamma[:, None] * (RET_CHUNK - 1.0 - idx))[..., None]
    chunk_decay = jnp.exp(log_gamma * RET_CHUNK)[:, None, None]
    return pl.pallas_call(
        functools.partial(_ret_core_kernel, k_scale=dk ** -0.5),
        grid=(batch, n_groups, spb),
        in_specs=[
            pl.BlockSpec((rows, group * dk), lambda b, h, i: (b * spb + i, h)),
            pl.BlockSpec((rows, group * dk), lambda b, h, i: (b * spb + i, n_groups + h)),
            pl.BlockSpec((rows, group * dv), lambda b, h, i: (b * spb + i, h)),
            pl.BlockSpec((rows, group * dv), lambda b, h, i: (b * spb + i, h)),
            pl.BlockSpec((group, RET_CHUNK, RET_CHUNK), lambda b, h, i: (h, 0, 0)),
            pl.BlockSpec((group, RET_CHUNK, 1), lambda b, h, i: (h, 0, 0)),
            pl.BlockSpec((group, RET_CHUNK, 1), lambda b, h, i: (h, 0, 0)),
            pl.BlockSpec((group, 1, 1), lambda b, h, i: (h, 0, 0)),
            pl.BlockSpec((1, group * dv), lambda b, h, i: (0, h)),
            pl.BlockSpec((1, group * dv), lambda b, h, i: (0, h)),
        ],
        out_specs=pl.BlockSpec((rows, group * dv), lambda b, h, i: (b * spb + i, h)),
        out_shape=jax.ShapeDtypeStruct((t, v.shape[1]), BF16),
        scratch_shapes=[pltpu.VMEM((group, dk, dv), F32)],
        compiler_params=_params(3),
        name="ret_core",
    )(qk, qk, v, g, decay_inner, q_decay, k_decay, chunk_decay,
      gn_g.reshape(1, -1), gn_b.reshape(1, -1))


def _shifted_inputs(h_ref, hprev_ref, sc_ref, sh_ref, seq_start):
    sc, sh = sc_ref[...], sh_ref[...]
    xm = _rms_mod(h_ref[...], sc, sh)
    prev = _rms_mod(hprev_ref[...], sc, sh)
    last = prev[V7X_SUBLANES - 1:V7X_SUBLANES, :]
    last = jnp.where(seq_start, 0.0, last)
    row = lax.broadcasted_iota(jnp.int32, xm.shape, 0)
    shifted = jnp.where(row == 0, last, pltpu.roll(xm, 1, 0))
    return xm, shifted - xm


def _rwkv_rkv_kernel(h_ref, hprev_ref, sc_ref, sh_ref, mu_ref, w_ref, o_ref, xm_ref, dx_ref, xs_ref,
                     *, tiles_per_batch, tiles_per_proj):
    i, j = pl.program_id(0), pl.program_id(1)

    @pl.when(j == 0)
    def _():
        xm, dx = _shifted_inputs(h_ref, hprev_ref, sc_ref, sh_ref, (i % tiles_per_batch) == 0)
        xm_ref[...] = xm
        dx_ref[...] = dx

    @pl.when(j % tiles_per_proj == 0)
    def _():
        xs_ref[...] = (xm_ref[...] + dx_ref[...] * mu_ref[...]).astype(BF16)

    xs = xs_ref[...]
    tn = w_ref.shape[1]
    sub = min(tn, MATMUL_SUBTILE)
    for c in range(tn // sub):
        cols = slice(c * sub, (c + 1) * sub)
        o_ref[:, cols] = _dot(xs, w_ref[:, cols])


def _prev_rows_spec(tm, d):
    blocks = tm // V7X_SUBLANES
    return pl.BlockSpec((V7X_SUBLANES, d), lambda i, *_: (jnp.maximum(i * blocks - 1, 0), 0))


def _rwkv_rkv(h, mod, layer, batch, mu, w_rkv, w_idx, tm=512, tn=2048):
    t, d = h.shape
    tm = min(tm, t // batch)
    tpb = (t // batch) // tm
    tpp = d // tn
    return pl.pallas_call(
        functools.partial(_rwkv_rkv_kernel, tiles_per_batch=tpb, tiles_per_proj=tpp),
        grid=(t // tm, 3 * tpp),
        in_specs=[
            pl.BlockSpec((tm, d), lambda i, j: (i, 0)),
            _prev_rows_spec(tm, d),
            _mod_spec(d, layer, 1, batch, tpb),
            _mod_spec(d, layer, 0, batch, tpb),
            pl.BlockSpec((None, 1, d), lambda i, j: (j // tpp, 0, 0)),
            pl.BlockSpec((None, None, d, tn), lambda i, j: (w_idx, j // tpp, 0, j % tpp)),
        ],
        out_specs=pl.BlockSpec((tm, tn), lambda i, j: (i, j)),
        out_shape=jax.ShapeDtypeStruct((t, 3 * d), F32),
        scratch_shapes=[pltpu.VMEM((tm, d), F32), pltpu.VMEM((tm, d), F32), pltpu.VMEM((tm, d), BF16)],
        compiler_params=_params(2),
        name="rwkv_rkv",
    )(h, h, mod, mod, mu.reshape(-1, 1, d), w_rkv)


def _rwkv_lora_kernel(h_ref, hprev_ref, sc_ref, sh_ref, mu_ref, w0_ref, w1_ref, w2_ref,
                      a0_ref, a1_ref, a2_ref, g1_ref, g2_ref, lw_ref, a_ref, g_ref, *, tiles_per_batch):
    i = pl.program_id(0)
    xm, dx = _shifted_inputs(h_ref, hprev_ref, sc_ref, sh_ref, (i % tiles_per_batch) == 0)

    def mixed(p):
        return (xm + dx * mu_ref[p]).astype(BF16)

    lora_w = _dot(jnp.tanh(_dot(mixed(3), w1_ref[...])).astype(BF16), w2_ref[...])
    w_log = -jax.nn.softplus(-(w0_ref[...] + lora_w)) - RWKV_DECAY_OFFSET
    lw_ref[...] = -jnp.exp(w_log)
    lora_a = _dot(_dot(mixed(4), a1_ref[...]).astype(BF16), a2_ref[...])
    a_ref[...] = jax.nn.sigmoid(a0_ref[...] + lora_a)
    g_ref[...] = _dot(jax.nn.sigmoid(_dot(mixed(5), g1_ref[...])).astype(BF16), g2_ref[...])


def _rwkv_lora(h, mod, layer, batch, mu, w0, w1, w2, a0, a1, a2, g1, g2, tm=256):
    t, d = h.shape
    tm = min(tm, t // batch)
    tpb = (t // batch) // tm
    full = lambda a: pl.BlockSpec(a.shape, lambda i: (0,) * a.ndim)
    mu3 = mu.reshape(-1, 1, d)
    w0, a0 = w0.reshape(1, d), a0.reshape(1, d)
    consts = [mu3, w0, w1, w2, a0, a1, a2, g1, g2]
    return pl.pallas_call(
        functools.partial(_rwkv_lora_kernel, tiles_per_batch=tpb),
        grid=(t // tm,),
        in_specs=[
            pl.BlockSpec((tm, d), lambda i: (i, 0)),
            _prev_rows_spec(tm, d),
            _mod_spec(d, layer, 1, batch, tpb),
            _mod_spec(d, layer, 0, batch, tpb),
        ] + [full(a) for a in consts],
        out_specs=[pl.BlockSpec((tm, d), lambda i: (i, 0))] * 3,
        out_shape=[jax.ShapeDtypeStruct((t, d), F32)] * 3,
        compiler_params=_params(1),
        name="rwkv_lora",
    )(h, h, mod, mod, *consts)


def _split_dot(x, ones, terms):
    acc = None
    for _ in range(terms):
        piece = x.astype(BF16)
        part = _dot(piece, ones)
        acc = part if acc is None else acc + part
        x = x - piece.astype(F32)
    return acc


def _split_dot_left(ones, x, terms):
    acc = None
    for _ in range(terms):
        piece = x.astype(BF16)
        part = _dot(ones, piece)
        acc = part if acc is None else acc + part
        x = x - piece.astype(F32)
    return acc


def _rwkv_core_kernel(r_ref, k_ref, v_ref, lw_ref, a_ref, g_ref, kk_ref, ka_ref, rk_ref,
                      lng_ref, lnb_ref, o_ref, state_ref):
    c_len, w = RWKV_CHUNK, RWKV_LANES
    heads = w // RWKV_HEAD_DIM
    n = heads * c_len

    @pl.when(pl.program_id(2) == 0)
    def _():
        state_ref[...] = jnp.zeros_like(state_ref)

    rb = lax.broadcasted_iota(jnp.int32, (n, w), 0)
    lb = lax.broadcasted_iota(jnp.int32, (n, w), 1)
    head_match = (rb // c_len) == (lb // RWKV_HEAD_DIM)
    ones_bd = jnp.where((rb // RWKV_HEAD_DIM) == (lb // RWKV_HEAD_DIM), 1.0, 0.0).astype(BF16)
    tr = lax.broadcasted_iota(jnp.int32, (c_len, n), 0)
    ts = lax.broadcasted_iota(jnp.int32, (c_len, n), 1) % c_len
    strict = tr > ts
    incl = tr >= ts
    eye = jnp.where(tr == ts, 1.0, 0.0)
    cr = lax.broadcasted_iota(jnp.int32, (c_len, c_len), 0)
    cc = lax.broadcasted_iota(jnp.int32, (c_len, c_len), 1)
    tril_ones = jnp.where(cr >= cc, 1.0, 0.0).astype(BF16)

    def expand(x):
        xb = x.astype(BF16)
        return jnp.where(head_match, jnp.concatenate([xb] * heads, axis=0), jnp.zeros((), BF16))

    def seg_sum(x):
        return _split_dot(x, ones_bd, 2)

    def chunk(rows, grp):
        lanes = slice(grp * w, (grp + 1) * w)
        r, k, v = r_ref[rows, lanes], k_ref[rows, lanes], v_ref[rows, lanes]
        lw, a, g = lw_ref[rows, lanes], a_ref[rows, lanes], g_ref[rows, lanes]
        k_k, k_a, r_k = kk_ref[:, lanes], ka_ref[:, lanes], rk_ref[:, lanes]
        ln_g, ln_b = lng_ref[:, lanes], lnb_ref[:, lanes]

        kk = k * k_k
        kk_sq = seg_sum(kk * kk)
        kp = k * (1.0 + (a - 1.0) * k_a)
        rk_sum = seg_sum(r * kp * r_k)
        cum = _split_dot_left(tril_ones, lw, 3)
        yield
        kk = kk / jnp.maximum(jnp.sqrt(kk_sq), 1e-12)
        bonus = rk_sum * v
        cum_end = cum[c_len - 1:c_len, :]
        e_pos, e_neg = jnp.exp(cum), jnp.exp(-cum)
        e_tail = jnp.exp(cum_end - cum)
        ba = kk * a
        r_t = (r * e_pos).astype(BF16)
        a_t = (-kk * jnp.exp(cum - lw)).astype(BF16)
        b_t, k_t = ba * e_neg, kp * e_neg
        b_h, k_h = ba * e_tail, kp * e_tail
        vb = v.astype(BF16)

        gram = _dot_nt(jnp.concatenate([a_t, r_t], axis=0),
                       jnp.concatenate([expand(b_t), expand(k_t)], axis=0))
        yield
        ab = jnp.where(strict, gram[:c_len, :n], 0.0)
        ak = jnp.where(strict, gram[:c_len, n:], 0.0)
        rbm = jnp.where(incl, gram[c_len:, :n], 0.0)
        rkm = jnp.where(incl, gram[c_len:, n:], 0.0)

        tinv = eye + ab
        p = _dot(ab.astype(BF16), expand(ab))
        akv_rkv = _dot(jnp.concatenate([ak, rkm], axis=0).astype(BF16), expand(v))
        yield
        for _ in range(c_len.bit_length() - 3):
            both = _dot(jnp.concatenate([p, tinv], axis=0).astype(BF16), expand(p))
            yield
            p, tinv = both[:c_len], tinv + both[c_len:]
        last = _dot(tinv.astype(BF16), expand(p))
        state = state_ref[grp]
        ag_rg = _dot_nt(jnp.concatenate([a_t, r_t], axis=0), state.astype(BF16))
        yield
        tb = (tinv + last).astype(BF16)
        u = _dot(tb, expand(akv_rkv[:c_len] + ag_rg[:c_len]))
        yield
        o = _dot(rbm.astype(BF16), expand(u))
        upd = _dot_tn(jnp.concatenate([u.astype(BF16), vb], axis=0),
                      jnp.concatenate([b_h, k_h], axis=0).astype(BF16))
        yield
        o = ag_rg[c_len:] + o + akv_rkv[c_len:]
        state_ref[grp] = state * jnp.exp(cum_end) + jnp.where(
            (rb // RWKV_HEAD_DIM) == (lb // RWKV_HEAD_DIM), upd, 0.0)
        mean = seg_sum(o) * (1.0 / RWKV_HEAD_DIM)
        yield
        cen = o - mean
        var = seg_sum(cen * cen) * (1.0 / RWKV_HEAD_DIM)
        yield
        y = (cen * lax.rsqrt(var + RWKV_GN_EPS)) * ln_g + ln_b
        o_ref[rows, lanes] = ((y + bonus) * g).astype(BF16)

    def body(c, carry):
        rows = pl.ds(pl.multiple_of(c * c_len, c_len), c_len)
        pending = [chunk(rows, grp) for grp in range(state_ref.shape[0])]
        while pending:
            pending = [gen for gen in pending if next(gen, True) is None]
        return carry

    lax.fori_loop(0, r_ref.shape[0] // c_len, body, 0)


def _rwkv_core(rkv, lw, a, g, k_k, k_a, r_k, ln_g, ln_b, batch, rows=256, groups=8):
    t, d = lw.shape
    s = t // batch
    rows = min(rows, s)
    w = RWKV_LANES * groups
    nq = d // w
    spb = s // rows
    seq = lambda off: pl.BlockSpec((rows, w), lambda b, q, i: (b * spb + i, off + q))
    par = pl.BlockSpec((1, w), lambda b, q, i: (0, q))
    vec = lambda x: x.reshape(1, d)
    return pl.pallas_call(
        _rwkv_core_kernel,
        grid=(batch, nq, spb),
        in_specs=[seq(0), seq(nq), seq(2 * nq), seq(0), seq(0), seq(0), par, par, par, par, par],
        out_specs=seq(0),
        out_shape=jax.ShapeDtypeStruct((t, d), BF16),
        scratch_shapes=[pltpu.VMEM((groups, RWKV_LANES, RWKV_LANES), F32)],
        compiler_params=_params(3),
        name="rwkv_core",
    )(rkv, rkv, rkv, lw, a, g, vec(k_k), vec(k_a), vec(r_k), vec(ln_g), vec(ln_b))


def _pad_cols(x, n):
    return jnp.pad(x, [(0, 0)] * (x.ndim - 1) + [(0, n - x.shape[-1])])


def _pad_rows(x, n):
    return jnp.pad(x, [(0, n - x.shape[0])] + [(0, 0)] * (x.ndim - 1))


def _round_up(x, m):
    return -(-x // m) * m


def _cast_pad_halves_kernel(x_ref, o_ref, *, half, padded):
    zeros = jnp.zeros((o_ref.shape[0], padded - half), o_ref.dtype)
    for k in range(2):
        o_ref[:, k * padded:k * padded + half] = x_ref[:, k * half:(k + 1) * half].astype(o_ref.dtype)
        o_ref[:, k * padded + half:(k + 1) * padded] = zeros


def _cast_pad_halves(x, half, padded, rows=128):
    n, r, _ = x.shape
    return pl.pallas_call(
        functools.partial(_cast_pad_halves_kernel, half=half, padded=padded),
        grid=(n, r // rows),
        in_specs=[pl.BlockSpec((None, rows, 2 * half), lambda l, i: (l, i, 0))],
        out_specs=pl.BlockSpec((None, rows, 2 * padded), lambda l, i: (l, i, 0)),
        out_shape=jax.ShapeDtypeStruct((n, r, 2 * padded), BF16),
        compiler_params=_params(2),
        name="cast_pad_halves",
    )(x)


def _cast_pad_rows_kernel(x_ref, o_ref):
    rows = x_ref.shape[0]
    o_ref[0:rows, :] = x_ref[...].astype(o_ref.dtype)
    o_ref[rows:, :] = jnp.zeros((o_ref.shape[0] - rows, o_ref.shape[1]), o_ref.dtype)


def _cast_pad_rows(x, padded, cols=256):
    n, r, c = x.shape
    return pl.pallas_call(
        _cast_pad_rows_kernel,
        grid=(n, c // cols),
        in_specs=[pl.BlockSpec((None, r, cols), lambda l, j: (l, 0, j))],
        out_specs=pl.BlockSpec((None, padded, cols), lambda l, j: (l, 0, j)),
        out_shape=jax.ShapeDtypeStruct((n, padded, c), BF16),
        compiler_params=_params(2),
        name="cast_pad_rows",
    )(x)


def _pad_val_gate(x, d_ff, fp):
    return jnp.concatenate([_pad_cols(x[..., :d_ff], fp), _pad_cols(x[..., d_ff:], fp)], axis=-1)


def kernel(x, c, positions, ada_w, ada_b, ffn_w_up, ffn_conv_w, ffn_conv_b, ffn_w_down, sg_w_in, sg_ln_g, sg_ln_b, sg_w_s, sg_b_s, sg_w_out, ret_w_in, ret_gn_g, ret_gn_b, ret_w_out, rwkv_mu, rwkv_w_rkv, rwkv_w0, rwkv_w1, rwkv_w2, rwkv_a0, rwkv_a1, rwkv_a2, rwkv_g1, rwkv_g2, rwkv_k_k, rwkv_k_a, rwkv_r_k, rwkv_ln_g, rwkv_ln_b, rwkv_w_out, final_norm_g):
    batch, seq, d = x.shape
    depth = ada_w.shape[0]
    t = batch * seq
    d_ff = ffn_w_down.shape[1]
    fp = _round_up(d_ff, FFN_SUBCHUNKS * FFN_SUB)

    sg_w_in_b, sg_w_out_b = sg_w_in.astype(BF16), sg_w_out.astype(BF16)
    ret_w_in_b, ret_w_out_b = ret_w_in.astype(BF16), ret_w_out.astype(BF16)
    rwkv_w_rkv_b, rwkv_w_out_b = rwkv_w_rkv.astype(BF16), rwkv_w_out.astype(BF16)
    ffn_w_up_b = _cast_pad_halves(ffn_w_up, d_ff, fp)
    ffn_w_down_b = _cast_pad_rows(ffn_w_down, fp)
    ffn_conv_w_p = _pad_val_gate(ffn_conv_w, d_ff, fp)
    ffn_conv_b_p = _pad_val_gate(ffn_conv_b, d_ff, fp).reshape(depth, 1, 2 * fp)

    mod = _modulation(c, ada_w, ada_b)
    h = x.reshape(t, d)
    cos = sin = None

    for layer in range(depth):
        kind, j = layer % N_MIXERS, layer // N_MIXERS
        if kind == 0:
            z = _norm_matmul(h, mod, layer, 1, 0, sg_w_in_b, j, batch, n=sg_w_in.shape[-1], epilogue="gelu")
            h = _sg_core(z, h, mod, layer, batch, sg_ln_g[j], sg_ln_b[j], sg_w_s[j], sg_b_s[j],
                         sg_w_out_b, j)
        elif kind == 1:
            if cos is None:
                cos, sin = _rope_tables(positions, d // RET_HEADS // 2)
            qk = _norm_matmul(h, mod, layer, 1, 0, ret_w_in_b, j, batch, n=2 * d, epilogue="rope",
                              cos=cos, sin=sin)
            v = _norm_matmul(h, mod, layer, 1, 0, ret_w_in_b, j, batch, n=2 * d, col_offset=2 * d,
                             out_dtype=BF16)
            g = _norm_matmul(h, mod, layer, 1, 0, ret_w_in_b, j, batch, n=2 * d, col_offset=4 * d)
            o = _ret_core(qk, v, g, ret_gn_g[j], ret_gn_b[j], batch)
            h = _proj_residual(o, ret_w_out_b, j, h, mod, layer, 2, batch)
        else:
            rkv = _rwkv_rkv(h, mod, layer, batch, rwkv_mu[j], rwkv_w_rkv_b, j)
            lora = _round_up(rwkv_w1.shape[-1], V7X_LANES)
            lora_a = _round_up(rwkv_a1.shape[-1], V7X_LANES)
            lw, a, g = _rwkv_lora(
                h, mod, layer, batch, rwkv_mu[j], rwkv_w0[j],
                _pad_cols(rwkv_w1[j], lora).astype(BF16), _pad_rows(rwkv_w2[j], lora).astype(BF16),
                rwkv_a0[j],
                _pad_cols(rwkv_a1[j], lora_a).astype(BF16), _pad_rows(rwkv_a2[j], lora_a).astype(BF16),
                rwkv_g1[j].astype(BF16), rwkv_g2[j].astype(BF16))
            o = _rwkv_core(rkv, lw, a, g, rwkv_k_k[j], rwkv_k_a[j], rwkv_r_k[j],
                           rwkv_ln_g[j], rwkv_ln_b[j], batch)
            h = _proj_residual(o, rwkv_w_out_b, j, h, mod, layer, 2, batch)

        h = _conv_ffn(h, mod, layer, batch, ffn_w_up_b, ffn_conv_w_p, ffn_conv_b_p, ffn_w_down_b,
                      final_gain=final_norm_g if layer == depth - 1 else None)

    return h.reshape(batch, seq, d)
```

```python
import functools

import jax
import jax.numpy as jnp
from jax import lax
from jax.experimental import pallas as pl
from jax.experimental.pallas import tpu as pltpu

F32 = jnp.float32
BF16 = jnp.bfloat16

NORM_EPS = 1e-6
LN_EPS = 1e-5
N_MOD = 6
N_MIXERS = 3

SG_CHUNK = 128
SG_GROUPS = 16

RET_HEADS = 8
RET_CHUNK = 128
ROPE_BASE = 10000.0

RWKV_HEAD_DIM = 64
RWKV_GN_EPS = RWKV_HEAD_DIM * 1e-5
RWKV_DECAY_OFFSET = 0.5
RWKV_CHUNK = 64
RWKV_LANES = 256

CONV_WIDTH = 3
FFN_SUB = 256
FFN_SUBCHUNKS = 2
MATMUL_SUBTILE = 512
WEIGHT_BLOCK_ELEMS = 2048 * 1024

V7X_LANES = 128
V7X_SUBLANES = 8
V7X_VMEM_LIMIT = 56 * 1024 * 1024


def _params(n_axes, vmem=V7X_VMEM_LIMIT):
    return pltpu.CompilerParams(dimension_semantics=("arbitrary",) * n_axes,
                                vmem_limit_bytes=vmem)


def _dot(a, b):
    return jnp.dot(a, b, preferred_element_type=F32)


def _dot_nt(a, b):
    return lax.dot_general(a, b, (((1,), (1,)), ((), ())), preferred_element_type=F32)


def _dot_tn(a, b):
    return lax.dot_general(a, b, (((0,), (0,)), ((), ())), preferred_element_type=F32)


def _rms_mod(h, sc, sh):
    ms = jnp.mean(h * h, axis=-1, keepdims=True)
    return (h * lax.rsqrt(ms + NORM_EPS)) * (1.0 + sc) + sh


def _mod_kernel(c_ref, w_ref, b_ref, o_ref):
    cond = jax.nn.silu(c_ref[...])
    o_ref[...] = _dot(cond.astype(BF16), w_ref[...].astype(BF16)) + b_ref[...]


def _modulation(c, ada_w, ada_b, tn=1024):
    depth, d, n = ada_w.shape
    b = c.shape[0]
    rows = -(-b // V7X_SUBLANES) * V7X_SUBLANES
    c_pad = jnp.pad(c, ((0, rows - b), (0, 0)))
    out = pl.pallas_call(
        _mod_kernel,
        grid=(depth, n // tn),
        in_specs=[
            pl.BlockSpec((rows, d), lambda l, j: (0, 0)),
            pl.BlockSpec((None, d, tn), lambda l, j: (l, 0, j)),
            pl.BlockSpec((None, 1, tn), lambda l, j: (l, 0, j)),
        ],
        out_specs=pl.BlockSpec((None, rows, tn), lambda l, j: (l, 0, j)),
        out_shape=jax.ShapeDtypeStruct((depth, rows, n), F32),
        compiler_params=_params(2),
        name="adaln_mod",
    )(c_pad, ada_w, ada_b.reshape(depth, 1, n))
    return out[:, :b].reshape(depth * b * N_MOD, 1, d)


def _mod_spec(d, layer, k, batch, tiles_per_batch):
    base = layer * batch * N_MOD + k
    return pl.BlockSpec((None, 1, d),
                        lambda i, *_: (base + (i // tiles_per_batch) * N_MOD, 0, 0))


def _norm_matmul_kernel(h_ref, sc_ref, sh_ref, w_ref, *rest, epilogue):
    if epilogue == "rope":
        cos_ref, sin_ref, o_ref, xm_ref = rest
    else:
        o_ref, xm_ref = rest

    @pl.when(pl.program_id(1) == 0)
    def _():
        xm_ref[...] = _rms_mod(h_ref[...], sc_ref[...], sh_ref[...]).astype(BF16)

    xm = xm_ref[...]
    tn = w_ref.shape[1]
    sub = min(tn, MATMUL_SUBTILE)
    for c in range(tn // sub):
        cols = slice(c * sub, (c + 1) * sub)
        y = _dot(xm, w_ref[:, cols])
        if epilogue == "gelu":
            y = jax.nn.gelu(y)
        elif epilogue == "rope":
            cos, sin = cos_ref[...], sin_ref[...]
            half = cos.shape[-1]
            parts = []
            for hd in range(sub // (2 * half)):
                x1 = y[:, 2 * hd * half:(2 * hd + 1) * half]
                x2 = y[:, (2 * hd + 1) * half:(2 * hd + 2) * half]
                parts += [x1 * cos - x2 * sin, x2 * cos + x1 * sin]
            y = jnp.concatenate(parts, axis=-1)
        o_ref[:, cols] = y.astype(o_ref.dtype)


def _norm_matmul(h, mod, layer, k_scale, k_shift, w, w_idx, batch, *, n, col_offset=0, epilogue="none",
                 out_dtype=F32, cos=None, sin=None, tm=1024, tn=1024):
    t, d = h.shape
    tm = min(tm, t // batch)
    tpb = (t // batch) // tm
    off = col_offset // tn
    in_specs = [
        pl.BlockSpec((tm, d), lambda i, j: (i, 0)),
        _mod_spec(d, layer, k_scale, batch, tpb),
        _mod_spec(d, layer, k_shift, batch, tpb),
        pl.BlockSpec((None, d, tn), lambda i, j: (w_idx, 0, off + j)),
    ]
    args = [h, mod, mod, w]
    if epilogue == "rope":
        half = cos.shape[-1]
        in_specs += [pl.BlockSpec((tm, half), lambda i, j: (i, 0))] * 2
        args += [cos, sin]
    return pl.pallas_call(
        functools.partial(_norm_matmul_kernel, epilogue=epilogue),
        grid=(t // tm, n // tn),
        in_specs=in_specs,
        out_specs=pl.BlockSpec((tm, tn), lambda i, j: (i, j)),
        out_shape=jax.ShapeDtypeStruct((t, n), out_dtype),
        scratch_shapes=[pltpu.VMEM((tm, d), BF16)],
        compiler_params=_params(2),
        name="norm_matmul_" + epilogue,
    )(*args)


def _proj_residual_kernel(a_ref, w_ref, h_ref, g_ref, o_ref):
    a = a_ref[...]
    tn = w_ref.shape[1]
    sub = min(tn, MATMUL_SUBTILE)
    for c in range(tn // sub):
        cols = slice(c * sub, (c + 1) * sub)
        o_ref[:, cols] = h_ref[:, cols] + g_ref[:, cols] * _dot(a, w_ref[:, cols])


def _proj_residual(a, w, w_idx, h, mod, layer, k_gate, batch, tm=1024, tn=1024):
    t, kdim = a.shape
    d = w.shape[-1]
    tn = min(tn, WEIGHT_BLOCK_ELEMS // kdim)
    tm = min(tm, t // batch)
    tpb = (t // batch) // tm
    base = layer * batch * N_MOD + k_gate
    return pl.pallas_call(
        _proj_residual_kernel,
        grid=(t // tm, d // tn),
        in_specs=[
            pl.BlockSpec((tm, kdim), lambda i, j: (i, 0)),
            pl.BlockSpec((None, kdim, tn), lambda i, j: (w_idx, 0, j)),
            pl.BlockSpec((tm, tn), lambda i, j: (i, j)),
            pl.BlockSpec((None, 1, tn), lambda i, j: (base + (i // tpb) * N_MOD, 0, j)),
        ],
        out_specs=pl.BlockSpec((tm, tn), lambda i, j: (i, j)),
        out_shape=jax.ShapeDtypeStruct((t, d), F32),
        compiler_params=_params(2),
        name="proj_residual",
    )(a, w, h, mod)


def _sg_core_kernel(u_ref, v_ref, h_ref, g_ref, lng_ref, lnb_ref, ws_ref, bs_ref, wo_ref,
                    o_ref, gated_ref):
    tm, width = v_ref.shape
    gdim = width // SG_GROUPS
    v = v_ref[...]
    mu = jnp.mean(v, axis=-1, keepdims=True)
    var = jnp.mean(jnp.square(v - mu), axis=-1, keepdims=True)
    vn = (((v - mu) * lax.rsqrt(var + LN_EPS)) * lng_ref[...] + lnb_ref[...]).astype(BF16)
    row = lax.broadcasted_iota(jnp.int32, (SG_CHUNK, SG_CHUNK), 0)
    col = lax.broadcasted_iota(jnp.int32, (SG_CHUNK, SG_CHUNK), 1)
    causal = row >= col
    for g in range(SG_GROUPS):
        w_causal = jnp.where(causal, ws_ref[g], 0.0).astype(BF16)
        bias = bs_ref[:, g:g + 1]
        cols = slice(g * gdim, (g + 1) * gdim)
        for c in range(tm // SG_CHUNK):
            rows = slice(c * SG_CHUNK, (c + 1) * SG_CHUNK)
            sv = _dot(w_causal, vn[rows, cols]) + bias
            gated_ref[rows, cols] = (u_ref[rows, cols] * sv).astype(BF16)
    o_ref[...] = h_ref[...] + g_ref[...] * _dot(gated_ref[...], wo_ref[...])


def _sg_core(z, h, mod, layer, batch, ln_g, ln_b, w_s, b_s, w_out, w_idx, tm=256):
    t, d = h.shape
    width = z.shape[1] // 2
    tm = min(tm, t // batch)
    tpb = (t // batch) // tm
    return pl.pallas_call(
        _sg_core_kernel,
        grid=(t // tm,),
        in_specs=[
            pl.BlockSpec((tm, width), lambda i: (i, 0)),
            pl.BlockSpec((tm, width), lambda i: (i, 1)),
            pl.BlockSpec((tm, d), lambda i: (i, 0)),
            _mod_spec(d, layer, 2, batch, tpb),
            pl.BlockSpec((1, width), lambda i: (0, 0)),
            pl.BlockSpec((1, width), lambda i: (0, 0)),
            pl.BlockSpec((SG_GROUPS, SG_CHUNK, SG_CHUNK), lambda i: (0, 0, 0)),
            pl.BlockSpec((SG_CHUNK, SG_GROUPS), lambda i: (0, 0)),
            pl.BlockSpec((None, width, d), lambda i: (w_idx, 0, 0)),
        ],
        out_specs=pl.BlockSpec((tm, d), lambda i: (i, 0)),
        out_shape=jax.ShapeDtypeStruct((t, d), F32),
        scratch_shapes=[pltpu.VMEM((tm, width), BF16)],
        compiler_params=_params(1),
        name="sg_core",
    )(z, z, h, mod, ln_g.reshape(1, width), ln_b.reshape(1, width), w_s, b_s.T, w_out)


def _ffn_kernel(h_ref, sc_ref, sh_ref, g_ref, wv_ref, wg_ref, cwv_ref, cwg_ref, cbv_ref, cbg_ref,
                wd_ref, *rest, tiles_per_batch, n_tiles, final_norm):
    if final_norm:
        fg_ref, o_ref, xm_ref, halo_ref = rest[:4]
    else:
        o_ref, xm_ref, halo_ref = rest[:3]
    hbufs = rest[-FFN_SUBCHUNKS:]
    i, j = pl.program_id(0), pl.program_id(1)
    nph = V7X_SUBLANES
    pad = V7X_SUBLANES
    q = h_ref.shape[0]
    d = h_ref.shape[1] // nph
    blk = q + pad
    sub = FFN_SUB
    seq_start = (i % tiles_per_batch) == 0

    @pl.when(j == 0)
    def _():
        sc, sh = sc_ref[...], sh_ref[...]
        for r in range(nph):
            xm_ref[r * q:(r + 1) * q, :] = _rms_mod(h_ref[:, r * d:(r + 1) * d], sc, sh).astype(BF16)
        o_ref[...] = jnp.zeros_like(o_ref)

    xm = xm_ref[...]
    for k, hb in enumerate(hbufs):
        cols = slice(k * sub, (k + 1) * sub)
        hv = _dot(xm, wv_ref[:, cols])
        hg = _dot(xm, wg_ref[:, cols])
        halo = jnp.where(seq_start, 0.0, halo_ref[j, k])
        for r in range(nph):
            base = r * blk
            hb[base + pad - 1:base + pad, :] = halo[r:r + 1, :]
            hb[base + pad:base + blk, 0:sub] = hv[r * q:(r + 1) * q]
            hb[base + pad:base + blk, sub:] = hg[r * q:(r + 1) * q]
        last = [jnp.concatenate([hv[(r + 1) * q - 1:(r + 1) * q], hg[(r + 1) * q - 1:(r + 1) * q]], axis=1)
                for r in range(nph)]
        halo_ref[j, k] = jnp.concatenate(last, axis=0)

    def conv(hb, lanes, cw, cb):
        block = lambda r: hb[r * blk + pad:r * blk + blk, lanes]
        shifted = lambda r: hb[r * blk + pad - 1:r * blk + blk - 1, lanes]
        out = []
        for r in range(nph):
            prev1 = block(r - 1) if r >= 1 else shifted(nph - 1)
            prev2 = block(r - 2) if r >= 2 else shifted(nph - 2 + r)
            acc = prev2 * cw[0:1, :]
            acc = acc + prev1 * cw[1:2, :]
            acc = acc + block(r) * cw[2:3, :]
            out.append(acc + cb)
        return jnp.concatenate(out, axis=0)

    for k, hb in enumerate(hbufs):
        cols = slice(k * sub, (k + 1) * sub)
        val = conv(hb, slice(0, sub), cwv_ref[:, cols], cbv_ref[:, cols])
        gate = conv(hb, slice(sub, 2 * sub), cwg_ref[:, cols], cbg_ref[:, cols])
        act = (jax.nn.silu(gate) * val).astype(BF16)
        part = _dot(act, wd_ref[cols, :])
        for r in range(nph):
            o_ref[:, r * d:(r + 1) * d] += part[r * q:(r + 1) * q]

    @pl.when(j == n_tiles - 1)
    def _():
        for r in range(nph):
            lanes = slice(r * d, (r + 1) * d)
            out = h_ref[:, lanes] + g_ref[...] * o_ref[:, lanes]
            if final_norm:
                ms = jnp.mean(out * out, axis=-1, keepdims=True)
                out = (out * lax.rsqrt(ms + NORM_EPS)) * fg_ref[...]
            o_ref[:, lanes] = out


def _conv_ffn(h, mod, layer, batch, w_up, conv_w, conv_b, w_down, final_gain=None, tm=512):
    t, d = h.shape
    fp = w_down.shape[1]
    tf = FFN_SUBCHUNKS * FFN_SUB
    nf = fp // tf
    tm = min(tm, t // batch)
    tpb = (t // batch) // tm
    final_norm = final_gain is not None
    nph = V7X_SUBLANES
    rows_spec = pl.BlockSpec((tm // nph, nph * d), lambda i, j: (i, 0))
    in_specs = [
        rows_spec,
        _mod_spec(d, layer, 4, batch, tpb),
        _mod_spec(d, layer, 3, batch, tpb),
        _mod_spec(d, layer, 5, batch, tpb),
        pl.BlockSpec((None, d, tf), lambda i, j: (layer, 0, j)),
        pl.BlockSpec((None, d, tf), lambda i, j: (layer, 0, nf + j)),
        pl.BlockSpec((None, CONV_WIDTH, tf), lambda i, j: (layer, 0, j)),
        pl.BlockSpec((None, CONV_WIDTH, tf), lambda i, j: (layer, 0, nf + j)),
        pl.BlockSpec((None, 1, tf), lambda i, j: (layer, 0, j)),
        pl.BlockSpec((None, 1, tf), lambda i, j: (layer, 0, nf + j)),
        pl.BlockSpec((None, tf, d), lambda i, j: (layer, j, 0)),
    ]
    args = [h.reshape(t // nph, nph * d), mod, mod, mod, w_up, w_up, conv_w, conv_w, conv_b, conv_b, w_down]
    if final_norm:
        in_specs.append(pl.BlockSpec((1, d), lambda i, j: (0, 0)))
        args.append(final_gain.reshape(1, d))
    out = pl.pallas_call(
        functools.partial(_ffn_kernel, tiles_per_batch=tpb, n_tiles=nf, final_norm=final_norm),
        grid=(t // tm, nf),
        in_specs=in_specs,
        out_specs=rows_spec,
        out_shape=jax.ShapeDtypeStruct((t // nph, nph * d), F32),
        scratch_shapes=[
            pltpu.VMEM((tm, d), BF16),
            pltpu.VMEM((nf, FFN_SUBCHUNKS, nph, 2 * FFN_SUB), F32),
        ] + [pltpu.VMEM((tm + nph * V7X_SUBLANES, 2 * FFN_SUB), F32)] * FFN_SUBCHUNKS,
        compiler_params=_params(2),
        name="conv_ffn",
    )(*args)
    return out.reshape(t, d)


def _rope_table_kernel(pos_ref, freq_ref, cos_ref, sin_ref):
    ang = pos_ref[...].astype(F32) * freq_ref[...]
    cos_ref[...] = jnp.cos(ang)
    sin_ref[...] = jnp.sin(ang)


def _rope_tables(positions, half, tm=1024):
    t = positions.size
    inv_freq = ROPE_BASE ** (-jnp.arange(half, dtype=F32) / half)
    tm = min(tm, t)
    return pl.pallas_call(
        _rope_table_kernel,
        grid=(t // tm,),
        in_specs=[pl.BlockSpec((tm, 1), lambda i: (i, 0)),
                  pl.BlockSpec((1, half), lambda i: (0, 0))],
        out_specs=[pl.BlockSpec((tm, half), lambda i: (i, 0))] * 2,
        out_shape=[jax.ShapeDtypeStruct((t, half), F32)] * 2,
        compiler_params=_params(1),
        name="rope_tables",
    )(positions.reshape(t, 1), inv_freq.reshape(1, half))


def _ret_core_kernel(q_ref, k_ref, v_ref, g_ref, di_ref, qd_ref, kd_ref, cd_ref, gg_ref, gb_ref,
                     o_ref, state_ref, *, k_scale):
    n_heads, dk, dv = state_ref.shape

    @pl.when(pl.program_id(2) == 0)
    def _():
        state_ref[...] = jnp.zeros_like(state_ref)

    def chunk(rows, hd):
        kcols = slice(hd * dk, (hd + 1) * dk)
        vcols = slice(hd * dv, (hd + 1) * dv)
        q = q_ref[rows, kcols]
        k = k_ref[rows, kcols] * k_scale
        v = v_ref[rows, vcols]
        state = state_ref[hd]
        inner = _dot_nt(q.astype(BF16), k.astype(BF16))
        cross = _dot((q * qd_ref[hd]).astype(BF16), state.astype(BF16))
        update = _dot_tn((k * kd_ref[hd]).astype(BF16), v)
        yield
        state_ref[hd] = state * cd_ref[hd] + update
        out = _dot((inner * di_ref[hd]).astype(BF16), v)
        yield
        out = out + cross
        mu = jnp.mean(out, axis=-1, keepdims=True)
        var = jnp.mean(jnp.square(out - mu), axis=-1, keepdims=True)
        y = ((out - mu) * lax.rsqrt(var + NORM_EPS)) * gg_ref[:, vcols] + gb_ref[:, vcols]
        o_ref[rows, vcols] = (jax.nn.silu(g_ref[rows, vcols]) * y).astype(BF16)

    def body(c, carry):
        rows = pl.ds(pl.multiple_of(c * RET_CHUNK, RET_CHUNK), RET_CHUNK)
        pending = [chunk(rows, hd) for hd in range(n_heads)]
        while pending:
            pending = [gen for gen in pending if next(gen, True) is None]
        return carry

    lax.fori_loop(0, q_ref.shape[0] // RET_CHUNK, body, 0)


def _ret_core(qk, v, g, gn_g, gn_b, batch, rows=512, group=4):
    t, two_d = qk.shape
    d = two_d // 2
    s = t // batch
    rows = min(rows, s)
    spb = s // rows
    n_groups = RET_HEADS // group
    dk = d // RET_HEADS
    dv = v.shape[1] // RET_HEADS
    log_gamma = jnp.log1p(-jnp.exp2(-5.0 - jnp.arange(RET_HEADS, dtype=F32)))
    idx = jnp.arange(RET_CHUNK, dtype=F32)
    rel = idx[:, None] - idx[None, :]
    decay_inner = jnp.where(rel >= 0, jnp.exp(log_gamma[:, None, None] * jnp.maximum(rel, 0.0)), 0.0)
    q_decay = jnp.exp(log_gamma[:, None] * (idx + 1.0))[..., None]
    k_decay = jnp.exp(log_gamma[:, None] * (RET_CHUNK - 1.0 - idx))[..., None]
    chunk_decay = jnp.exp(log_gamma * RET_CHUNK)[:, None, None]
    return pl.pallas_call(
        functools.partial(_ret_core_kernel, k_scale=dk ** -0.5),
        grid=(batch, n_groups, spb),
        in_specs=[
            pl.BlockSpec((rows, group * dk), lambda b, h, i: (b * spb + i, h)),
            pl.BlockSpec((rows, group * dk), lambda b, h, i: (b * spb + i, n_groups + h)),
            pl.BlockSpec((rows, group * dv), lambda b, h, i: (b * spb + i, h)),
            pl.BlockSpec((rows, group * dv), lambda b, h, i: (b * spb + i, h)),
            pl.BlockSpec((group, RET_CHUNK, RET_CHUNK), lambda b, h, i: (h, 0, 0)),
            pl.BlockSpec((group, RET_CHUNK, 1), lambda b, h, i: (h, 0, 0)),
            pl.BlockSpec((group, RET_CHUNK, 1), lambda b, h, i: (h, 0, 0)),
            pl.BlockSpec((group, 1, 1), lambda b, h, i: (h, 0, 0)),
            pl.BlockSpec((1, group * dv), lambda b, h, i: (0, h)),
            pl.BlockSpec((1, group * dv), lambda b, h, i: (0, h)),
        ],
        out_specs=pl.BlockSpec((rows, group * dv), lambda b, h, i: (b * spb + i, h)),
        out_shape=jax.ShapeDtypeStruct((t, v.shape[1]), BF16),
        scratch_shapes=[pltpu.VMEM((group, dk, dv), F32)],
        compiler_params=_params(3),
        name="ret_core",
    )(qk, qk, v, g, decay_inner, q_decay, k_decay, chunk_decay,
      gn_g.reshape(1, -1), gn_b.reshape(1, -1))


def _shifted_inputs(h_ref, hprev_ref, sc_ref, sh_ref, seq_start):
    sc, sh = sc_ref[...], sh_ref[...]
    xm = _rms_mod(h_ref[...], sc, sh)
    prev = _rms_mod(hprev_ref[...], sc, sh)
    last = prev[V7X_SUBLANES - 1:V7X_SUBLANES, :]
    last = jnp.where(seq_start, 0.0, last)
    row = lax.broadcasted_iota(jnp.int32, xm.shape, 0)
    shifted = jnp.where(row == 0, last, pltpu.roll(xm, 1, 0))
    return xm, shifted - xm


def _rwkv_rkv_kernel(h_ref, hprev_ref, sc_ref, sh_ref, mu_ref, w_ref, o_ref, xm_ref, dx_ref, xs_ref,
                     *, tiles_per_batch, tiles_per_proj):
    i, j = pl.program_id(0), pl.program_id(1)

    @pl.when(j == 0)
    def _():
        xm, dx = _shifted_inputs(h_ref, hprev_ref, sc_ref, sh_ref, (i % tiles_per_batch) == 0)
        xm_ref[...] = xm
        dx_ref[...] = dx

    @pl.when(j % tiles_per_proj == 0)
    def _():
        xs_ref[...] = (xm_ref[...] + dx_ref[...] * mu_ref[...]).astype(BF16)

    xs = xs_ref[...]
    tn = w_ref.shape[1]
    sub = min(tn, MATMUL_SUBTILE)
    for c in range(tn // sub):
        cols = slice(c * sub, (c + 1) * sub)
        o_ref[:, cols] = _dot(xs, w_ref[:, cols])


def _prev_rows_spec(tm, d):
    blocks = tm // V7X_SUBLANES
    return pl.BlockSpec((V7X_SUBLANES, d), lambda i, *_: (jnp.maximum(i * blocks - 1, 0), 0))


def _rwkv_rkv(h, mod, layer, batch, mu, w_rkv, w_idx, tm=512, tn=2048):
    t, d = h.shape
    tm = min(tm, t // batch)
    tpb = (t // batch) // tm
    tpp = d // tn
    return pl.pallas_call(
        functools.partial(_rwkv_rkv_kernel, tiles_per_batch=tpb, tiles_per_proj=tpp),
        grid=(t // tm, 3 * tpp),
        in_specs=[
            pl.BlockSpec((tm, d), lambda i, j: (i, 0)),
            _prev_rows_spec(tm, d),
            _mod_spec(d, layer, 1, batch, tpb),
            _mod_spec(d, layer, 0, batch, tpb),
            pl.BlockSpec((None, 1, d), lambda i, j: (j // tpp, 0, 0)),
            pl.BlockSpec((None, None, d, tn), lambda i, j: (w_idx, j // tpp, 0, j % tpp)),
        ],
        out_specs=pl.BlockSpec((tm, tn), lambda i, j: (i, j)),
        out_shape=jax.ShapeDtypeStruct((t, 3 * d), F32),
        scratch_shapes=[pltpu.VMEM((tm, d), F32), pltpu.VMEM((tm, d), F32), pltpu.VMEM((tm, d), BF16)],
        compiler_params=_params(2),
        name="rwkv_rkv",
    )(h, h, mod, mod, mu.reshape(-1, 1, d), w_rkv)


def _rwkv_lora_kernel(h_ref, hprev_ref, sc_ref, sh_ref, mu_ref, w0_ref, w1_ref, w2_ref,
                      a0_ref, a1_ref, a2_ref, g1_ref, g2_ref, lw_ref, a_ref, g_ref, *, tiles_per_batch):
    i = pl.program_id(0)
    xm, dx = _shifted_inputs(h_ref, hprev_ref, sc_ref, sh_ref, (i % tiles_per_batch) == 0)

    def mixed(p):
        return (xm + dx * mu_ref[p]).astype(BF16)

    lora_w = _dot(jnp.tanh(_dot(mixed(3), w1_ref[...])).astype(BF16), w2_ref[...])
    w_log = -jax.nn.softplus(-(w0_ref[...] + lora_w)) - RWKV_DECAY_OFFSET
    lw_ref[...] = -jnp.exp(w_log)
    lora_a = _dot(_dot(mixed(4), a1_ref[...]).astype(BF16), a2_ref[...])
    a_ref[...] = jax.nn.sigmoid(a0_ref[...] + lora_a)
    g_ref[...] = _dot(jax.nn.sigmoid(_dot(mixed(5), g1_ref[...])).astype(BF16), g2_ref[...])


def _rwkv_lora(h, mod, layer, batch, mu, w0, w1, w2, a0, a1, a2, g1, g2, tm=256):
    t, d = h.shape
    tm = min(tm, t // batch)
    tpb = (t // batch) // tm
    full = lambda a: pl.BlockSpec(a.shape, lambda i: (0,) * a.ndim)
    mu3 = mu.reshape(-1, 1, d)
    w0, a0 = w0.reshape(1, d), a0.reshape(1, d)
    consts = [mu3, w0, w1, w2, a0, a1, a2, g1, g2]
    return pl.pallas_call(
        functools.partial(_rwkv_lora_kernel, tiles_per_batch=tpb),
        grid=(t // tm,),
        in_specs=[
            pl.BlockSpec((tm, d), lambda i: (i, 0)),
            _prev_rows_spec(tm, d),
            _mod_spec(d, layer, 1, batch, tpb),
            _mod_spec(d, layer, 0, batch, tpb),
        ] + [full(a) for a in consts],
        out_specs=[pl.BlockSpec((tm, d), lambda i: (i, 0))] * 3,
        out_shape=[jax.ShapeDtypeStruct((t, d), F32)] * 3,
        compiler_params=_params(1),
        name="rwkv_lora",
    )(h, h, mod, mod, *consts)


def _split_dot(x, ones, terms):
    acc = None
    for _ in range(terms):
        piece = x.astype(BF16)
        part = _dot(piece, ones)
        acc = part if acc is None else acc + part
        x = x - piece.astype(F32)
    return acc


def _split_dot_left(ones, x, terms):
    acc = None
    for _ in range(terms):
        piece = x.astype(BF16)
        part = _dot(ones, piece)
        acc = part if acc is None else acc + part
        x = x - piece.astype(F32)
    return acc


def _rwkv_core_kernel(r_ref, k_ref, v_ref, lw_ref, a_ref, g_ref, kk_ref, ka_ref, rk_ref,
                      lng_ref, lnb_ref, o_ref, state_ref):
    c_len, w = RWKV_CHUNK, RWKV_LANES
    heads = w // RWKV_HEAD_DIM
    n = heads * c_len

    @pl.when(pl.program_id(2) == 0)
    def _():
        state_ref[...] = jnp.zeros_like(state_ref)

    rb = lax.broadcasted_iota(jnp.int32, (n, w), 0)
    lb = lax.broadcasted_iota(jnp.int32, (n, w), 1)
    head_match = (rb // c_len) == (lb // RWKV_HEAD_DIM)
    ones_bd = jnp.where((rb // RWKV_HEAD_DIM) == (lb // RWKV_HEAD_DIM), 1.0, 0.0).astype(BF16)
    tr = lax.broadcasted_iota(jnp.int32, (c_len, n), 0)
    ts = lax.broadcasted_iota(jnp.int32, (c_len, n), 1) % c_len
    strict = tr > ts
    incl = tr >= ts
    eye = jnp.where(tr == ts, 1.0, 0.0)
    cr = lax.broadcasted_iota(jnp.int32, (c_len, c_len), 0)
    cc = lax.broadcasted_iota(jnp.int32, (c_len, c_len), 1)
    tril_ones = jnp.where(cr >= cc, 1.0, 0.0).astype(BF16)

    def expand(x):
        xb = x.astype(BF16)
        return jnp.where(head_match, jnp.concatenate([xb] * heads, axis=0), jnp.zeros((), BF16))

    def seg_sum(x):
        return _split_dot(x, ones_bd, 2)

    def chunk(rows, grp):
        lanes = slice(grp * w, (grp + 1) * w)
        r, k, v = r_ref[rows, lanes], k_ref[rows, lanes], v_ref[rows, lanes]
        lw, a, g = lw_ref[rows, lanes], a_ref[rows, lanes], g_ref[rows, lanes]
        k_k, k_a, r_k = kk_ref[:, lanes], ka_ref[:, lanes], rk_ref[:, lanes]
        ln_g, ln_b = lng_ref[:, lanes], lnb_ref[:, lanes]

        kk = k * k_k
        kk_sq = seg_sum(kk * kk)
        kp = k * (1.0 + (a - 1.0) * k_a)
        rk_sum = seg_sum(r * kp * r_k)
        cum = _split_dot_left(tril_ones, lw, 3)
        yield
        kk = kk / jnp.maximum(jnp.sqrt(kk_sq), 1e-12)
        bonus = rk_sum * v
        cum_end = cum[c_len - 1:c_len, :]
        e_pos, e_neg = jnp.exp(cum), jnp.exp(-cum)
        e_tail = jnp.exp(cum_end - cum)
        ba = kk * a
        r_t = (r * e_pos).astype(BF16)
        a_t = (-kk * jnp.exp(cum - lw)).astype(BF16)
        b_t, k_t = ba * e_neg, kp * e_neg
        bk_h = jnp.concatenate([ba * e_tail, kp * e_tail], axis=0).astype(BF16)
        vb = v.astype(BF16)

        gram = _dot_nt(jnp.concatenate([a_t, r_t], axis=0),
                       jnp.concatenate([expand(b_t), expand(k_t)], axis=0))
        yield
        ab = jnp.where(strict, gram[:c_len, :n], 0.0)
        ak = jnp.where(strict, gram[:c_len, n:], 0.0).astype(BF16)
        rbm = jnp.where(incl, gram[c_len:, :n], 0.0).astype(BF16)
        rkm = jnp.where(incl, gram[c_len:, n:], 0.0).astype(BF16)

        tinv = eye + ab
        p = _dot(ab.astype(BF16), expand(ab))
        akv_rkv = _dot(jnp.concatenate([ak, rkm], axis=0), expand(v))
        yield
        for _ in range(c_len.bit_length() - 3):
            both = _dot(jnp.concatenate([p, tinv], axis=0).astype(BF16), expand(p))
            yield
            p, tinv = both[:c_len], tinv + both[c_len:]
        last = _dot(tinv.astype(BF16), expand(p))
        state = state_ref[grp]
        ag_rg = _dot_nt(jnp.concatenate([a_t, r_t], axis=0), state.astype(BF16))
        yield
        tb = (tinv + last).astype(BF16)
        u = _dot(tb, expand(akv_rkv[:c_len] + ag_rg[:c_len]))
        yield
        o = _dot(rbm, expand(u))
        upd = _dot_tn(jnp.concatenate([u.astype(BF16), vb], axis=0), bk_h)
        yield
        o = ag_rg[c_len:] + o + akv_rkv[c_len:]
        state_ref[grp] = state * jnp.exp(cum_end) + jnp.where(
            (rb // RWKV_HEAD_DIM) == (lb // RWKV_HEAD_DIM), upd, 0.0)
        mean = seg_sum(o) * (1.0 / RWKV_HEAD_DIM)
        yield
        cen = o - mean
        var = seg_sum(cen * cen) * (1.0 / RWKV_HEAD_DIM)
        yield
        y = (cen * lax.rsqrt(var + RWKV_GN_EPS)) * ln_g + ln_b
        o_ref[rows, lanes] = ((y + bonus) * g).astype(BF16)

    def body(c, carry):
        rows = pl.ds(pl.multiple_of(c * c_len, c_len), c_len)
        pending = [chunk(rows, grp) for grp in range(state_ref.shape[0])]
        while pending:
            pending = [gen for gen in pending if next(gen, True) is None]
        return carry

    lax.fori_loop(0, r_ref.shape[0] // c_len, body, 0)


def _rwkv_core(rkv, lw, a, g, k_k, k_a, r_k, ln_g, ln_b, batch, rows=256, groups=8):
    t, d = lw.shape
    s = t // batch
    rows = min(rows, s)
    w = RWKV_LANES * groups
    nq = d // w
    spb = s // rows
    seq = lambda off: pl.BlockSpec((rows, w), lambda b, q, i: (b * spb + i, off + q))
    par = pl.BlockSpec((1, w), lambda b, q, i: (0, q))
    vec = lambda x: x.reshape(1, d)
    return pl.pallas_call(
        _rwkv_core_kernel,
        grid=(batch, nq, spb),
        in_specs=[seq(0), seq(nq), seq(2 * nq), seq(0), seq(0), seq(0), par, par, par, par, par],
        out_specs=seq(0),
        out_shape=jax.ShapeDtypeStruct((t, d), BF16),
        scratch_shapes=[pltpu.VMEM((groups, RWKV_LANES, RWKV_LANES), F32)],
        compiler_params=_params(3),
        name="rwkv_core",
    )(rkv, rkv, rkv, lw, a, g, vec(k_k), vec(k_a), vec(r_k), vec(ln_g), vec(ln_b))


def _pad_cols(x, n):
    return jnp.pad(x, [(0, 0)] * (x.ndim - 1) + [(0, n - x.shape[-1])])


def _pad_rows(x, n):
    return jnp.pad(x, [(0, n - x.shape[0])] + [(0, 0)] * (x.ndim - 1))


def _round_up(x, m):
    return -(-x // m) * m


def _cast_pad_halves_kernel(x_ref, o_ref, *, half, padded):
    zeros = jnp.zeros((o_ref.shape[0], padded - half), o_ref.dtype)
    for k in range(2):
        o_ref[:, k * padded:k * padded + half] = x_ref[:, k * half:(k + 1) * half].astype(o_ref.dtype)
        o_ref[:, k * padded + half:(k + 1) * padded] = zeros


def _cast_pad_halves(x, half, padded, rows=128):
    n, r, _ = x.shape
    return pl.pallas_call(
        functools.partial(_cast_pad_halves_kernel, half=half, padded=padded),
        grid=(n, r // rows),
        in_specs=[pl.BlockSpec((None, rows, 2 * half), lambda l, i: (l, i, 0))],
        out_specs=pl.BlockSpec((None, rows, 2 * padded), lambda l, i: (l, i, 0)),
        out_shape=jax.ShapeDtypeStruct((n, r, 2 * padded), BF16),
        compiler_params=_params(2),
        name="cast_pad_halves",
    )(x)


def _cast_pad_rows_kernel(x_ref, o_ref):
    rows = x_ref.shape[0]
    o_ref[0:rows, :] = x_ref[...].astype(o_ref.dtype)
    o_ref[rows:, :] = jnp.zeros((o_ref.shape[0] - rows, o_ref.shape[1]), o_ref.dtype)


def _cast_pad_rows(x, padded, cols=256):
    n, r, c = x.shape
    return pl.pallas_call(
        _cast_pad_rows_kernel,
        grid=(n, c // cols),
        in_specs=[pl.BlockSpec((None, r, cols), lambda l, j: (l, 0, j))],
        out_specs=pl.BlockSpec((None, padded, cols), lambda l, j: (l, 0, j)),
        out_shape=jax.ShapeDtypeStruct((n, padded, c), BF16),
        compiler_params=_params(2),
        name="cast_pad_rows",
    )(x)


def _pad_val_gate(x, d_ff, fp):
    return jnp.concatenate([_pad_cols(x[..., :d_ff], fp), _pad_cols(x[..., d_ff:], fp)], axis=-1)


def kernel(x, c, positions, ada_w, ada_b, ffn_w_up, ffn_conv_w, ffn_conv_b, ffn_w_down, sg_w_in, sg_ln_g, sg_ln_b, sg_w_s, sg_b_s, sg_w_out, ret_w_in, ret_gn_g, ret_gn_b, ret_w_out, rwkv_mu, rwkv_w_rkv, rwkv_w0, rwkv_w1, rwkv_w2, rwkv_a0, rwkv_a1, rwkv_a2, rwkv_g1, rwkv_g2, rwkv_k_k, rwkv_k_a, rwkv_r_k, rwkv_ln_g, rwkv_ln_b, rwkv_w_out, final_norm_g):
    batch, seq, d = x.shape
    depth = ada_w.shape[0]
    t = batch * seq
    d_ff = ffn_w_down.shape[1]
    fp = _round_up(d_ff, FFN_SUBCHUNKS * FFN_SUB)

    sg_w_in_b, sg_w_out_b = sg_w_in.astype(BF16), sg_w_out.astype(BF16)
    ret_w_in_b, ret_w_out_b = ret_w_in.astype(BF16), ret_w_out.astype(BF16)
    rwkv_w_rkv_b, rwkv_w_out_b = rwkv_w_rkv.astype(BF16), rwkv_w_out.astype(BF16)
    ffn_w_up_b = _cast_pad_halves(ffn_w_up, d_ff, fp)
    ffn_w_down_b = _cast_pad_rows(ffn_w_down, fp)
    ffn_conv_w_p = _pad_val_gate(ffn_conv_w, d_ff, fp)
    ffn_conv_b_p = _pad_val_gate(ffn_conv_b, d_ff, fp).reshape(depth, 1, 2 * fp)

    mod = _modulation(c, ada_w, ada_b)
    h = x.reshape(t, d)
    cos = sin = None

    for layer in range(depth):
        kind, j = layer % N_MIXERS, layer // N_MIXERS
        if kind == 0:
            z = _norm_matmul(h, mod, layer, 1, 0, sg_w_in_b, j, batch, n=sg_w_in.shape[-1], epilogue="gelu")
            h = _sg_core(z, h, mod, layer, batch, sg_ln_g[j], sg_ln_b[j], sg_w_s[j], sg_b_s[j],
                         sg_w_out_b, j)
        elif kind == 1:
            if cos is None:
                cos, sin = _rope_tables(positions, d // RET_HEADS // 2)
            qk = _norm_matmul(h, mod, layer, 1, 0, ret_w_in_b, j, batch, n=2 * d, epilogue="rope",
                              cos=cos, sin=sin)
            v = _norm_matmul(h, mod, layer, 1, 0, ret_w_in_b, j, batch, n=2 * d, col_offset=2 * d,
                             out_dtype=BF16)
            g = _norm_matmul(h, mod, layer, 1, 0, ret_w_in_b, j, batch, n=2 * d, col_offset=4 * d)
            o = _ret_core(qk, v, g, ret_gn_g[j], ret_gn_b[j], batch)
            h = _proj_residual(o, ret_w_out_b, j, h, mod, layer, 2, batch)
        else:
            rkv = _rwkv_rkv(h, mod, layer, batch, rwkv_mu[j], rwkv_w_rkv_b, j)
            lora = _round_up(rwkv_w1.shape[-1], V7X_LANES)
            lora_a = _round_up(rwkv_a1.shape[-1], V7X_LANES)
            lw, a, g = _rwkv_lora(
                h, mod, layer, batch, rwkv_mu[j], rwkv_w0[j],
                _pad_cols(rwkv_w1[j], lora).astype(BF16), _pad_rows(rwkv_w2[j], lora).astype(BF16),
                rwkv_a0[j],
                _pad_cols(rwkv_a1[j], lora_a).astype(BF16), _pad_rows(rwkv_a2[j], lora_a).astype(BF16),
                rwkv_g1[j].astype(BF16), rwkv_g2[j].astype(BF16))
            o = _rwkv_core(rkv, lw, a, g, rwkv_k_k[j], rwkv_k_a[j], rwkv_r_k[j],
                           rwkv_ln_g[j], rwkv_ln_b[j], batch)
            h = _proj_residual(o, rwkv_w_out_b, j, h, mod, layer, 2, batch)

        h = _conv_ffn(h, mod, layer, batch, ffn_w_up_b, ffn_conv_w_p, ffn_conv_b_p, ffn_w_down_b,
                      final_gain=final_norm_g if layer == depth - 1 else None)

    return h.reshape(batch, seq, d)
```

```python
import functools
import math

import jax
import jax.numpy as jnp
from jax import lax
from jax.experimental import pallas as pl
from jax.experimental.pallas import tpu as pltpu

F32 = jnp.float32
BF16 = jnp.bfloat16

NORM_EPS = 1e-6
LN_EPS = 1e-5
N_MOD = 6
N_MIXERS = 3

SG_CHUNK = 128
SG_GROUPS = 16

RET_HEADS = 8
RET_CHUNK = 128
ROPE_BASE = 10000.0

RWKV_HEAD_DIM = 64
RWKV_GN_EPS = RWKV_HEAD_DIM * 1e-5
RWKV_DECAY_OFFSET = 0.5
RWKV_CHUNK = 64
RWKV_LANES = 256

CONV_WIDTH = 3
FFN_SUB = 256
FFN_SUBCHUNKS = 2
MATMUL_SUBTILE = 512
WEIGHT_BLOCK_ELEMS = 2048 * 1024

V7X_LANES = 128
V7X_SUBLANES = 8
V7X_VMEM_LIMIT = 56 * 1024 * 1024


def _params(n_axes, vmem=V7X_VMEM_LIMIT):
    return pltpu.CompilerParams(dimension_semantics=("arbitrary",) * n_axes,
                                vmem_limit_bytes=vmem)


def _dot(a, b):
    return jnp.dot(a, b, preferred_element_type=F32)


def _dot_nt(a, b):
    return lax.dot_general(a, b, (((1,), (1,)), ((), ())), preferred_element_type=F32)


def _dot_tn(a, b):
    return lax.dot_general(a, b, (((0,), (0,)), ((), ())), preferred_element_type=F32)


def _rms_mod(h, sc, sh):
    ms = jnp.mean(h * h, axis=-1, keepdims=True)
    return (h * lax.rsqrt(ms + NORM_EPS)) * (1.0 + sc) + sh


def _mod_kernel(c_ref, w_ref, b_ref, o_ref):
    cond = jax.nn.silu(c_ref[...])
    o_ref[...] = _dot(cond.astype(BF16), w_ref[...].astype(BF16)) + b_ref[...]


def _modulation(c, ada_w, ada_b, tn=2048):
    depth, d, n = ada_w.shape
    b = c.shape[0]
    rows = -(-b // V7X_SUBLANES) * V7X_SUBLANES
    c_pad = jnp.pad(c, ((0, rows - b), (0, 0)))
    out = pl.pallas_call(
        _mod_kernel,
        grid=(depth, n // tn),
        in_specs=[
            pl.BlockSpec((rows, d), lambda l, j: (0, 0)),
            pl.BlockSpec((None, d, tn), lambda l, j: (l, 0, j)),
            pl.BlockSpec((None, 1, tn), lambda l, j: (l, 0, j)),
        ],
        out_specs=pl.BlockSpec((None, rows, tn), lambda l, j: (l, 0, j)),
        out_shape=jax.ShapeDtypeStruct((depth, rows, n), F32),
        compiler_params=_params(2),
        name="adaln_mod",
    )(c_pad, ada_w, ada_b.reshape(depth, 1, n))
    return out[:, :b].reshape(depth * b * N_MOD, 1, d)


def _mod_spec(d, layer, k, batch, tiles_per_batch):
    base = layer * batch * N_MOD + k
    return pl.BlockSpec((None, 1, d),
                        lambda i, *_: (base + (i // tiles_per_batch) * N_MOD, 0, 0))


def _norm_matmul_kernel(h_ref, sc_ref, sh_ref, w_ref, *rest, epilogue):
    if epilogue == "rope":
        cos_ref, sin_ref, o_ref, xm_ref = rest
    else:
        o_ref, xm_ref = rest

    @pl.when(pl.program_id(1) == 0)
    def _():
        xm_ref[...] = _rms_mod(h_ref[...], sc_ref[...], sh_ref[...]).astype(BF16)

    xm = xm_ref[...]
    tn = w_ref.shape[1]
    sub = min(tn, MATMUL_SUBTILE)
    for c in range(tn // sub):
        cols = slice(c * sub, (c + 1) * sub)
        y = _dot(xm, w_ref[:, cols])
        if epilogue == "gelu":
            y = jax.nn.gelu(y)
        elif epilogue == "rope":
            cos, sin = cos_ref[...], sin_ref[...]
            half = cos.shape[-1]
            parts = []
            for hd in range(sub // (2 * half)):
                x1 = y[:, 2 * hd * half:(2 * hd + 1) * half]
                x2 = y[:, (2 * hd + 1) * half:(2 * hd + 2) * half]
                parts += [x1 * cos - x2 * sin, x2 * cos + x1 * sin]
            y = jnp.concatenate(parts, axis=-1)
        o_ref[:, cols] = y.astype(o_ref.dtype)


def _norm_matmul(h, mod, layer, k_scale, k_shift, w, w_idx, batch, *, n, col_offset=0, epilogue="none",
                 out_dtype=F32, cos=None, sin=None, tm=1024, tn=1024):
    t, d = h.shape
    tm = min(tm, t // batch)
    tpb = (t // batch) // tm
    off = col_offset // tn
    in_specs = [
        pl.BlockSpec((tm, d), lambda i, j: (i, 0)),
        _mod_spec(d, layer, k_scale, batch, tpb),
        _mod_spec(d, layer, k_shift, batch, tpb),
        pl.BlockSpec((None, d, tn), lambda i, j: (w_idx, 0, off + j)),
    ]
    args = [h, mod, mod, w]
    if epilogue == "rope":
        half = cos.shape[-1]
        in_specs += [pl.BlockSpec((tm, half), lambda i, j: (i, 0))] * 2
        args += [cos, sin]
    return pl.pallas_call(
        functools.partial(_norm_matmul_kernel, epilogue=epilogue),
        grid=(t // tm, n // tn),
        in_specs=in_specs,
        out_specs=pl.BlockSpec((tm, tn), lambda i, j: (i, j)),
        out_shape=jax.ShapeDtypeStruct((t, n), out_dtype),
        scratch_shapes=[pltpu.VMEM((tm, d), BF16)],
        compiler_params=_params(2),
        name="norm_matmul_" + epilogue,
    )(*args)


def _proj_residual_kernel(a_ref, w_ref, h_ref, g_ref, o_ref):
    a = a_ref[...]
    tn = w_ref.shape[1]
    sub = min(tn, MATMUL_SUBTILE)
    for c in range(tn // sub):
        cols = slice(c * sub, (c + 1) * sub)
        o_ref[:, cols] = h_ref[:, cols] + g_ref[:, cols] * _dot(a, w_ref[:, cols])


def _proj_residual(a, w, w_idx, h, mod, layer, k_gate, batch, tm=1024, tn=1024):
    t, kdim = a.shape
    d = w.shape[-1]
    tn = min(tn, WEIGHT_BLOCK_ELEMS // kdim)
    tm = min(tm, t // batch)
    tpb = (t // batch) // tm
    base = layer * batch * N_MOD + k_gate
    return pl.pallas_call(
        _proj_residual_kernel,
        grid=(t // tm, d // tn),
        in_specs=[
            pl.BlockSpec((tm, kdim), lambda i, j: (i, 0)),
            pl.BlockSpec((None, kdim, tn), lambda i, j: (w_idx, 0, j)),
            pl.BlockSpec((tm, tn), lambda i, j: (i, j)),
            pl.BlockSpec((None, 1, tn), lambda i, j: (base + (i // tpb) * N_MOD, 0, j)),
        ],
        out_specs=pl.BlockSpec((tm, tn), lambda i, j: (i, j)),
        out_shape=jax.ShapeDtypeStruct((t, d), F32),
        compiler_params=_params(2),
        name="proj_residual",
    )(a, w, h, mod)


def _sg_core_kernel(u_ref, v_ref, h_ref, g_ref, lng_ref, lnb_ref, ws_ref, bs_ref, wo_ref,
                    o_ref, gated_ref):
    tm, width = v_ref.shape
    gdim = width // SG_GROUPS
    v = v_ref[...]
    mu = jnp.mean(v, axis=-1, keepdims=True)
    var = jnp.mean(jnp.square(v - mu), axis=-1, keepdims=True)
    vn = (((v - mu) * lax.rsqrt(var + LN_EPS)) * lng_ref[...] + lnb_ref[...]).astype(BF16)
    row = lax.broadcasted_iota(jnp.int32, (SG_CHUNK, SG_CHUNK), 0)
    col = lax.broadcasted_iota(jnp.int32, (SG_CHUNK, SG_CHUNK), 1)
    causal = row >= col
    for g in range(SG_GROUPS):
        w_causal = jnp.where(causal, ws_ref[g], 0.0).astype(BF16)
        bias = bs_ref[:, g:g + 1]
        cols = slice(g * gdim, (g + 1) * gdim)
        for c in range(tm // SG_CHUNK):
            rows = slice(c * SG_CHUNK, (c + 1) * SG_CHUNK)
            sv = _dot(w_causal, vn[rows, cols]) + bias
            gated_ref[rows, cols] = (u_ref[rows, cols] * sv).astype(BF16)
    o_ref[...] = h_ref[...] + g_ref[...] * _dot(gated_ref[...], wo_ref[...])


def _sg_core(z, h, mod, layer, batch, ln_g, ln_b, w_s, b_s, w_out, w_idx, tm=256):
    t, d = h.shape
    width = z.shape[1] // 2
    tm = min(tm, t // batch)
    tpb = (t // batch) // tm
    return pl.pallas_call(
        _sg_core_kernel,
        grid=(t // tm,),
        in_specs=[
            pl.BlockSpec((tm, width), lambda i: (i, 0)),
            pl.BlockSpec((tm, width), lambda i: (i, 1)),
            pl.BlockSpec((tm, d), lambda i: (i, 0)),
            _mod_spec(d, layer, 2, batch, tpb),
            pl.BlockSpec((1, width), lambda i: (0, 0)),
            pl.BlockSpec((1, width), lambda i: (0, 0)),
            pl.BlockSpec((SG_GROUPS, SG_CHUNK, SG_CHUNK), lambda i: (0, 0, 0)),
            pl.BlockSpec((SG_CHUNK, SG_GROUPS), lambda i: (0, 0)),
            pl.BlockSpec((None, width, d), lambda i: (w_idx, 0, 0)),
        ],
        out_specs=pl.BlockSpec((tm, d), lambda i: (i, 0)),
        out_shape=jax.ShapeDtypeStruct((t, d), F32),
        scratch_shapes=[pltpu.VMEM((tm, width), BF16)],
        compiler_params=_params(1),
        name="sg_core",
    )(z, z, h, mod, ln_g.reshape(1, width), ln_b.reshape(1, width), w_s, b_s.T, w_out)


def _ffn_kernel(h_ref, sc_ref, sh_ref, g_ref, wv_ref, wg_ref, cwv_ref, cwg_ref, cbv_ref, cbg_ref,
                wd_ref, *rest, tiles_per_batch, n_tiles, final_norm):
    if final_norm:
        fg_ref, o_ref, xm_ref, halo_ref = rest[:4]
    else:
        o_ref, xm_ref, halo_ref = rest[:3]
    hbufs = rest[-FFN_SUBCHUNKS:]
    i, j = pl.program_id(0), pl.program_id(1)
    tm = h_ref.shape[0]
    pad = V7X_SUBLANES
    sub = FFN_SUB

    @pl.when(j == 0)
    def _():
        xm_ref[...] = _rms_mod(h_ref[...], sc_ref[...], sh_ref[...]).astype(BF16)
        o_ref[...] = jnp.zeros_like(o_ref)

    seq_start = (i % tiles_per_batch) == 0
    xm = xm_ref[...]
    for k, hb in enumerate(hbufs):
        cols = slice(k * sub, (k + 1) * sub)
        hb[0:pad, :] = jnp.where(seq_start, 0.0, halo_ref[j, k])
        hb[pad:, 0:sub] = _dot(xm, wv_ref[:, cols])
        hb[pad:, sub:] = _dot(xm, wg_ref[:, cols])
        halo_ref[j, k] = hb[tm:, :]

    def conv(hb, lanes, cw, cb):
        acc = hb[pad - 2:pad - 2 + tm, lanes] * cw[0:1, :]
        acc = acc + hb[pad - 1:pad - 1 + tm, lanes] * cw[1:2, :]
        acc = acc + hb[pad:pad + tm, lanes] * cw[2:3, :]
        return acc + cb

    for k, hb in enumerate(hbufs):
        cols = slice(k * sub, (k + 1) * sub)
        val = conv(hb, slice(0, sub), cwv_ref[:, cols], cbv_ref[:, cols])
        gate = conv(hb, slice(sub, 2 * sub), cwg_ref[:, cols], cbg_ref[:, cols])
        act = (jax.nn.silu(gate) * val).astype(BF16)
        o_ref[...] += _dot(act, wd_ref[cols, :])

    @pl.when(j == n_tiles - 1)
    def _():
        out = h_ref[...] + g_ref[...] * o_ref[...]
        if final_norm:
            ms = jnp.mean(out * out, axis=-1, keepdims=True)
            out = (out * lax.rsqrt(ms + NORM_EPS)) * fg_ref[...]
        o_ref[...] = out


def _conv_ffn(h, mod, layer, batch, w_up, conv_w, conv_b, w_down, final_gain=None, tm=512):
    t, d = h.shape
    fp = w_down.shape[1]
    tf = FFN_SUBCHUNKS * FFN_SUB
    nf = fp // tf
    tm = min(tm, t // batch)
    tpb = (t // batch) // tm
    final_norm = final_gain is not None
    in_specs = [
        pl.BlockSpec((tm, d), lambda i, j: (i, 0)),
        _mod_spec(d, layer, 4, batch, tpb),
        _mod_spec(d, layer, 3, batch, tpb),
        _mod_spec(d, layer, 5, batch, tpb),
        pl.BlockSpec((None, d, tf), lambda i, j: (layer, 0, j)),
        pl.BlockSpec((None, d, tf), lambda i, j: (layer, 0, nf + j)),
        pl.BlockSpec((None, CONV_WIDTH, tf), lambda i, j: (layer, 0, j)),
        pl.BlockSpec((None, CONV_WIDTH, tf), lambda i, j: (layer, 0, nf + j)),
        pl.BlockSpec((None, 1, tf), lambda i, j: (layer, 0, j)),
        pl.BlockSpec((None, 1, tf), lambda i, j: (layer, 0, nf + j)),
        pl.BlockSpec((None, tf, d), lambda i, j: (layer, j, 0)),
    ]
    args = [h, mod, mod, mod, w_up, w_up, conv_w, conv_w, conv_b, conv_b, w_down]
    if final_norm:
        in_specs.append(pl.BlockSpec((1, d), lambda i, j: (0, 0)))
        args.append(final_gain.reshape(1, d))
    return pl.pallas_call(
        functools.partial(_ffn_kernel, tiles_per_batch=tpb, n_tiles=nf, final_norm=final_norm),
        grid=(t // tm, nf),
        in_specs=in_specs,
        out_specs=pl.BlockSpec((tm, d), lambda i, j: (i, 0)),
        out_shape=jax.ShapeDtypeStruct((t, d), F32),
        scratch_shapes=[
            pltpu.VMEM((tm, d), BF16),
            pltpu.VMEM((nf, FFN_SUBCHUNKS, V7X_SUBLANES, 2 * FFN_SUB), F32),
        ] + [pltpu.VMEM((tm + V7X_SUBLANES, 2 * FFN_SUB), F32)] * FFN_SUBCHUNKS,
        compiler_params=_params(2),
        name="conv_ffn",
    )(*args)


def _rope_table_kernel(pos_ref, freq_ref, cos_ref, sin_ref):
    ang = pos_ref[...].astype(F32) * freq_ref[...]
    cos_ref[...] = jnp.cos(ang)
    sin_ref[...] = jnp.sin(ang)


def _rope_tables(positions, half, tm=1024):
    t = positions.size
    inv_freq = ROPE_BASE ** (-jnp.arange(half, dtype=F32) / half)
    tm = min(tm, t)
    return pl.pallas_call(
        _rope_table_kernel,
        grid=(t // tm,),
        in_specs=[pl.BlockSpec((tm, 1), lambda i: (i, 0)),
                  pl.BlockSpec((1, half), lambda i: (0, 0))],
        out_specs=[pl.BlockSpec((tm, half), lambda i: (i, 0))] * 2,
        out_shape=[jax.ShapeDtypeStruct((t, half), F32)] * 2,
        compiler_params=_params(1),
        name="rope_tables",
    )(positions.reshape(t, 1), inv_freq.reshape(1, half))


def _ret_core_kernel(q_ref, k_ref, v_ref, g_ref, di_ref, qd_ref, kd_ref, cd_ref, gg_ref, gb_ref,
                     o_ref, state_ref, *, k_scale):
    n_heads, dk, dv = state_ref.shape

    @pl.when(pl.program_id(2) == 0)
    def _():
        state_ref[...] = jnp.zeros_like(state_ref)

    def chunk(rows, hd):
        kcols = slice(hd * dk, (hd + 1) * dk)
        vcols = slice(hd * dv, (hd + 1) * dv)
        q = q_ref[rows, kcols]
        k = k_ref[rows, kcols] * k_scale
        v = v_ref[rows, vcols]
        state = state_ref[hd]
        inner = _dot_nt(q.astype(BF16), k.astype(BF16))
        cross = _dot((q * qd_ref[hd]).astype(BF16), state.astype(BF16))
        update = _dot_tn((k * kd_ref[hd]).astype(BF16), v)
        yield
        state_ref[hd] = state * cd_ref[hd] + update
        out = _dot((inner * di_ref[hd]).astype(BF16), v)
        yield
        out = out + cross
        mu = jnp.mean(out, axis=-1, keepdims=True)
        var = jnp.mean(jnp.square(out - mu), axis=-1, keepdims=True)
        y = ((out - mu) * lax.rsqrt(var + NORM_EPS)) * gg_ref[:, vcols] + gb_ref[:, vcols]
        o_ref[rows, vcols] = (jax.nn.silu(g_ref[rows, vcols]) * y).astype(BF16)

    def body(c, carry):
        rows = pl.ds(pl.multiple_of(c * RET_CHUNK, RET_CHUNK), RET_CHUNK)
        pending = [chunk(rows, hd) for hd in range(n_heads)]
        while pending:
            pending = [gen for gen in pending if next(gen, True) is None]
        return carry

    lax.fori_loop(0, q_ref.shape[0] // RET_CHUNK, body, 0)


def _ret_core(qk, v, g, gn_g, gn_b, batch, rows=512, group=4):
    t, two_d = qk.shape
    d = two_d // 2
    s = t // batch
    rows = min(rows, s)
    spb = s // rows
    n_groups = RET_HEADS // group
    dk = d // RET_HEADS
    dv = v.shape[1] // RET_HEADS
    log_gamma = jnp.log1p(-jnp.exp2(-5.0 - jnp.arange(RET_HEADS, dtype=F32)))
    idx = jnp.arange(RET_CHUNK, dtype=F32)
    rel = idx[:, None] - idx[None, :]
    decay_inner = jnp.where(rel >= 0, jnp.exp(log_gamma[:, None, None] * jnp.maximum(rel, 0.0)), 0.0)
    q_decay = jnp.exp(log_gamma[:, None] * (idx + 1.0))[..., None]
    k_decay = jnp.exp(log_gamma[:, None] * (RET_CHUNK - 1.0 - idx))[..., None]
    chunk_decay = jnp.exp(log_gamma * RET_CHUNK)[:, None, None]
    return pl.pallas_call(
        functools.partial(_ret_core_kernel, k_scale=dk ** -0.5),
        grid=(batch, n_groups, spb),
        in_specs=[
            pl.BlockSpec((rows, group * dk), lambda b, h, i: (b * spb + i, h)),
            pl.BlockSpec((rows, group * dk), lambda b, h, i: (b * spb + i, n_groups + h)),
            pl.BlockSpec((rows, group * dv), lambda b, h, i: (b * spb + i, h)),
            pl.BlockSpec((rows, group * dv), lambda b, h, i: (b * spb + i, h)),
            pl.BlockSpec((group, RET_CHUNK, RET_CHUNK), lambda b, h, i: (h, 0, 0)),
            pl.BlockSpec((group, RET_CHUNK, 1), lambda b, h, i: (h, 0, 0)),
            pl.BlockSpec((group, RET_CHUNK, 1), lambda b, h, i: (h, 0, 0)),
            pl.BlockSpec((group, 1, 1), lambda b, h, i: (h, 0, 0)),
            pl.BlockSpec((1, group * dv), lambda b, h, i: (0, h)),
            pl.BlockSpec((1, group * dv), lambda b, h, i: (0, h)),
        ],
        out_specs=pl.BlockSpec((rows, group * dv), lambda b, h, i: (b * spb + i, h)),
        out_shape=jax.ShapeDtypeStruct((t, v.shape[1]), BF16),
        scratch_shapes=[pltpu.VMEM((group, dk, dv), F32)],
        compiler_params=_params(3),
        name="ret_core",
    )(qk, qk, v, g, decay_inner, q_decay, k_decay, chunk_decay,
      gn_g.reshape(1, -1), gn_b.reshape(1, -1))


def _shifted_inputs(h_ref, hprev_ref, sc_ref, sh_ref, seq_start):
    sc, sh = sc_ref[...], sh_ref[...]
    xm = _rms_mod(h_ref[...], sc, sh)
    prev = _rms_mod(hprev_ref[...], sc, sh)
    last = prev[V7X_SUBLANES - 1:V7X_SUBLANES, :]
    last = jnp.where(seq_start, 0.0, last)
    row = lax.broadcasted_iota(jnp.int32, xm.shape, 0)
    shifted = jnp.where(row == 0, last, pltpu.roll(xm, 1, 0))
    return xm, shifted - xm


def _rwkv_rkv_kernel(h_ref, hprev_ref, sc_ref, sh_ref, mu_ref, w_ref, o_ref, xm_ref, dx_ref, xs_ref,
                     *, tiles_per_batch, tiles_per_proj):
    i, j = pl.program_id(0), pl.program_id(1)

    @pl.when(j == 0)
    def _():
        xm, dx = _shifted_inputs(h_ref, hprev_ref, sc_ref, sh_ref, (i % tiles_per_batch) == 0)
        xm_ref[...] = xm
        dx_ref[...] = dx

    @pl.when(j % tiles_per_proj == 0)
    def _():
        xs_ref[...] = (xm_ref[...] + dx_ref[...] * mu_ref[...]).astype(BF16)

    xs = xs_ref[...]
    tn = w_ref.shape[1]
    sub = min(tn, MATMUL_SUBTILE)
    for c in range(tn // sub):
        cols = slice(c * sub, (c + 1) * sub)
        o_ref[:, cols] = _dot(xs, w_ref[:, cols])


def _prev_rows_spec(tm, d):
    blocks = tm // V7X_SUBLANES
    return pl.BlockSpec((V7X_SUBLANES, d), lambda i, *_: (jnp.maximum(i * blocks - 1, 0), 0))


def _rwkv_rkv(h, mod, layer, batch, mu, w_rkv, w_idx, tm=512, tn=2048):
    t, d = h.shape
    tm = min(tm, t // batch)
    tpb = (t // batch) // tm
    tpp = d // tn
    return pl.pallas_call(
        functools.partial(_rwkv_rkv_kernel, tiles_per_batch=tpb, tiles_per_proj=tpp),
        grid=(t // tm, 3 * tpp),
        in_specs=[
            pl.BlockSpec((tm, d), lambda i, j: (i, 0)),
            _prev_rows_spec(tm, d),
            _mod_spec(d, layer, 1, batch, tpb),
            _mod_spec(d, layer, 0, batch, tpb),
            pl.BlockSpec((None, 1, d), lambda i, j: (j // tpp, 0, 0)),
            pl.BlockSpec((None, None, d, tn), lambda i, j: (w_idx, j // tpp, 0, j % tpp)),
        ],
        out_specs=pl.BlockSpec((tm, tn), lambda i, j: (i, j)),
        out_shape=jax.ShapeDtypeStruct((t, 3 * d), F32),
        scratch_shapes=[pltpu.VMEM((tm, d), F32), pltpu.VMEM((tm, d), F32), pltpu.VMEM((tm, d), BF16)],
        compiler_params=_params(2),
        name="rwkv_rkv",
    )(h, h, mod, mod, mu.reshape(-1, 1, d), w_rkv)


def _rwkv_lora_kernel(h_ref, hprev_ref, sc_ref, sh_ref, mu_ref, w0_ref, w1_ref, w2_ref,
                      a0_ref, a1_ref, a2_ref, g1_ref, g2_ref, lw_ref, a_ref, g_ref, *, tiles_per_batch):
    i = pl.program_id(0)
    xm, dx = _shifted_inputs(h_ref, hprev_ref, sc_ref, sh_ref, (i % tiles_per_batch) == 0)

    def mixed(p):
        return (xm + dx * mu_ref[p]).astype(BF16)

    lora_w = _dot(jnp.tanh(_dot(mixed(3), w1_ref[...])).astype(BF16), w2_ref[...])
    lw_ref[...] = -math.exp(-RWKV_DECAY_OFFSET) * jax.nn.sigmoid(w0_ref[...] + lora_w)
    lora_a = _dot(_dot(mixed(4), a1_ref[...]).astype(BF16), a2_ref[...])
    a_ref[...] = jax.nn.sigmoid(a0_ref[...] + lora_a)
    g_ref[...] = _dot(jax.nn.sigmoid(_dot(mixed(5), g1_ref[...])).astype(BF16), g2_ref[...])


def _rwkv_lora(h, mod, layer, batch, mu, w0, w1, w2, a0, a1, a2, g1, g2, tm=256):
    t, d = h.shape
    tm = min(tm, t // batch)
    tpb = (t // batch) // tm
    full = lambda a: pl.BlockSpec(a.shape, lambda i: (0,) * a.ndim)
    mu3 = mu.reshape(-1, 1, d)
    w0, a0 = w0.reshape(1, d), a0.reshape(1, d)
    consts = [mu3, w0, w1, w2, a0, a1, a2, g1, g2]
    return pl.pallas_call(
        functools.partial(_rwkv_lora_kernel, tiles_per_batch=tpb),
        grid=(t // tm,),
        in_specs=[
            pl.BlockSpec((tm, d), lambda i: (i, 0)),
            _prev_rows_spec(tm, d),
            _mod_spec(d, layer, 1, batch, tpb),
            _mod_spec(d, layer, 0, batch, tpb),
        ] + [full(a) for a in consts],
        out_specs=[pl.BlockSpec((tm, d), lambda i: (i, 0))] * 3,
        out_shape=[jax.ShapeDtypeStruct((t, d), F32)] * 3,
        compiler_params=_params(1),
        name="rwkv_lora",
    )(h, h, mod, mod, *consts)


def _split_dot(x, ones, terms):
    acc = None
    for _ in range(terms):
        piece = x.astype(BF16)
        part = _dot(piece, ones)
        acc = part if acc is None else acc + part
        x = x - piece.astype(F32)
    return acc


def _split_dot_left(ones, x, terms):
    acc = None
    for _ in range(terms):
        piece = x.astype(BF16)
        part = _dot(ones, piece)
        acc = part if acc is None else acc + part
        x = x - piece.astype(F32)
    return acc


def _rwkv_core_kernel(r_ref, k_ref, v_ref, lw_ref, a_ref, g_ref, kk_ref, ka_ref, rk_ref,
                      lng_ref, lnb_ref, o_ref, state_ref):
    c_len, w = RWKV_CHUNK, RWKV_LANES
    heads = w // RWKV_HEAD_DIM
    n = heads * c_len

    @pl.when(pl.program_id(2) == 0)
    def _():
        state_ref[...] = jnp.zeros_like(state_ref)

    rb = lax.broadcasted_iota(jnp.int32, (n, w), 0)
    lb = lax.broadcasted_iota(jnp.int32, (n, w), 1)
    head_match = (rb // c_len) == (lb // RWKV_HEAD_DIM)
    ones_bd = jnp.where((rb // RWKV_HEAD_DIM) == (lb // RWKV_HEAD_DIM), 1.0, 0.0).astype(BF16)
    tr = lax.broadcasted_iota(jnp.int32, (c_len, n), 0)
    ts = lax.broadcasted_iota(jnp.int32, (c_len, n), 1) % c_len
    strict = tr > ts
    incl = tr >= ts
    eye = jnp.where(tr == ts, 1.0, 0.0)
    cr = lax.broadcasted_iota(jnp.int32, (c_len, c_len), 0)
    cc = lax.broadcasted_iota(jnp.int32, (c_len, c_len), 1)
    tril_ones = jnp.where(cr >= cc, 1.0, 0.0).astype(BF16)

    def expand(x):
        xb = x.astype(BF16)
        return jnp.where(head_match, jnp.concatenate([xb] * heads, axis=0), jnp.zeros((), BF16))

    def seg_sum(x):
        return _split_dot(x, ones_bd, 2)

    def chunk(rows, grp):
        lanes = slice(grp * w, (grp + 1) * w)
        r, k, v = r_ref[rows, lanes], k_ref[rows, lanes], v_ref[rows, lanes]
        lw, a, g = lw_ref[rows, lanes], a_ref[rows, lanes], g_ref[rows, lanes]
        k_k, k_a, r_k = kk_ref[:, lanes], ka_ref[:, lanes], rk_ref[:, lanes]
        ln_g, ln_b = lng_ref[:, lanes], lnb_ref[:, lanes]

        kk = k * k_k
        kk_sq = seg_sum(kk * kk)
        kp = k * (1.0 + (a - 1.0) * k_a)
        rk_sum = seg_sum(r * kp * r_k)
        cum = _split_dot_left(tril_ones, lw, 3)
        yield
        kk = kk / jnp.maximum(jnp.sqrt(kk_sq), 1e-12)
        bonus = rk_sum * v
        cum_end = cum[c_len - 1:c_len, :]
        e_pos, e_neg = jnp.exp(cum), jnp.exp(-cum)
        e_tail = jnp.exp(cum_end - cum)
        ba = kk * a
        r_t = (r * e_pos).astype(BF16)
        a_t = (-kk * jnp.exp(cum - lw)).astype(BF16)
        b_t, k_t = ba * e_neg, kp * e_neg
        bk_h = jnp.concatenate([ba * e_tail, kp * e_tail], axis=0).astype(BF16)
        vb = v.astype(BF16)

        gram = _dot_nt(jnp.concatenate([a_t, r_t], axis=0),
                       jnp.concatenate([expand(b_t), expand(k_t)], axis=0))
        yield
        ab = jnp.where(strict, gram[:c_len, :n], 0.0)
        ak = jnp.where(strict, gram[:c_len, n:], 0.0).astype(BF16)
        rbm = jnp.where(incl, gram[c_len:, :n], 0.0).astype(BF16)
        rkm = jnp.where(incl, gram[c_len:, n:], 0.0).astype(BF16)

        tinv = eye + ab
        p = _dot(ab.astype(BF16), expand(ab))
        akv_rkv = _dot(jnp.concatenate([ak, rkm], axis=0), expand(v))
        yield
        for _ in range(c_len.bit_length() - 3):
            both = _dot(jnp.concatenate([p, tinv], axis=0).astype(BF16), expand(p))
            yield
            p, tinv = both[:c_len], tinv + both[c_len:]
        last = _dot(tinv.astype(BF16), expand(p))
        state = state_ref[grp]
        ag_rg = _dot_nt(jnp.concatenate([a_t, r_t], axis=0), state.astype(BF16))
        yield
        tb = (tinv + last).astype(BF16)
        u = _dot(tb, expand(akv_rkv[:c_len] + ag_rg[:c_len]))
        yield
        o = _dot(rbm, expand(u))
        upd = _dot_tn(jnp.concatenate([u.astype(BF16), vb], axis=0), bk_h)
        yield
        o = ag_rg[c_len:] + o + akv_rkv[c_len:]
        state_ref[grp] = state * jnp.exp(cum_end) + jnp.where(
            (rb // RWKV_HEAD_DIM) == (lb // RWKV_HEAD_DIM), upd, 0.0)
        mean = seg_sum(o) * (1.0 / RWKV_HEAD_DIM)
        yield
        cen = o - mean
        var = seg_sum(cen * cen) * (1.0 / RWKV_HEAD_DIM)
        yield
        y = (cen * lax.rsqrt(var + RWKV_GN_EPS)) * ln_g + ln_b
        o_ref[rows, lanes] = ((y + bonus) * g).astype(BF16)

    def body(c, carry):
        rows = pl.ds(pl.multiple_of(c * c_len, c_len), c_len)
        pending = [chunk(rows, grp) for grp in range(state_ref.shape[0])]
        while pending:
            pending = [gen for gen in pending if next(gen, True) is None]
        return carry

    lax.fori_loop(0, r_ref.shape[0] // c_len, body, 0)


def _rwkv_core(rkv, lw, a, g, k_k, k_a, r_k, ln_g, ln_b, batch, rows=256, groups=8):
    t, d = lw.shape
    s = t // batch
    rows = min(rows, s)
    w = RWKV_LANES * groups
    nq = d // w
    spb = s // rows
    seq = lambda off: pl.BlockSpec((rows, w), lambda b, q, i: (b * spb + i, off + q))
    par = pl.BlockSpec((1, w), lambda b, q, i: (0, q))
    vec = lambda x: x.reshape(1, d)
    return pl.pallas_call(
        _rwkv_core_kernel,
        grid=(batch, nq, spb),
        in_specs=[seq(0), seq(nq), seq(2 * nq), seq(0), seq(0), seq(0), par, par, par, par, par],
        out_specs=seq(0),
        out_shape=jax.ShapeDtypeStruct((t, d), BF16),
        scratch_shapes=[pltpu.VMEM((groups, RWKV_LANES, RWKV_LANES), F32)],
        compiler_params=_params(3),
        name="rwkv_core",
    )(rkv, rkv, rkv, lw, a, g, vec(k_k), vec(k_a), vec(r_k), vec(ln_g), vec(ln_b))


def _pad_cols(x, n):
    return jnp.pad(x, [(0, 0)] * (x.ndim - 1) + [(0, n - x.shape[-1])])


def _pad_rows(x, n):
    return jnp.pad(x, [(0, n - x.shape[0])] + [(0, 0)] * (x.ndim - 1))


def _round_up(x, m):
    return -(-x // m) * m


def _cast_pad_halves_kernel(x_ref, o_ref, *, half, padded):
    zeros = jnp.zeros((o_ref.shape[0], padded - half), o_ref.dtype)
    for k in range(2):
        o_ref[:, k * padded:k * padded + half] = x_ref[:, k * half:(k + 1) * half].astype(o_ref.dtype)
        o_ref[:, k * padded + half:(k + 1) * padded] = zeros


def _cast_pad_halves(x, half, padded, rows=128):
    n, r, _ = x.shape
    return pl.pallas_call(
        functools.partial(_cast_pad_halves_kernel, half=half, padded=padded),
        grid=(n, r // rows),
        in_specs=[pl.BlockSpec((None, rows, 2 * half), lambda l, i: (l, i, 0))],
        out_specs=pl.BlockSpec((None, rows, 2 * padded), lambda l, i: (l, i, 0)),
        out_shape=jax.ShapeDtypeStruct((n, r, 2 * padded), BF16),
        compiler_params=_params(2),
        name="cast_pad_halves",
    )(x)


def _cast_pad_rows_kernel(x_ref, o_ref):
    rows = x_ref.shape[0]
    o_ref[0:rows, :] = x_ref[...].astype(o_ref.dtype)
    o_ref[rows:, :] = jnp.zeros((o_ref.shape[0] - rows, o_ref.shape[1]), o_ref.dtype)


def _cast_pad_rows(x, padded, cols=256):
    n, r, c = x.shape
    return pl.pallas_call(
        _cast_pad_rows_kernel,
        grid=(n, c // cols),
        in_specs=[pl.BlockSpec((None, r, cols), lambda l, j: (l, 0, j))],
        out_specs=pl.BlockSpec((None, padded, cols), lambda l, j: (l, 0, j)),
        out_shape=jax.ShapeDtypeStruct((n, padded, c), BF16),
        compiler_params=_params(2),
        name="cast_pad_rows",
    )(x)


def _pad_val_gate(x, d_ff, fp):
    return jnp.concatenate([_pad_cols(x[..., :d_ff], fp), _pad_cols(x[..., d_ff:], fp)], axis=-1)


def kernel(x, c, positions, ada_w, ada_b, ffn_w_up, ffn_conv_w, ffn_conv_b, ffn_w_down, sg_w_in, sg_ln_g, sg_ln_b, sg_w_s, sg_b_s, sg_w_out, ret_w_in, ret_gn_g, ret_gn_b, ret_w_out, rwkv_mu, rwkv_w_rkv, rwkv_w0, rwkv_w1, rwkv_w2, rwkv_a0, rwkv_a1, rwkv_a2, rwkv_g1, rwkv_g2, rwkv_k_k, rwkv_k_a, rwkv_r_k, rwkv_ln_g, rwkv_ln_b, rwkv_w_out, final_norm_g):
    batch, seq, d = x.shape
    depth = ada_w.shape[0]
    t = batch * seq
    d_ff = ffn_w_down.shape[1]
    fp = _round_up(d_ff, FFN_SUBCHUNKS * FFN_SUB)

    sg_w_in_b, sg_w_out_b = sg_w_in.astype(BF16), sg_w_out.astype(BF16)
    ret_w_in_b, ret_w_out_b = ret_w_in.astype(BF16), ret_w_out.astype(BF16)
    rwkv_w_rkv_b, rwkv_w_out_b = rwkv_w_rkv.astype(BF16), rwkv_w_out.astype(BF16)
    ffn_w_up_b = _cast_pad_halves(ffn_w_up, d_ff, fp)
    ffn_w_down_b = _cast_pad_rows(ffn_w_down, fp)
    ffn_conv_w_p = _pad_val_gate(ffn_conv_w, d_ff, fp)
    ffn_conv_b_p = _pad_val_gate(ffn_conv_b, d_ff, fp).reshape(depth, 1, 2 * fp)

    mod = _modulation(c, ada_w, ada_b)
    h = x.reshape(t, d)
    cos = sin = None

    for layer in range(depth):
        kind, j = layer % N_MIXERS, layer // N_MIXERS
        if kind == 0:
            z = _norm_matmul(h, mod, layer, 1, 0, sg_w_in_b, j, batch, n=sg_w_in.shape[-1], epilogue="gelu")
            h = _sg_core(z, h, mod, layer, batch, sg_ln_g[j], sg_ln_b[j], sg_w_s[j], sg_b_s[j],
                         sg_w_out_b, j)
        elif kind == 1:
            if cos is None:
                cos, sin = _rope_tables(positions, d // RET_HEADS // 2)
            qk = _norm_matmul(h, mod, layer, 1, 0, ret_w_in_b, j, batch, n=2 * d, epilogue="rope",
                              cos=cos, sin=sin)
            v = _norm_matmul(h, mod, layer, 1, 0, ret_w_in_b, j, batch, n=2 * d, col_offset=2 * d,
                             out_dtype=BF16)
            g = _norm_matmul(h, mod, layer, 1, 0, ret_w_in_b, j, batch, n=2 * d, col_offset=4 * d)
            o = _ret_core(qk, v, g, ret_gn_g[j], ret_gn_b[j], batch)
            h = _proj_residual(o, ret_w_out_b, j, h, mod, layer, 2, batch)
        else:
            rkv = _rwkv_rkv(h, mod, layer, batch, rwkv_mu[j], rwkv_w_rkv_b, j)
            lora = _round_up(rwkv_w1.shape[-1], V7X_LANES)
            lora_a = _round_up(rwkv_a1.shape[-1], V7X_LANES)
            lw, a, g = _rwkv_lora(
                h, mod, layer, batch, rwkv_mu[j], rwkv_w0[j],
                _pad_cols(rwkv_w1[j], lora).astype(BF16), _pad_rows(rwkv_w2[j], lora).astype(BF16),
                rwkv_a0[j],
                _pad_cols(rwkv_a1[j], lora_a).astype(BF16), _pad_rows(rwkv_a2[j], lora_a).astype(BF16),
                rwkv_g1[j].astype(BF16), rwkv_g2[j].astype(BF16))
            o = _rwkv_core(rkv, lw, a, g, rwkv_k_k[j], rwkv_k_a[j], rwkv_r_k[j],
                           rwkv_ln_g[j], rwkv_ln_b[j], batch)
            h = _proj_residual(o, rwkv_w_out_b, j, h, mod, layer, 2, batch)

        h = _conv_ffn(h, mod, layer, batch, ffn_w_up_b, ffn_conv_w_p, ffn_conv_b_p, ffn_w_down_b,
                      final_gain=final_norm_g if layer == depth - 1 else None)

    return h.reshape(batch, seq, d)
```

```python
import functools
import math

import jax
import jax.numpy as jnp
from jax import lax
from jax.experimental import pallas as pl
from jax.experimental.pallas import tpu as pltpu

F32 = jnp.float32
BF16 = jnp.bfloat16

NORM_EPS = 1e-6
LN_EPS = 1e-5
N_MOD = 6
N_MIXERS = 3

SG_CHUNK = 128
SG_GROUPS = 16

RET_HEADS = 8
RET_CHUNK = 128
ROPE_BASE = 10000.0

RWKV_HEAD_DIM = 64
RWKV_GN_EPS = RWKV_HEAD_DIM * 1e-5
RWKV_DECAY_OFFSET = 0.5
RWKV_CHUNK = 64
RWKV_LANES = 256

CONV_WIDTH = 3
FFN_SUB = 256
FFN_SUBCHUNKS = 2
MATMUL_SUBTILE = 512
WEIGHT_BLOCK_ELEMS = 2048 * 1024

V7X_LANES = 128
V7X_SUBLANES = 8
V7X_VMEM_LIMIT = 56 * 1024 * 1024


def _params(n_axes, vmem=V7X_VMEM_LIMIT):
    return pltpu.CompilerParams(dimension_semantics=("arbitrary",) * n_axes,
                                vmem_limit_bytes=vmem)


def _dot(a, b):
    return jnp.dot(a, b, preferred_element_type=F32)


def _dot_nt(a, b):
    return lax.dot_general(a, b, (((1,), (1,)), ((), ())), preferred_element_type=F32)


def _dot_tn(a, b):
    return lax.dot_general(a, b, (((0,), (0,)), ((), ())), preferred_element_type=F32)


def _rms_mod(h, sc, sh):
    ms = jnp.mean(h * h, axis=-1, keepdims=True)
    return (h * lax.rsqrt(ms + NORM_EPS)) * (1.0 + sc) + sh


def _mod_kernel(c_ref, w_ref, b_ref, o_ref):
    cond = jax.nn.silu(c_ref[...])
    o_ref[...] = _dot(cond.astype(BF16), w_ref[...].astype(BF16)) + b_ref[...]


def _modulation(c, ada_w, ada_b, tn=2048):
    depth, d, n = ada_w.shape
    b = c.shape[0]
    rows = -(-b // V7X_SUBLANES) * V7X_SUBLANES
    c_pad = jnp.pad(c, ((0, rows - b), (0, 0)))
    out = pl.pallas_call(
        _mod_kernel,
        grid=(depth, n // tn),
        in_specs=[
            pl.BlockSpec((rows, d), lambda l, j: (0, 0)),
            pl.BlockSpec((None, d, tn), lambda l, j: (l, 0, j)),
            pl.BlockSpec((None, 1, tn), lambda l, j: (l, 0, j)),
        ],
        out_specs=pl.BlockSpec((None, rows, tn), lambda l, j: (l, 0, j)),
        out_shape=jax.ShapeDtypeStruct((depth, rows, n), F32),
        compiler_params=_params(2),
        name="adaln_mod",
    )(c_pad, ada_w, ada_b.reshape(depth, 1, n))
    return out[:, :b].reshape(depth * b * N_MOD, 1, d)


def _mod_spec(d, layer, k, batch, tiles_per_batch):
    base = layer * batch * N_MOD + k
    return pl.BlockSpec((None, 1, d),
                        lambda i, *_: (base + (i // tiles_per_batch) * N_MOD, 0, 0))


def _norm_matmul_kernel(h_ref, sc_ref, sh_ref, w_ref, *rest, epilogue):
    if epilogue == "rope":
        cos_ref, sin_ref, o_ref, xm_ref = rest
    else:
        o_ref, xm_ref = rest

    @pl.when(pl.program_id(1) == 0)
    def _():
        xm_ref[...] = _rms_mod(h_ref[...], sc_ref[...], sh_ref[...]).astype(xm_ref.dtype)

    xm = xm_ref[...]
    tn = w_ref.shape[1]
    sub = min(tn, MATMUL_SUBTILE)
    for c in range(tn // sub):
        cols = slice(c * sub, (c + 1) * sub)
        y = _dot(xm, w_ref[:, cols])
        if epilogue == "gelu":
            y = jax.nn.gelu(y)
        elif epilogue == "rope":
            cos, sin = cos_ref[...], sin_ref[...]
            half = cos.shape[-1]
            parts = []
            for hd in range(sub // (2 * half)):
                x1 = y[:, 2 * hd * half:(2 * hd + 1) * half]
                x2 = y[:, (2 * hd + 1) * half:(2 * hd + 2) * half]
                parts += [x1 * cos - x2 * sin, x2 * cos + x1 * sin]
            y = jnp.concatenate(parts, axis=-1)
        o_ref[:, cols] = y.astype(o_ref.dtype)


def _norm_matmul(h, mod, layer, k_scale, k_shift, w, w_idx, batch, *, n, col_offset=0, epilogue="none",
                 out_dtype=F32, cos=None, sin=None, tm=1024, tn=1024):
    t, d = h.shape
    tm = min(tm, t // batch)
    tpb = (t // batch) // tm
    off = col_offset // tn
    in_specs = [
        pl.BlockSpec((tm, d), lambda i, j: (i, 0)),
        _mod_spec(d, layer, k_scale, batch, tpb),
        _mod_spec(d, layer, k_shift, batch, tpb),
        pl.BlockSpec((None, d, tn), lambda i, j: (w_idx, 0, off + j)),
    ]
    args = [h, mod, mod, w]
    if epilogue == "rope":
        half = cos.shape[-1]
        in_specs += [pl.BlockSpec((tm, half), lambda i, j: (i, 0))] * 2
        args += [cos, sin]
    return pl.pallas_call(
        functools.partial(_norm_matmul_kernel, epilogue=epilogue),
        grid=(t // tm, n // tn),
        in_specs=in_specs,
        out_specs=pl.BlockSpec((tm, tn), lambda i, j: (i, j)),
        out_shape=jax.ShapeDtypeStruct((t, n), out_dtype),
        scratch_shapes=[pltpu.VMEM((tm, d), w.dtype)],
        compiler_params=_params(2),
        name="norm_matmul_" + epilogue,
    )(*args)


def _proj_residual_kernel(a_ref, w_ref, h_ref, g_ref, o_ref):
    a = a_ref[...]
    tn = w_ref.shape[1]
    sub = min(tn, MATMUL_SUBTILE)
    for c in range(tn // sub):
        cols = slice(c * sub, (c + 1) * sub)
        o_ref[:, cols] = h_ref[:, cols] + g_ref[:, cols] * _dot(a, w_ref[:, cols])


def _proj_residual(a, w, w_idx, h, mod, layer, k_gate, batch, tm=1024, tn=1024):
    t, kdim = a.shape
    d = w.shape[-1]
    tn = min(tn, WEIGHT_BLOCK_ELEMS // kdim)
    tm = min(tm, t // batch)
    tpb = (t // batch) // tm
    base = layer * batch * N_MOD + k_gate
    return pl.pallas_call(
        _proj_residual_kernel,
        grid=(t // tm, d // tn),
        in_specs=[
            pl.BlockSpec((tm, kdim), lambda i, j: (i, 0)),
            pl.BlockSpec((None, kdim, tn), lambda i, j: (w_idx, 0, j)),
            pl.BlockSpec((tm, tn), lambda i, j: (i, j)),
            pl.BlockSpec((None, 1, tn), lambda i, j: (base + (i // tpb) * N_MOD, 0, j)),
        ],
        out_specs=pl.BlockSpec((tm, tn), lambda i, j: (i, j)),
        out_shape=jax.ShapeDtypeStruct((t, d), F32),
        compiler_params=_params(2),
        name="proj_residual",
    )(a, w, h, mod)


def _sg_core_kernel(u_ref, v_ref, h_ref, g_ref, lng_ref, lnb_ref, ws_ref, bs_ref, wo_ref,
                    o_ref, gated_ref):
    tm, width = v_ref.shape
    gdim = width // SG_GROUPS
    v = v_ref[...]
    mu = jnp.mean(v, axis=-1, keepdims=True)
    var = jnp.mean(jnp.square(v - mu), axis=-1, keepdims=True)
    vn = (((v - mu) * lax.rsqrt(var + LN_EPS)) * lng_ref[...] + lnb_ref[...]).astype(BF16)
    row = lax.broadcasted_iota(jnp.int32, (SG_CHUNK, SG_CHUNK), 0)
    col = lax.broadcasted_iota(jnp.int32, (SG_CHUNK, SG_CHUNK), 1)
    causal = row >= col
    for g in range(SG_GROUPS):
        w_causal = jnp.where(causal, ws_ref[g], 0.0).astype(BF16)
        bias = bs_ref[:, g:g + 1]
        cols = slice(g * gdim, (g + 1) * gdim)
        for c in range(tm // SG_CHUNK):
            rows = slice(c * SG_CHUNK, (c + 1) * SG_CHUNK)
            sv = _dot(w_causal, vn[rows, cols]) + bias
            gated_ref[rows, cols] = (u_ref[rows, cols] * sv).astype(BF16)
    o_ref[...] = h_ref[...] + g_ref[...] * _dot(gated_ref[...], wo_ref[...])


def _sg_core(z, h, mod, layer, batch, ln_g, ln_b, w_s, b_s, w_out, w_idx, tm=256):
    t, d = h.shape
    width = z.shape[1] // 2
    tm = min(tm, t // batch)
    tpb = (t // batch) // tm
    return pl.pallas_call(
        _sg_core_kernel,
        grid=(t // tm,),
        in_specs=[
            pl.BlockSpec((tm, width), lambda i: (i, 0)),
            pl.BlockSpec((tm, width), lambda i: (i, 1)),
            pl.BlockSpec((tm, d), lambda i: (i, 0)),
            _mod_spec(d, layer, 2, batch, tpb),
            pl.BlockSpec((1, width), lambda i: (0, 0)),
            pl.BlockSpec((1, width), lambda i: (0, 0)),
            pl.BlockSpec((SG_GROUPS, SG_CHUNK, SG_CHUNK), lambda i: (0, 0, 0)),
            pl.BlockSpec((SG_CHUNK, SG_GROUPS), lambda i: (0, 0)),
            pl.BlockSpec((None, width, d), lambda i: (w_idx, 0, 0)),
        ],
        out_specs=pl.BlockSpec((tm, d), lambda i: (i, 0)),
        out_shape=jax.ShapeDtypeStruct((t, d), F32),
        scratch_shapes=[pltpu.VMEM((tm, width), BF16)],
        compiler_params=_params(1),
        name="sg_core",
    )(z, z, h, mod, ln_g.reshape(1, width), ln_b.reshape(1, width), w_s, b_s.T, w_out)


def _ffn_kernel(h_ref, sc_ref, sh_ref, g_ref, wv_ref, wg_ref, cwv_ref, cwg_ref, cbv_ref, cbg_ref,
                wd_ref, *rest, tiles_per_batch, n_tiles, final_norm):
    if final_norm:
        fg_ref, o_ref, xm_ref, halo_ref = rest[:4]
    else:
        o_ref, xm_ref, halo_ref = rest[:3]
    hbufs = rest[-FFN_SUBCHUNKS:]
    i, j = pl.program_id(0), pl.program_id(1)
    tm = h_ref.shape[0]
    pad = V7X_SUBLANES
    sub = FFN_SUB

    @pl.when(j == 0)
    def _():
        xm_ref[...] = _rms_mod(h_ref[...], sc_ref[...], sh_ref[...]).astype(BF16)
        o_ref[...] = jnp.zeros_like(o_ref)

    seq_start = (i % tiles_per_batch) == 0
    xm = xm_ref[...]
    for k, hb in enumerate(hbufs):
        cols = slice(k * sub, (k + 1) * sub)
        hb[0:pad, :] = jnp.where(seq_start, 0.0, halo_ref[j, k])
        hb[pad:, 0:sub] = _dot(xm, wv_ref[:, cols])
        hb[pad:, sub:] = _dot(xm, wg_ref[:, cols])
        halo_ref[j, k] = hb[tm:, :]

    def conv(hb, lanes, cw, cb):
        acc = hb[pad - 2:pad - 2 + tm, lanes] * cw[0:1, :]
        acc = acc + hb[pad - 1:pad - 1 + tm, lanes] * cw[1:2, :]
        acc = acc + hb[pad:pad + tm, lanes] * cw[2:3, :]
        return acc + cb

    for k, hb in enumerate(hbufs):
        cols = slice(k * sub, (k + 1) * sub)
        val = conv(hb, slice(0, sub), cwv_ref[:, cols], cbv_ref[:, cols])
        gate = conv(hb, slice(sub, 2 * sub), cwg_ref[:, cols], cbg_ref[:, cols])
        act = (jax.nn.silu(gate) * val).astype(BF16)
        o_ref[...] += _dot(act, wd_ref[cols, :])

    @pl.when(j == n_tiles - 1)
    def _():
        out = h_ref[...] + g_ref[...] * o_ref[...]
        if final_norm:
            ms = jnp.mean(out * out, axis=-1, keepdims=True)
            out = (out * lax.rsqrt(ms + NORM_EPS)) * fg_ref[...]
        o_ref[...] = out


def _conv_ffn(h, mod, layer, batch, w_up, conv_w, conv_b, w_down, final_gain=None, tm=512):
    t, d = h.shape
    fp = w_down.shape[1]
    tf = FFN_SUBCHUNKS * FFN_SUB
    nf = fp // tf
    tm = min(tm, t // batch)
    tpb = (t // batch) // tm
    final_norm = final_gain is not None
    in_specs = [
        pl.BlockSpec((tm, d), lambda i, j: (i, 0)),
        _mod_spec(d, layer, 4, batch, tpb),
        _mod_spec(d, layer, 3, batch, tpb),
        _mod_spec(d, layer, 5, batch, tpb),
        pl.BlockSpec((None, d, tf), lambda i, j: (layer, 0, j)),
        pl.BlockSpec((None, d, tf), lambda i, j: (layer, 0, nf + j)),
        pl.BlockSpec((None, CONV_WIDTH, tf), lambda i, j: (layer, 0, j)),
        pl.BlockSpec((None, CONV_WIDTH, tf), lambda i, j: (layer, 0, nf + j)),
        pl.BlockSpec((None, 1, tf), lambda i, j: (layer, 0, j)),
        pl.BlockSpec((None, 1, tf), lambda i, j: (layer, 0, nf + j)),
        pl.BlockSpec((None, tf, d), lambda i, j: (layer, j, 0)),
    ]
    args = [h, mod, mod, mod, w_up, w_up, conv_w, conv_w, conv_b, conv_b, w_down]
    if final_norm:
        in_specs.append(pl.BlockSpec((1, d), lambda i, j: (0, 0)))
        args.append(final_gain.reshape(1, d))
    return pl.pallas_call(
        functools.partial(_ffn_kernel, tiles_per_batch=tpb, n_tiles=nf, final_norm=final_norm),
        grid=(t // tm, nf),
        in_specs=in_specs,
        out_specs=pl.BlockSpec((tm, d), lambda i, j: (i, 0)),
        out_shape=jax.ShapeDtypeStruct((t, d), F32),
        scratch_shapes=[
            pltpu.VMEM((tm, d), BF16),
            pltpu.VMEM((nf, FFN_SUBCHUNKS, V7X_SUBLANES, 2 * FFN_SUB), F32),
        ] + [pltpu.VMEM((tm + V7X_SUBLANES, 2 * FFN_SUB), F32)] * FFN_SUBCHUNKS,
        compiler_params=_params(2),
        name="conv_ffn",
    )(*args)


def _rope_table_kernel(pos_ref, freq_ref, cos_ref, sin_ref):
    ang = pos_ref[...].astype(F32) * freq_ref[...]
    cos_ref[...] = jnp.cos(ang)
    sin_ref[...] = jnp.sin(ang)


def _rope_tables(positions, half, tm=1024):
    t = positions.size
    inv_freq = ROPE_BASE ** (-jnp.arange(half, dtype=F32) / half)
    tm = min(tm, t)
    return pl.pallas_call(
        _rope_table_kernel,
        grid=(t // tm,),
        in_specs=[pl.BlockSpec((tm, 1), lambda i: (i, 0)),
                  pl.BlockSpec((1, half), lambda i: (0, 0))],
        out_specs=[pl.BlockSpec((tm, half), lambda i: (i, 0))] * 2,
        out_shape=[jax.ShapeDtypeStruct((t, half), F32)] * 2,
        compiler_params=_params(1),
        name="rope_tables",
    )(positions.reshape(t, 1), inv_freq.reshape(1, half))


def _ret_core_kernel(q_ref, k_ref, v_ref, g_ref, di_ref, qd_ref, kd_ref, cd_ref, gg_ref, gb_ref,
                     o_ref, state_ref, *, k_scale):
    n_heads, dk, dv = state_ref.shape

    @pl.when(pl.program_id(2) == 0)
    def _():
        state_ref[...] = jnp.zeros_like(state_ref)

    def chunk(rows, hd):
        kcols = slice(hd * dk, (hd + 1) * dk)
        vcols = slice(hd * dv, (hd + 1) * dv)
        q = q_ref[rows, kcols]
        k = k_ref[rows, kcols] * k_scale
        v = v_ref[rows, vcols]
        state = state_ref[hd]
        inner = _dot_nt(q.astype(BF16), k.astype(BF16))
        cross = _dot((q * qd_ref[hd]).astype(BF16), state.astype(BF16))
        update = _dot_tn((k * kd_ref[hd]).astype(BF16), v)
        yield
        state_ref[hd] = state * cd_ref[hd] + update
        out = _dot((inner * di_ref[hd]).astype(BF16), v)
        yield
        out = out + cross
        mu = jnp.mean(out, axis=-1, keepdims=True)
        var = jnp.mean(jnp.square(out - mu), axis=-1, keepdims=True)
        y = ((out - mu) * lax.rsqrt(var + NORM_EPS)) * gg_ref[:, vcols] + gb_ref[:, vcols]
        o_ref[rows, vcols] = (jax.nn.silu(g_ref[rows, vcols]) * y).astype(BF16)

    def body(c, carry):
        rows = pl.ds(pl.multiple_of(c * RET_CHUNK, RET_CHUNK), RET_CHUNK)
        pending = [chunk(rows, hd) for hd in range(n_heads)]
        while pending:
            pending = [gen for gen in pending if next(gen, True) is None]
        return carry

    lax.fori_loop(0, q_ref.shape[0] // RET_CHUNK, body, 0)


def _ret_core(qk, v, g, gn_g, gn_b, batch, rows=256, group=8):
    t, two_d = qk.shape
    d = two_d // 2
    s = t // batch
    rows = min(rows, s)
    spb = s // rows
    n_groups = RET_HEADS // group
    dk = d // RET_HEADS
    dv = v.shape[1] // RET_HEADS
    log_gamma = jnp.log1p(-jnp.exp2(-5.0 - jnp.arange(RET_HEADS, dtype=F32)))
    idx = jnp.arange(RET_CHUNK, dtype=F32)
    rel = idx[:, None] - idx[None, :]
    decay_inner = jnp.where(rel >= 0, jnp.exp(log_gamma[:, None, None] * jnp.maximum(rel, 0.0)), 0.0)
    q_decay = jnp.exp(log_gamma[:, None] * (idx + 1.0))[..., None]
    k_decay = jnp.exp(log_gamma[:, None] * (RET_CHUNK - 1.0 - idx))[..., None]
    chunk_decay = jnp.exp(log_gamma * RET_CHUNK)[:, None, None]
    return pl.pallas_call(
        functools.partial(_ret_core_kernel, k_scale=dk ** -0.5),
        grid=(batch, n_groups, spb),
        in_specs=[
            pl.BlockSpec((rows, group * dk), lambda b, h, i: (b * spb + i, h)),
            pl.BlockSpec((rows, group * dk), lambda b, h, i: (b * spb + i, n_groups + h)),
            pl.BlockSpec((rows, group * dv), lambda b, h, i: (b * spb + i, h)),
            pl.BlockSpec((rows, group * dv), lambda b, h, i: (b * spb + i, h)),
            pl.BlockSpec((group, RET_CHUNK, RET_CHUNK), lambda b, h, i: (h, 0, 0)),
            pl.BlockSpec((group, RET_CHUNK, 1), lambda b, h, i: (h, 0, 0)),
            pl.BlockSpec((group, RET_CHUNK, 1), lambda b, h, i: (h, 0, 0)),
            pl.BlockSpec((group, 1, 1), lambda b, h, i: (h, 0, 0)),
            pl.BlockSpec((1, group * dv), lambda b, h, i: (0, h)),
            pl.BlockSpec((1, group * dv), lambda b, h, i: (0, h)),
        ],
        out_specs=pl.BlockSpec((rows, group * dv), lambda b, h, i: (b * spb + i, h)),
        out_shape=jax.ShapeDtypeStruct((t, v.shape[1]), BF16),
        scratch_shapes=[pltpu.VMEM((group, dk, dv), F32)],
        compiler_params=_params(3),
        name="ret_core",
    )(qk, qk, v, g, decay_inner, q_decay, k_decay, chunk_decay,
      gn_g.reshape(1, -1), gn_b.reshape(1, -1))


def _shifted_inputs(h_ref, hprev_ref, sc_ref, sh_ref, seq_start):
    sc, sh = sc_ref[...], sh_ref[...]
    xm = _rms_mod(h_ref[...], sc, sh)
    prev = _rms_mod(hprev_ref[...], sc, sh)
    last = prev[V7X_SUBLANES - 1:V7X_SUBLANES, :]
    last = jnp.where(seq_start, 0.0, last)
    row = lax.broadcasted_iota(jnp.int32, xm.shape, 0)
    shifted = jnp.where(row == 0, last, pltpu.roll(xm, 1, 0))
    return xm, shifted - xm


def _rwkv_rkv_kernel(h_ref, hprev_ref, sc_ref, sh_ref, mu_ref, w_ref, o_ref, xm_ref, dx_ref, xs_ref,
                     *, tiles_per_batch, tiles_per_proj):
    i, j = pl.program_id(0), pl.program_id(1)

    @pl.when(j == 0)
    def _():
        xm, dx = _shifted_inputs(h_ref, hprev_ref, sc_ref, sh_ref, (i % tiles_per_batch) == 0)
        xm_ref[...] = xm
        dx_ref[...] = dx

    @pl.when(j % tiles_per_proj == 0)
    def _():
        xs_ref[...] = (xm_ref[...] + dx_ref[...] * mu_ref[...]).astype(BF16)

    xs = xs_ref[...]
    tn = w_ref.shape[1]
    sub = min(tn, MATMUL_SUBTILE)
    for c in range(tn // sub):
        cols = slice(c * sub, (c + 1) * sub)
        o_ref[:, cols] = _dot(xs, w_ref[:, cols])


def _prev_rows_spec(tm, d):
    blocks = tm // V7X_SUBLANES
    return pl.BlockSpec((V7X_SUBLANES, d), lambda i, *_: (jnp.maximum(i * blocks - 1, 0), 0))


def _rwkv_rkv(h, mod, layer, batch, mu, w_rkv, w_idx, tm=512, tn=2048):
    t, d = h.shape
    tm = min(tm, t // batch)
    tpb = (t // batch) // tm
    tpp = d // tn
    return pl.pallas_call(
        functools.partial(_rwkv_rkv_kernel, tiles_per_batch=tpb, tiles_per_proj=tpp),
        grid=(t // tm, 3 * tpp),
        in_specs=[
            pl.BlockSpec((tm, d), lambda i, j: (i, 0)),
            _prev_rows_spec(tm, d),
            _mod_spec(d, layer, 1, batch, tpb),
            _mod_spec(d, layer, 0, batch, tpb),
            pl.BlockSpec((None, 1, d), lambda i, j: (j // tpp, 0, 0)),
            pl.BlockSpec((None, None, d, tn), lambda i, j: (w_idx, j // tpp, 0, j % tpp)),
        ],
        out_specs=pl.BlockSpec((tm, tn), lambda i, j: (i, j)),
        out_shape=jax.ShapeDtypeStruct((t, 3 * d), F32),
        scratch_shapes=[pltpu.VMEM((tm, d), F32), pltpu.VMEM((tm, d), F32), pltpu.VMEM((tm, d), BF16)],
        compiler_params=_params(2),
        name="rwkv_rkv",
    )(h, h, mod, mod, mu.reshape(-1, 1, d), w_rkv)


def _rwkv_lora_kernel(h_ref, hprev_ref, sc_ref, sh_ref, mu_ref, w0_ref, w1_ref, w2_ref,
                      a0_ref, a1_ref, a2_ref, g1_ref, g2_ref, lw_ref, a_ref, g_ref, *, tiles_per_batch):
    i = pl.program_id(0)
    xm, dx = _shifted_inputs(h_ref, hprev_ref, sc_ref, sh_ref, (i % tiles_per_batch) == 0)

    def mixed(p):
        return (xm + dx * mu_ref[p]).astype(BF16)

    lora_w = _dot(jnp.tanh(_dot(mixed(3), w1_ref[...])).astype(BF16), w2_ref[...])
    lw_ref[...] = -math.exp(-RWKV_DECAY_OFFSET) * jax.nn.sigmoid(w0_ref[...] + lora_w)
    lora_a = _dot(_dot(mixed(4), a1_ref[...]).astype(BF16), a2_ref[...])
    a_ref[...] = jax.nn.sigmoid(a0_ref[...] + lora_a)
    g_ref[...] = _dot(jax.nn.sigmoid(_dot(mixed(5), g1_ref[...])).astype(BF16), g2_ref[...])


def _rwkv_lora(h, mod, layer, batch, mu, w0, w1, w2, a0, a1, a2, g1, g2, tm=256):
    t, d = h.shape
    tm = min(tm, t // batch)
    tpb = (t // batch) // tm
    full = lambda a: pl.BlockSpec(a.shape, lambda i: (0,) * a.ndim)
    mu3 = mu.reshape(-1, 1, d)
    w0, a0 = w0.reshape(1, d), a0.reshape(1, d)
    consts = [mu3, w0, w1, w2, a0, a1, a2, g1, g2]
    return pl.pallas_call(
        functools.partial(_rwkv_lora_kernel, tiles_per_batch=tpb),
        grid=(t // tm,),
        in_specs=[
            pl.BlockSpec((tm, d), lambda i: (i, 0)),
            _prev_rows_spec(tm, d),
            _mod_spec(d, layer, 1, batch, tpb),
            _mod_spec(d, layer, 0, batch, tpb),
        ] + [full(a) for a in consts],
        out_specs=[pl.BlockSpec((tm, d), lambda i: (i, 0))] * 3,
        out_shape=[jax.ShapeDtypeStruct((t, d), F32)] * 3,
        compiler_params=_params(1),
        name="rwkv_lora",
    )(h, h, mod, mod, *consts)


def _split_dot(x, ones, terms):
    acc = None
    for _ in range(terms):
        piece = x.astype(BF16)
        part = _dot(piece, ones)
        acc = part if acc is None else acc + part
        x = x - piece.astype(F32)
    return acc


def _split_dot_left(ones, x, terms):
    acc = None
    for _ in range(terms):
        piece = x.astype(BF16)
        part = _dot(ones, piece)
        acc = part if acc is None else acc + part
        x = x - piece.astype(F32)
    return acc


def _rwkv_core_kernel(r_ref, k_ref, v_ref, lw_ref, a_ref, g_ref, kk_ref, ka_ref, rk_ref,
                      lng_ref, lnb_ref, o_ref, state_ref):
    c_len, w = RWKV_CHUNK, RWKV_LANES
    heads = w // RWKV_HEAD_DIM
    n = heads * c_len

    @pl.when(pl.program_id(2) == 0)
    def _():
        state_ref[...] = jnp.zeros_like(state_ref)

    rb = lax.broadcasted_iota(jnp.int32, (n, w), 0)
    lb = lax.broadcasted_iota(jnp.int32, (n, w), 1)
    head_match = (rb // c_len) == (lb // RWKV_HEAD_DIM)
    ones_bd = jnp.where((rb // RWKV_HEAD_DIM) == (lb // RWKV_HEAD_DIM), 1.0, 0.0).astype(BF16)
    tr = lax.broadcasted_iota(jnp.int32, (c_len, n), 0)
    ts = lax.broadcasted_iota(jnp.int32, (c_len, n), 1) % c_len
    strict = tr > ts
    incl = tr >= ts
    eye = jnp.where(tr == ts, 1.0, 0.0)
    cr = lax.broadcasted_iota(jnp.int32, (c_len, c_len), 0)
    cc = lax.broadcasted_iota(jnp.int32, (c_len, c_len), 1)
    tril_ones = jnp.where(cr >= cc, 1.0, 0.0).astype(BF16)

    def expand(x):
        xb = x.astype(BF16)
        return jnp.where(head_match, jnp.concatenate([xb] * heads, axis=0), jnp.zeros((), BF16))

    def seg_sum(x):
        return _split_dot(x, ones_bd, 2)

    def chunk(rows, grp):
        lanes = slice(grp * w, (grp + 1) * w)
        r, k, v = r_ref[rows, lanes], k_ref[rows, lanes], v_ref[rows, lanes]
        lw, a, g = lw_ref[rows, lanes], a_ref[rows, lanes], g_ref[rows, lanes]
        k_k, k_a, r_k = kk_ref[:, lanes], ka_ref[:, lanes], rk_ref[:, lanes]
        ln_g, ln_b = lng_ref[:, lanes], lnb_ref[:, lanes]

        kk = k * k_k
        kk_sq = seg_sum(kk * kk)
        kp = k * (1.0 + (a - 1.0) * k_a)
        rk_sum = seg_sum(r * kp * r_k)
        cum = _split_dot_left(tril_ones, lw, 3)
        yield
        kk = kk / jnp.maximum(jnp.sqrt(kk_sq), 1e-12)
        bonus = rk_sum * v
        cum_end = cum[c_len - 1:c_len, :]
        e_pos, e_neg = jnp.exp(cum), jnp.exp(-cum)
        e_tail = jnp.exp(cum_end - cum)
        ba = kk * a
        r_t = (r * e_pos).astype(BF16)
        a_t = (-kk * jnp.exp(cum - lw)).astype(BF16)
        b_t, k_t = ba * e_neg, kp * e_neg
        bk_h = jnp.concatenate([ba * e_tail, kp * e_tail], axis=0).astype(BF16)
        vb = v.astype(BF16)

        gram = _dot_nt(jnp.concatenate([a_t, r_t], axis=0),
                       jnp.concatenate([expand(b_t), expand(k_t)], axis=0))
        yield
        ab = jnp.where(strict, gram[:c_len, :n], 0.0)
        ak = jnp.where(strict, gram[:c_len, n:], 0.0).astype(BF16)
        rbm = jnp.where(incl, gram[c_len:, :n], 0.0).astype(BF16)
        rkm = jnp.where(incl, gram[c_len:, n:], 0.0).astype(BF16)

        tinv = eye + ab
        p = _dot(ab.astype(BF16), expand(ab))
        akv_rkv = _dot(jnp.concatenate([ak, rkm], axis=0), expand(v))
        yield
        for _ in range(c_len.bit_length() - 3):
            both = _dot(jnp.concatenate([p, tinv], axis=0).astype(BF16), expand(p))
            yield
            p, tinv = both[:c_len], tinv + both[c_len:]
        last = _dot(tinv.astype(BF16), expand(p))
        state = state_ref[grp]
        ag_rg = _dot_nt(jnp.concatenate([a_t, r_t], axis=0), state.astype(BF16))
        yield
        tb = (tinv + last).astype(BF16)
        u = _dot(tb, expand(akv_rkv[:c_len] + ag_rg[:c_len]))
        yield
        o = _dot(rbm, expand(u))
        upd = _dot_tn(jnp.concatenate([u.astype(BF16), vb], axis=0), bk_h)
        yield
        o = ag_rg[c_len:] + o + akv_rkv[c_len:]
        state_ref[grp] = state * jnp.exp(cum_end) + jnp.where(
            (rb // RWKV_HEAD_DIM) == (lb // RWKV_HEAD_DIM), upd, 0.0)
        mean = seg_sum(o) * (1.0 / RWKV_HEAD_DIM)
        yield
        cen = o - mean
        var = seg_sum(cen * cen) * (1.0 / RWKV_HEAD_DIM)
        yield
        y = (cen * lax.rsqrt(var + RWKV_GN_EPS)) * ln_g + ln_b
        o_ref[rows, lanes] = ((y + bonus) * g).astype(BF16)

    def body(c, carry):
        rows = pl.ds(pl.multiple_of(c * c_len, c_len), c_len)
        pending = [chunk(rows, grp) for grp in range(state_ref.shape[0])]
        while pending:
            pending = [gen for gen in pending if next(gen, True) is None]
        return carry

    lax.fori_loop(0, r_ref.shape[0] // c_len, body, 0)


def _rwkv_core(rkv, lw, a, g, k_k, k_a, r_k, ln_g, ln_b, batch, rows=256, groups=8):
    t, d = lw.shape
    s = t // batch
    rows = min(rows, s)
    w = RWKV_LANES * groups
    nq = d // w
    spb = s // rows
    seq = lambda off: pl.BlockSpec((rows, w), lambda b, q, i: (b * spb + i, off + q))
    par = pl.BlockSpec((1, w), lambda b, q, i: (0, q))
    vec = lambda x: x.reshape(1, d)
    return pl.pallas_call(
        _rwkv_core_kernel,
        grid=(batch, nq, spb),
        in_specs=[seq(0), seq(nq), seq(2 * nq), seq(0), seq(0), seq(0), par, par, par, par, par],
        out_specs=seq(0),
        out_shape=jax.ShapeDtypeStruct((t, d), BF16),
        scratch_shapes=[pltpu.VMEM((groups, RWKV_LANES, RWKV_LANES), F32)],
        compiler_params=_params(3),
        name="rwkv_core",
    )(rkv, rkv, rkv, lw, a, g, vec(k_k), vec(k_a), vec(r_k), vec(ln_g), vec(ln_b))


def _pad_cols(x, n):
    return jnp.pad(x, [(0, 0)] * (x.ndim - 1) + [(0, n - x.shape[-1])])


def _pad_rows(x, n):
    return jnp.pad(x, [(0, n - x.shape[0])] + [(0, 0)] * (x.ndim - 1))


def _round_up(x, m):
    return -(-x // m) * m


def _cast_pad_halves_kernel(x_ref, o_ref, *, half, padded):
    zeros = jnp.zeros((o_ref.shape[0], padded - half), o_ref.dtype)
    for k in range(2):
        o_ref[:, k * padded:k * padded + half] = x_ref[:, k * half:(k + 1) * half].astype(o_ref.dtype)
        o_ref[:, k * padded + half:(k + 1) * padded] = zeros


def _cast_pad_halves(x, half, padded, rows=128):
    n, r, _ = x.shape
    return pl.pallas_call(
        functools.partial(_cast_pad_halves_kernel, half=half, padded=padded),
        grid=(n, r // rows),
        in_specs=[pl.BlockSpec((None, rows, 2 * half), lambda l, i: (l, i, 0))],
        out_specs=pl.BlockSpec((None, rows, 2 * padded), lambda l, i: (l, i, 0)),
        out_shape=jax.ShapeDtypeStruct((n, r, 2 * padded), BF16),
        compiler_params=_params(2),
        name="cast_pad_halves",
    )(x)


def _cast_pad_rows_kernel(x_ref, o_ref):
    rows = x_ref.shape[0]
    o_ref[0:rows, :] = x_ref[...].astype(o_ref.dtype)
    o_ref[rows:, :] = jnp.zeros((o_ref.shape[0] - rows, o_ref.shape[1]), o_ref.dtype)


def _cast_pad_rows(x, padded, cols=256):
    n, r, c = x.shape
    return pl.pallas_call(
        _cast_pad_rows_kernel,
        grid=(n, c // cols),
        in_specs=[pl.BlockSpec((None, r, cols), lambda l, j: (l, 0, j))],
        out_specs=pl.BlockSpec((None, padded, cols), lambda l, j: (l, 0, j)),
        out_shape=jax.ShapeDtypeStruct((n, padded, c), BF16),
        compiler_params=_params(2),
        name="cast_pad_rows",
    )(x)


def _pad_val_gate(x, d_ff, fp):
    return jnp.concatenate([_pad_cols(x[..., :d_ff], fp), _pad_cols(x[..., d_ff:], fp)], axis=-1)


def kernel(x, c, positions, ada_w, ada_b, ffn_w_up, ffn_conv_w, ffn_conv_b, ffn_w_down, sg_w_in, sg_ln_g, sg_ln_b, sg_w_s, sg_b_s, sg_w_out, ret_w_in, ret_gn_g, ret_gn_b, ret_w_out, rwkv_mu, rwkv_w_rkv, rwkv_w0, rwkv_w1, rwkv_w2, rwkv_a0, rwkv_a1, rwkv_a2, rwkv_g1, rwkv_g2, rwkv_k_k, rwkv_k_a, rwkv_r_k, rwkv_ln_g, rwkv_ln_b, rwkv_w_out, final_norm_g):
    batch, seq, d = x.shape
    depth = ada_w.shape[0]
    t = batch * seq
    d_ff = ffn_w_down.shape[1]
    fp = _round_up(d_ff, FFN_SUBCHUNKS * FFN_SUB)

    sg_w_out_b, ret_w_out_b = sg_w_out.astype(BF16), ret_w_out.astype(BF16)
    rwkv_w_rkv_b, rwkv_w_out_b = rwkv_w_rkv.astype(BF16), rwkv_w_out.astype(BF16)
    ffn_w_up_b = _cast_pad_halves(ffn_w_up, d_ff, fp)
    ffn_w_down_b = _cast_pad_rows(ffn_w_down, fp)
    ffn_conv_w_p = _pad_val_gate(ffn_conv_w, d_ff, fp)
    ffn_conv_b_p = _pad_val_gate(ffn_conv_b, d_ff, fp).reshape(depth, 1, 2 * fp)

    mod = _modulation(c, ada_w, ada_b)
    h = x.reshape(t, d)
    cos = sin = None

    for layer in range(depth):
        kind, j = layer % N_MIXERS, layer // N_MIXERS
        if kind == 0:
            z = _norm_matmul(h, mod, layer, 1, 0, sg_w_in, j, batch, n=sg_w_in.shape[-1], epilogue="gelu")
            h = _sg_core(z, h, mod, layer, batch, sg_ln_g[j], sg_ln_b[j], sg_w_s[j], sg_b_s[j],
                         sg_w_out_b, j)
        elif kind == 1:
            if cos is None:
                cos, sin = _rope_tables(positions, d // RET_HEADS // 2)
            qk = _norm_matmul(h, mod, layer, 1, 0, ret_w_in, j, batch, n=2 * d, epilogue="rope",
                              cos=cos, sin=sin)
            v = _norm_matmul(h, mod, layer, 1, 0, ret_w_in, j, batch, n=2 * d, col_offset=2 * d,
                             out_dtype=BF16)
            g = _norm_matmul(h, mod, layer, 1, 0, ret_w_in, j, batch, n=2 * d, col_offset=4 * d)
            o = _ret_core(qk, v, g, ret_gn_g[j], ret_gn_b[j], batch)
            h = _proj_residual(o, ret_w_out_b, j, h, mod, layer, 2, batch)
        else:
            rkv = _rwkv_rkv(h, mod, layer, batch, rwkv_mu[j], rwkv_w_rkv_b, j)
            lora = _round_up(rwkv_w1.shape[-1], V7X_LANES)
            lora_a = _round_up(rwkv_a1.shape[-1], V7X_LANES)
            lw, a, g = _rwkv_lora(
                h, mod, layer, batch, rwkv_mu[j], rwkv_w0[j],
                _pad_cols(rwkv_w1[j], lora).astype(BF16), _pad_rows(rwkv_w2[j], lora).astype(BF16),
                rwkv_a0[j],
                _pad_cols(rwkv_a1[j], lora_a).astype(BF16), _pad_rows(rwkv_a2[j], lora_a).astype(BF16),
                rwkv_g1[j].astype(BF16), rwkv_g2[j].astype(BF16))
            o = _rwkv_core(rkv, lw, a, g, rwkv_k_k[j], rwkv_k_a[j], rwkv_r_k[j],
                           rwkv_ln_g[j], rwkv_ln_b[j], batch)
            h = _proj_residual(o, rwkv_w_out_b, j, h, mod, layer, 2, batch)

        h = _conv_ffn(h, mod, layer, batch, ffn_w_up_b, ffn_conv_w_p, ffn_conv_b_p, ffn_w_down_b,
                      final_gain=final_norm_g if layer == depth - 1 else None)

    return h.reshape(batch, seq, d)
```

```python
import functools
import math

import jax
import jax.numpy as jnp
from jax import lax
from jax.experimental import pallas as pl
from jax.experimental.pallas import tpu as pltpu

F32 = jnp.float32
BF16 = jnp.bfloat16

NORM_EPS = 1e-6
LN_EPS = 1e-5
N_MOD = 6
N_MIXERS = 3

SG_CHUNK = 128
SG_GROUPS = 16

RET_HEADS = 8
RET_CHUNK = 128
ROPE_BASE = 10000.0

RWKV_HEAD_DIM = 64
RWKV_GN_EPS = RWKV_HEAD_DIM * 1e-5
RWKV_DECAY_OFFSET = 0.5
RWKV_CHUNK = 64
RWKV_LANES = 256

CONV_WIDTH = 3
FFN_SUB = 256
FFN_SUBCHUNKS = 2
MATMUL_SUBTILE = 512
WEIGHT_BLOCK_ELEMS = 2048 * 1024

V7X_LANES = 128
V7X_SUBLANES = 8
V7X_VMEM_LIMIT = 56 * 1024 * 1024


def _params(n_axes, vmem=V7X_VMEM_LIMIT):
    return pltpu.CompilerParams(dimension_semantics=("arbitrary",) * n_axes,
                                vmem_limit_bytes=vmem)


def _dot(a, b):
    return jnp.dot(a, b, preferred_element_type=F32)


def _dot_nt(a, b):
    return lax.dot_general(a, b, (((1,), (1,)), ((), ())), preferred_element_type=F32)


def _dot_tn(a, b):
    return lax.dot_general(a, b, (((0,), (0,)), ((), ())), preferred_element_type=F32)


def _rms_mod(h, sc, sh):
    ms = jnp.mean(h * h, axis=-1, keepdims=True)
    return (h * lax.rsqrt(ms + NORM_EPS)) * (1.0 + sc) + sh


def _mod_kernel(c_ref, w_ref, b_ref, o_ref):
    cond = jax.nn.silu(c_ref[...])
    o_ref[...] = _dot(cond.astype(BF16), w_ref[...].astype(BF16)) + b_ref[...]


def _modulation(c, ada_w, ada_b, tn=2048):
    depth, d, n = ada_w.shape
    b = c.shape[0]
    rows = -(-b // V7X_SUBLANES) * V7X_SUBLANES
    c_pad = jnp.pad(c, ((0, rows - b), (0, 0)))
    out = pl.pallas_call(
        _mod_kernel,
        grid=(depth, n // tn),
        in_specs=[
            pl.BlockSpec((rows, d), lambda l, j: (0, 0)),
            pl.BlockSpec((None, d, tn), lambda l, j: (l, 0, j)),
            pl.BlockSpec((None, 1, tn), lambda l, j: (l, 0, j)),
        ],
        out_specs=pl.BlockSpec((None, rows, tn), lambda l, j: (l, 0, j)),
        out_shape=jax.ShapeDtypeStruct((depth, rows, n), F32),
        compiler_params=_params(2),
        name="adaln_mod",
    )(c_pad, ada_w, ada_b.reshape(depth, 1, n))
    return out[:, :b].reshape(depth * b * N_MOD, 1, d)


def _mod_spec(d, layer, k, batch, tiles_per_batch):
    base = layer * batch * N_MOD + k
    return pl.BlockSpec((None, 1, d),
                        lambda i, *_: (base + (i // tiles_per_batch) * N_MOD, 0, 0))


def _norm_matmul_kernel(h_ref, sc_ref, sh_ref, w_ref, *rest, epilogue):
    if epilogue == "rope":
        cos_ref, sin_ref, o_ref, xm_ref = rest
    else:
        o_ref, xm_ref = rest

    @pl.when(pl.program_id(1) == 0)
    def _():
        xm_ref[...] = _rms_mod(h_ref[...], sc_ref[...], sh_ref[...]).astype(xm_ref.dtype)

    xm = xm_ref[...]
    tn = w_ref.shape[1]
    sub = min(tn, MATMUL_SUBTILE)
    for c in range(tn // sub):
        cols = slice(c * sub, (c + 1) * sub)
        y = _dot(xm, w_ref[:, cols])
        if epilogue == "gelu":
            y = jax.nn.gelu(y)
        elif epilogue == "rope":
            cos, sin = cos_ref[...], sin_ref[...]
            half = cos.shape[-1]
            parts = []
            for hd in range(sub // (2 * half)):
                x1 = y[:, 2 * hd * half:(2 * hd + 1) * half]
                x2 = y[:, (2 * hd + 1) * half:(2 * hd + 2) * half]
                parts += [x1 * cos - x2 * sin, x2 * cos + x1 * sin]
            y = jnp.concatenate(parts, axis=-1)
        o_ref[:, cols] = y.astype(o_ref.dtype)


def _norm_matmul(h, mod, layer, k_scale, k_shift, w, w_idx, batch, *, n, col_offset=0, epilogue="none",
                 out_dtype=F32, cos=None, sin=None, tm=1024, tn=1024):
    t, d = h.shape
    tm = min(tm, t // batch)
    tpb = (t // batch) // tm
    off = col_offset // tn
    in_specs = [
        pl.BlockSpec((tm, d), lambda i, j: (i, 0)),
        _mod_spec(d, layer, k_scale, batch, tpb),
        _mod_spec(d, layer, k_shift, batch, tpb),
        pl.BlockSpec((None, d, tn), lambda i, j: (w_idx, 0, off + j)),
    ]
    args = [h, mod, mod, w]
    if epilogue == "rope":
        half = cos.shape[-1]
        in_specs += [pl.BlockSpec((tm, half), lambda i, j: (i, 0))] * 2
        args += [cos, sin]
    return pl.pallas_call(
        functools.partial(_norm_matmul_kernel, epilogue=epilogue),
        grid=(t // tm, n // tn),
        in_specs=in_specs,
        out_specs=pl.BlockSpec((tm, tn), lambda i, j: (i, j)),
        out_shape=jax.ShapeDtypeStruct((t, n), out_dtype),
        scratch_shapes=[pltpu.VMEM((tm, d), w.dtype)],
        compiler_params=_params(2),
        name="norm_matmul_" + epilogue,
    )(*args)


def _proj_residual_kernel(a_ref, w_ref, h_ref, g_ref, o_ref):
    a = a_ref[...]
    tn = w_ref.shape[1]
    sub = min(tn, MATMUL_SUBTILE)
    for c in range(tn // sub):
        cols = slice(c * sub, (c + 1) * sub)
        o_ref[:, cols] = h_ref[:, cols] + g_ref[:, cols] * _dot(a, w_ref[:, cols])


def _proj_residual(a, w, w_idx, h, mod, layer, k_gate, batch, tm=1024, tn=1024):
    t, kdim = a.shape
    d = w.shape[-1]
    tn = min(tn, WEIGHT_BLOCK_ELEMS // kdim)
    tm = min(tm, t // batch)
    tpb = (t // batch) // tm
    base = layer * batch * N_MOD + k_gate
    return pl.pallas_call(
        _proj_residual_kernel,
        grid=(t // tm, d // tn),
        in_specs=[
            pl.BlockSpec((tm, kdim), lambda i, j: (i, 0)),
            pl.BlockSpec((None, kdim, tn), lambda i, j: (w_idx, 0, j)),
            pl.BlockSpec((tm, tn), lambda i, j: (i, j)),
            pl.BlockSpec((None, 1, tn), lambda i, j: (base + (i // tpb) * N_MOD, 0, j)),
        ],
        out_specs=pl.BlockSpec((tm, tn), lambda i, j: (i, j)),
        out_shape=jax.ShapeDtypeStruct((t, d), F32),
        compiler_params=_params(2),
        name="proj_residual",
    )(a, w, h, mod)


def _sg_core_kernel(u_ref, v_ref, h_ref, g_ref, lng_ref, lnb_ref, ws_ref, bs_ref, wo_ref,
                    o_ref, gated_ref):
    tm, width = v_ref.shape
    gdim = width // SG_GROUPS
    v = v_ref[...]
    mu = jnp.mean(v, axis=-1, keepdims=True)
    var = jnp.mean(jnp.square(v - mu), axis=-1, keepdims=True)
    vn = (((v - mu) * lax.rsqrt(var + LN_EPS)) * lng_ref[...] + lnb_ref[...]).astype(BF16)
    row = lax.broadcasted_iota(jnp.int32, (SG_CHUNK, SG_CHUNK), 0)
    col = lax.broadcasted_iota(jnp.int32, (SG_CHUNK, SG_CHUNK), 1)
    causal = row >= col
    for g in range(SG_GROUPS):
        w_causal = jnp.where(causal, ws_ref[g], 0.0).astype(BF16)
        bias = bs_ref[:, g:g + 1]
        cols = slice(g * gdim, (g + 1) * gdim)
        for c in range(tm // SG_CHUNK):
            rows = slice(c * SG_CHUNK, (c + 1) * SG_CHUNK)
            sv = _dot(w_causal, vn[rows, cols]) + bias
            gated_ref[rows, cols] = (u_ref[rows, cols] * sv).astype(BF16)
    o_ref[...] = h_ref[...] + g_ref[...] * _dot(gated_ref[...], wo_ref[...])


def _sg_core(z, h, mod, layer, batch, ln_g, ln_b, w_s, b_s, w_out, w_idx, tm=256):
    t, d = h.shape
    width = z.shape[1] // 2
    tm = min(tm, t // batch)
    tpb = (t // batch) // tm
    return pl.pallas_call(
        _sg_core_kernel,
        grid=(t // tm,),
        in_specs=[
            pl.BlockSpec((tm, width), lambda i: (i, 0)),
            pl.BlockSpec((tm, width), lambda i: (i, 1)),
            pl.BlockSpec((tm, d), lambda i: (i, 0)),
            _mod_spec(d, layer, 2, batch, tpb),
            pl.BlockSpec((1, width), lambda i: (0, 0)),
            pl.BlockSpec((1, width), lambda i: (0, 0)),
            pl.BlockSpec((SG_GROUPS, SG_CHUNK, SG_CHUNK), lambda i: (0, 0, 0)),
            pl.BlockSpec((SG_CHUNK, SG_GROUPS), lambda i: (0, 0)),
            pl.BlockSpec((None, width, d), lambda i: (w_idx, 0, 0)),
        ],
        out_specs=pl.BlockSpec((tm, d), lambda i: (i, 0)),
        out_shape=jax.ShapeDtypeStruct((t, d), F32),
        scratch_shapes=[pltpu.VMEM((tm, width), BF16)],
        compiler_params=_params(1),
        name="sg_core",
    )(z, z, h, mod, ln_g.reshape(1, width), ln_b.reshape(1, width), w_s, b_s.T, w_out)


def _ffn_kernel(h_ref, sc_ref, sh_ref, g_ref, wv_ref, wg_ref, cwv_ref, cwg_ref, cbv_ref, cbg_ref,
                wd_ref, *rest, tiles_per_batch, n_tiles, final_norm):
    if final_norm:
        fg_ref, o_ref, xm_ref, halo_ref = rest[:4]
    else:
        o_ref, xm_ref, halo_ref = rest[:3]
    hbufs = rest[-FFN_SUBCHUNKS:]
    i, j = pl.program_id(0), pl.program_id(1)
    tm = h_ref.shape[0]
    pad = V7X_SUBLANES
    sub = FFN_SUB

    @pl.when(j == 0)
    def _():
        xm_ref[...] = _rms_mod(h_ref[...], sc_ref[...], sh_ref[...]).astype(BF16)
        o_ref[...] = jnp.zeros_like(o_ref)

    seq_start = (i % tiles_per_batch) == 0
    xm = xm_ref[...]
    for k, hb in enumerate(hbufs):
        cols = slice(k * sub, (k + 1) * sub)
        hb[0:pad, :] = jnp.where(seq_start, 0.0, halo_ref[j, k])
        hb[pad:, 0:sub] = _dot(xm, wv_ref[:, cols])
        hb[pad:, sub:] = _dot(xm, wg_ref[:, cols])
        halo_ref[j, k] = hb[tm:, :]

    def conv(hb, lanes, cw, cb):
        acc = hb[pad - 2:pad - 2 + tm, lanes] * cw[0:1, :]
        acc = acc + hb[pad - 1:pad - 1 + tm, lanes] * cw[1:2, :]
        acc = acc + hb[pad:pad + tm, lanes] * cw[2:3, :]
        return acc + cb

    for k, hb in enumerate(hbufs):
        cols = slice(k * sub, (k + 1) * sub)
        val = conv(hb, slice(0, sub), cwv_ref[:, cols], cbv_ref[:, cols])
        gate = conv(hb, slice(sub, 2 * sub), cwg_ref[:, cols], cbg_ref[:, cols])
        act = (jax.nn.silu(gate) * val).astype(BF16)
        o_ref[...] += _dot(act, wd_ref[cols, :])

    @pl.when(j == n_tiles - 1)
    def _():
        out = h_ref[...] + g_ref[...] * o_ref[...]
        if final_norm:
            ms = jnp.mean(out * out, axis=-1, keepdims=True)
            out = (out * lax.rsqrt(ms + NORM_EPS)) * fg_ref[...]
        o_ref[...] = out


def _conv_ffn(h, mod, layer, batch, w_up, conv_w, conv_b, w_down, final_gain=None, tm=512):
    t, d = h.shape
    fp = w_down.shape[1]
    tf = FFN_SUBCHUNKS * FFN_SUB
    nf = fp // tf
    tm = min(tm, t // batch)
    tpb = (t // batch) // tm
    final_norm = final_gain is not None
    in_specs = [
        pl.BlockSpec((tm, d), lambda i, j: (i, 0)),
        _mod_spec(d, layer, 4, batch, tpb),
        _mod_spec(d, layer, 3, batch, tpb),
        _mod_spec(d, layer, 5, batch, tpb),
        pl.BlockSpec((None, d, tf), lambda i, j: (layer, 0, j)),
        pl.BlockSpec((None, d, tf), lambda i, j: (layer, 0, nf + j)),
        pl.BlockSpec((None, CONV_WIDTH, tf), lambda i, j: (layer, 0, j)),
        pl.BlockSpec((None, CONV_WIDTH, tf), lambda i, j: (layer, 0, nf + j)),
        pl.BlockSpec((None, 1, tf), lambda i, j: (layer, 0, j)),
        pl.BlockSpec((None, 1, tf), lambda i, j: (layer, 0, nf + j)),
        pl.BlockSpec((None, tf, d), lambda i, j: (layer, j, 0)),
    ]
    args = [h, mod, mod, mod, w_up, w_up, conv_w, conv_w, conv_b, conv_b, w_down]
    if final_norm:
        in_specs.append(pl.BlockSpec((1, d), lambda i, j: (0, 0)))
        args.append(final_gain.reshape(1, d))
    return pl.pallas_call(
        functools.partial(_ffn_kernel, tiles_per_batch=tpb, n_tiles=nf, final_norm=final_norm),
        grid=(t // tm, nf),
        in_specs=in_specs,
        out_specs=pl.BlockSpec((tm, d), lambda i, j: (i, 0)),
        out_shape=jax.ShapeDtypeStruct((t, d), F32),
        scratch_shapes=[
            pltpu.VMEM((tm, d), BF16),
            pltpu.VMEM((nf, FFN_SUBCHUNKS, V7X_SUBLANES, 2 * FFN_SUB), F32),
        ] + [pltpu.VMEM((tm + V7X_SUBLANES, 2 * FFN_SUB), F32)] * FFN_SUBCHUNKS,
        compiler_params=_params(2),
        name="conv_ffn",
    )(*args)


def _rope_table_kernel(pos_ref, freq_ref, cos_ref, sin_ref):
    ang = pos_ref[...].astype(F32) * freq_ref[...]
    cos_ref[...] = jnp.cos(ang)
    sin_ref[...] = jnp.sin(ang)


def _rope_tables(positions, half, tm=1024):
    t = positions.size
    inv_freq = ROPE_BASE ** (-jnp.arange(half, dtype=F32) / half)
    tm = min(tm, t)
    return pl.pallas_call(
        _rope_table_kernel,
        grid=(t // tm,),
        in_specs=[pl.BlockSpec((tm, 1), lambda i: (i, 0)),
                  pl.BlockSpec((1, half), lambda i: (0, 0))],
        out_specs=[pl.BlockSpec((tm, half), lambda i: (i, 0))] * 2,
        out_shape=[jax.ShapeDtypeStruct((t, half), F32)] * 2,
        compiler_params=_params(1),
        name="rope_tables",
    )(positions.reshape(t, 1), inv_freq.reshape(1, half))


def _ret_core_kernel(q_ref, k_ref, v_ref, g_ref, di_ref, qd_ref, kd_ref, cd_ref, gg_ref, gb_ref,
                     o_ref, state_ref, *, k_scale):
    n_heads, dk, dv = state_ref.shape

    @pl.when(pl.program_id(2) == 0)
    def _():
        state_ref[...] = jnp.zeros_like(state_ref)

    def chunk(rows, hd):
        kcols = slice(hd * dk, (hd + 1) * dk)
        vcols = slice(hd * dv, (hd + 1) * dv)
        q = q_ref[rows, kcols]
        k = k_ref[rows, kcols] * k_scale
        v = v_ref[rows, vcols]
        state = state_ref[hd]
        inner = _dot_nt(q.astype(BF16), k.astype(BF16))
        cross = _dot((q * qd_ref[hd]).astype(BF16), state.astype(BF16))
        update = _dot_tn((k * kd_ref[hd]).astype(BF16), v)
        yield
        state_ref[hd] = state * cd_ref[hd] + update
        out = _dot((inner * di_ref[hd]).astype(BF16), v)
        yield
        out = out + cross
        mu = jnp.mean(out, axis=-1, keepdims=True)
        var = jnp.mean(jnp.square(out - mu), axis=-1, keepdims=True)
        y = ((out - mu) * lax.rsqrt(var + NORM_EPS)) * gg_ref[:, vcols] + gb_ref[:, vcols]
        o_ref[rows, vcols] = (jax.nn.silu(g_ref[rows, vcols]) * y).astype(BF16)

    def body(c, carry):
        rows = pl.ds(pl.multiple_of(c * RET_CHUNK, RET_CHUNK), RET_CHUNK)
        pending = [chunk(rows, hd) for hd in range(n_heads)]
        while pending:
            pending = [gen for gen in pending if next(gen, True) is None]
        return carry

    lax.fori_loop(0, q_ref.shape[0] // RET_CHUNK, body, 0)


def _ret_core(qk, v, g, gn_g, gn_b, batch, rows=256, group=8):
    t, two_d = qk.shape
    d = two_d // 2
    s = t // batch
    rows = min(rows, s)
    spb = s // rows
    n_groups = RET_HEADS // group
    dk = d // RET_HEADS
    dv = v.shape[1] // RET_HEADS
    log_gamma = jnp.log1p(-jnp.exp2(-5.0 - jnp.arange(RET_HEADS, dtype=F32)))
    idx = jnp.arange(RET_CHUNK, dtype=F32)
    rel = idx[:, None] - idx[None, :]
    decay_inner = jnp.where(rel >= 0, jnp.exp(log_gamma[:, None, None] * jnp.maximum(rel, 0.0)), 0.0)
    q_decay = jnp.exp(log_gamma[:, None] * (idx + 1.0))[..., None]
    k_decay = jnp.exp(log_gamma[:, None] * (RET_CHUNK - 1.0 - idx))[..., None]
    chunk_decay = jnp.exp(log_gamma * RET_CHUNK)[:, None, None]
    return pl.pallas_call(
        functools.partial(_ret_core_kernel, k_scale=dk ** -0.5),
        grid=(batch, n_groups, spb),
        in_specs=[
            pl.BlockSpec((rows, group * dk), lambda b, h, i: (b * spb + i, h)),
            pl.BlockSpec((rows, group * dk), lambda b, h, i: (b * spb + i, n_groups + h)),
            pl.BlockSpec((rows, group * dv), lambda b, h, i: (b * spb + i, h)),
            pl.BlockSpec((rows, group * dv), lambda b, h, i: (b * spb + i, h)),
            pl.BlockSpec((group, RET_CHUNK, RET_CHUNK), lambda b, h, i: (h, 0, 0)),
            pl.BlockSpec((group, RET_CHUNK, 1), lambda b, h, i: (h, 0, 0)),
            pl.BlockSpec((group, RET_CHUNK, 1), lambda b, h, i: (h, 0, 0)),
            pl.BlockSpec((group, 1, 1), lambda b, h, i: (h, 0, 0)),
            pl.BlockSpec((1, group * dv), lambda b, h, i: (0, h)),
            pl.BlockSpec((1, group * dv), lambda b, h, i: (0, h)),
        ],
        out_specs=pl.BlockSpec((rows, group * dv), lambda b, h, i: (b * spb + i, h)),
        out_shape=jax.ShapeDtypeStruct((t, v.shape[1]), BF16),
        scratch_shapes=[pltpu.VMEM((group, dk, dv), F32)],
        compiler_params=_params(3),
        name="ret_core",
    )(qk, qk, v, g, decay_inner, q_decay, k_decay, chunk_decay,
      gn_g.reshape(1, -1), gn_b.reshape(1, -1))


def _shifted_inputs(h_ref, hprev_ref, sc_ref, sh_ref, seq_start):
    sc, sh = sc_ref[...], sh_ref[...]
    xm = _rms_mod(h_ref[...], sc, sh)
    prev = _rms_mod(hprev_ref[...], sc, sh)
    last = prev[V7X_SUBLANES - 1:V7X_SUBLANES, :]
    last = jnp.where(seq_start, 0.0, last)
    row = lax.broadcasted_iota(jnp.int32, xm.shape, 0)
    shifted = jnp.where(row == 0, last, pltpu.roll(xm, 1, 0))
    return xm, shifted - xm


def _rwkv_rkv_kernel(h_ref, hprev_ref, sc_ref, sh_ref, mu_ref, w_ref, o_ref, xm_ref, dx_ref, xs_ref,
                     *, tiles_per_batch, tiles_per_proj):
    i, j = pl.program_id(0), pl.program_id(1)

    @pl.when(j == 0)
    def _():
        xm, dx = _shifted_inputs(h_ref, hprev_ref, sc_ref, sh_ref, (i % tiles_per_batch) == 0)
        xm_ref[...] = xm
        dx_ref[...] = dx

    @pl.when(j % tiles_per_proj == 0)
    def _():
        xs_ref[...] = (xm_ref[...] + dx_ref[...] * mu_ref[...]).astype(BF16)

    xs = xs_ref[...]
    tn = w_ref.shape[1]
    sub = min(tn, MATMUL_SUBTILE)
    for c in range(tn // sub):
        cols = slice(c * sub, (c + 1) * sub)
        o_ref[:, cols] = _dot(xs, w_ref[:, cols])


def _prev_rows_spec(tm, d):
    blocks = tm // V7X_SUBLANES
    return pl.BlockSpec((V7X_SUBLANES, d), lambda i, *_: (jnp.maximum(i * blocks - 1, 0), 0))


def _rwkv_rkv(h, mod, layer, batch, mu, w_rkv, w_idx, tm=512, tn=2048):
    t, d = h.shape
    tm = min(tm, t // batch)
    tpb = (t // batch) // tm
    tpp = d // tn
    return pl.pallas_call(
        functools.partial(_rwkv_rkv_kernel, tiles_per_batch=tpb, tiles_per_proj=tpp),
        grid=(t // tm, 3 * tpp),
        in_specs=[
            pl.BlockSpec((tm, d), lambda i, j: (i, 0)),
            _prev_rows_spec(tm, d),
            _mod_spec(d, layer, 1, batch, tpb),
            _mod_spec(d, layer, 0, batch, tpb),
            pl.BlockSpec((None, 1, d), lambda i, j: (j // tpp, 0, 0)),
            pl.BlockSpec((None, None, d, tn), lambda i, j: (w_idx, j // tpp, 0, j % tpp)),
        ],
        out_specs=pl.BlockSpec((tm, tn), lambda i, j: (i, j)),
        out_shape=jax.ShapeDtypeStruct((t, 3 * d), F32),
        scratch_shapes=[pltpu.VMEM((tm, d), F32), pltpu.VMEM((tm, d), F32), pltpu.VMEM((tm, d), BF16)],
        compiler_params=_params(2),
        name="rwkv_rkv",
    )(h, h, mod, mod, mu.reshape(-1, 1, d), w_rkv)


def _rwkv_lora_kernel(h_ref, hprev_ref, sc_ref, sh_ref, mu_ref, w0_ref, w1_ref, w2_ref,
                      a0_ref, a1_ref, a2_ref, g1_ref, g2_ref, lw_ref, a_ref, g_ref, *, tiles_per_batch):
    i = pl.program_id(0)
    xm, dx = _shifted_inputs(h_ref, hprev_ref, sc_ref, sh_ref, (i % tiles_per_batch) == 0)

    def mixed(p):
        return (xm + dx * mu_ref[p]).astype(BF16)

    lora_w = _dot(jnp.tanh(_dot(mixed(3), w1_ref[...])).astype(BF16), w2_ref[...])
    lw_ref[...] = -math.exp(-RWKV_DECAY_OFFSET) * jax.nn.sigmoid(w0_ref[...] + lora_w)
    lora_a = _dot(_dot(mixed(4), a1_ref[...]).astype(BF16), a2_ref[...])
    a_ref[...] = jax.nn.sigmoid(a0_ref[...] + lora_a)
    g_ref[...] = _dot(jax.nn.sigmoid(_dot(mixed(5), g1_ref[...])).astype(BF16), g2_ref[...])


def _rwkv_lora(h, mod, layer, batch, mu, w0, w1, w2, a0, a1, a2, g1, g2, tm=256):
    t, d = h.shape
    tm = min(tm, t // batch)
    tpb = (t // batch) // tm
    full = lambda a: pl.BlockSpec(a.shape, lambda i: (0,) * a.ndim)
    mu3 = mu.reshape(-1, 1, d)
    w0, a0 = w0.reshape(1, d), a0.reshape(1, d)
    consts = [mu3, w0, w1, w2, a0, a1, a2, g1, g2]
    return pl.pallas_call(
        functools.partial(_rwkv_lora_kernel, tiles_per_batch=tpb),
        grid=(t // tm,),
        in_specs=[
            pl.BlockSpec((tm, d), lambda i: (i, 0)),
            _prev_rows_spec(tm, d),
            _mod_spec(d, layer, 1, batch, tpb),
            _mod_spec(d, layer, 0, batch, tpb),
        ] + [full(a) for a in consts],
        out_specs=[pl.BlockSpec((tm, d), lambda i: (i, 0))] * 3,
        out_shape=[jax.ShapeDtypeStruct((t, d), F32)] * 3,
        compiler_params=_params(1),
        name="rwkv_lora",
    )(h, h, mod, mod, *consts)


def _split_dot(x, ones, terms):
    acc = None
    for _ in range(terms):
        piece = x.astype(BF16)
        part = _dot(piece, ones)
        acc = part if acc is None else acc + part
        x = x - piece.astype(F32)
    return acc


def _split_dot_left(ones, x, terms):
    acc = None
    for _ in range(terms):
        piece = x.astype(BF16)
        part = _dot(ones, piece)
        acc = part if acc is None else acc + part
        x = x - piece.astype(F32)
    return acc


def _rwkv_core_kernel(r_ref, k_ref, v_ref, lw_ref, a_ref, g_ref, kk_ref, ka_ref, rk_ref,
                      lng_ref, lnb_ref, o_ref, state_ref):
    c_len, w = RWKV_CHUNK, RWKV_LANES
    heads = w // RWKV_HEAD_DIM
    n = heads * c_len

    @pl.when(pl.program_id(2) == 0)
    def _():
        state_ref[...] = jnp.zeros_like(state_ref)

    rb = lax.broadcasted_iota(jnp.int32, (n, w), 0)
    lb = lax.broadcasted_iota(jnp.int32, (n, w), 1)
    head_match = (rb // c_len) == (lb // RWKV_HEAD_DIM)
    ones_bd = jnp.where((rb // RWKV_HEAD_DIM) == (lb // RWKV_HEAD_DIM), 1.0, 0.0).astype(BF16)
    tr = lax.broadcasted_iota(jnp.int32, (c_len, n), 0)
    ts = lax.broadcasted_iota(jnp.int32, (c_len, n), 1) % c_len
    strict = tr > ts
    incl = tr >= ts
    eye = jnp.where(tr == ts, 1.0, 0.0)
    cr = lax.broadcasted_iota(jnp.int32, (c_len, c_len), 0)
    cc = lax.broadcasted_iota(jnp.int32, (c_len, c_len), 1)
    tril_ones = jnp.where(cr >= cc, 1.0, 0.0).astype(BF16)

    def expand(x):
        xb = x.astype(BF16)
        return jnp.where(head_match, jnp.concatenate([xb] * heads, axis=0), jnp.zeros((), BF16))

    lane = lax.broadcasted_iota(jnp.int32, (c_len, V7X_LANES), 1)
    first_head = lane < RWKV_HEAD_DIM

    def seg_sum(x):
        out = []
        for col in range(w // V7X_LANES):
            xc = x[:, col * V7X_LANES:(col + 1) * V7X_LANES]
            s0 = jnp.sum(jnp.where(first_head, xc, 0.0), axis=-1, keepdims=True)
            s1 = jnp.sum(jnp.where(first_head, 0.0, xc), axis=-1, keepdims=True)
            out.append(jnp.where(first_head, s0, s1))
        return jnp.concatenate(out, axis=1)

    def chunk(rows, grp):
        lanes = slice(grp * w, (grp + 1) * w)
        r, k, v = r_ref[rows, lanes], k_ref[rows, lanes], v_ref[rows, lanes]
        lw, a, g = lw_ref[rows, lanes], a_ref[rows, lanes], g_ref[rows, lanes]
        k_k, k_a, r_k = kk_ref[:, lanes], ka_ref[:, lanes], rk_ref[:, lanes]
        ln_g, ln_b = lng_ref[:, lanes], lnb_ref[:, lanes]

        kk = k * k_k
        kk_sq = seg_sum(kk * kk)
        kp = k * (1.0 + (a - 1.0) * k_a)
        rk_sum = seg_sum(r * kp * r_k)
        cum = _split_dot_left(tril_ones, lw, 3)
        yield
        kk = kk / jnp.maximum(jnp.sqrt(kk_sq), 1e-12)
        bonus = rk_sum * v
        cum_end = cum[c_len - 1:c_len, :]
        e_pos, e_neg = jnp.exp(cum), jnp.exp(-cum)
        e_tail = jnp.exp(cum_end - cum)
        ba = kk * a
        r_t = (r * e_pos).astype(BF16)
        a_t = (-kk * jnp.exp(cum - lw)).astype(BF16)
        b_t, k_t = ba * e_neg, kp * e_neg
        bk_h = jnp.concatenate([ba * e_tail, kp * e_tail], axis=0).astype(BF16)
        vb = v.astype(BF16)

        gram = _dot_nt(jnp.concatenate([a_t, r_t], axis=0),
                       jnp.concatenate([expand(b_t), expand(k_t)], axis=0))
        yield
        ab = jnp.where(strict, gram[:c_len, :n], 0.0)
        ak = jnp.where(strict, gram[:c_len, n:], 0.0).astype(BF16)
        rbm = jnp.where(incl, gram[c_len:, :n], 0.0).astype(BF16)
        rkm = jnp.where(incl, gram[c_len:, n:], 0.0).astype(BF16)

        tinv = eye + ab
        p = _dot(ab.astype(BF16), expand(ab))
        akv_rkv = _dot(jnp.concatenate([ak, rkm], axis=0), expand(v))
        yield
        for _ in range(c_len.bit_length() - 3):
            both = _dot(jnp.concatenate([p, tinv], axis=0).astype(BF16), expand(p))
            yield
            p, tinv = both[:c_len], tinv + both[c_len:]
        last = _dot(tinv.astype(BF16), expand(p))
        state = state_ref[grp]
        ag_rg = _dot_nt(jnp.concatenate([a_t, r_t], axis=0), state.astype(BF16))
        yield
        tb = (tinv + last).astype(BF16)
        u = _dot(tb, expand(akv_rkv[:c_len] + ag_rg[:c_len]))
        yield
        o = _dot(rbm, expand(u))
        upd = _dot_tn(jnp.concatenate([u.astype(BF16), vb], axis=0), bk_h)
        yield
        o = ag_rg[c_len:] + o + akv_rkv[c_len:]
        state_ref[grp] = state * jnp.exp(cum_end) + jnp.where(
            (rb // RWKV_HEAD_DIM) == (lb // RWKV_HEAD_DIM), upd, 0.0)
        mean = seg_sum(o) * (1.0 / RWKV_HEAD_DIM)
        yield
        cen = o - mean
        var = seg_sum(cen * cen) * (1.0 / RWKV_HEAD_DIM)
        yield
        y = (cen * lax.rsqrt(var + RWKV_GN_EPS)) * ln_g + ln_b
        o_ref[rows, lanes] = ((y + bonus) * g).astype(BF16)

    def body(c, carry):
        rows = pl.ds(pl.multiple_of(c * c_len, c_len), c_len)
        pending = [chunk(rows, grp) for grp in range(state_ref.shape[0])]
        while pending:
            pending = [gen for gen in pending if next(gen, True) is None]
        return carry

    lax.fori_loop(0, r_ref.shape[0] // c_len, body, 0)


def _rwkv_core(rkv, lw, a, g, k_k, k_a, r_k, ln_g, ln_b, batch, rows=256, groups=8):
    t, d = lw.shape
    s = t // batch
    rows = min(rows, s)
    w = RWKV_LANES * groups
    nq = d // w
    spb = s // rows
    seq = lambda off: pl.BlockSpec((rows, w), lambda b, q, i: (b * spb + i, off + q))
    par = pl.BlockSpec((1, w), lambda b, q, i: (0, q))
    vec = lambda x: x.reshape(1, d)
    return pl.pallas_call(
        _rwkv_core_kernel,
        grid=(batch, nq, spb),
        in_specs=[seq(0), seq(nq), seq(2 * nq), seq(0), seq(0), seq(0), par, par, par, par, par],
        out_specs=seq(0),
        out_shape=jax.ShapeDtypeStruct((t, d), BF16),
        scratch_shapes=[pltpu.VMEM((groups, RWKV_LANES, RWKV_LANES), F32)],
        compiler_params=_params(3),
        name="rwkv_core",
    )(rkv, rkv, rkv, lw, a, g, vec(k_k), vec(k_a), vec(r_k), vec(ln_g), vec(ln_b))


def _pad_cols(x, n):
    return jnp.pad(x, [(0, 0)] * (x.ndim - 1) + [(0, n - x.shape[-1])])


def _pad_rows(x, n):
    return jnp.pad(x, [(0, n - x.shape[0])] + [(0, 0)] * (x.ndim - 1))


def _round_up(x, m):
    return -(-x // m) * m


def _cast_pad_halves_kernel(x_ref, o_ref, *, half, padded):
    zeros = jnp.zeros((o_ref.shape[0], padded - half), o_ref.dtype)
    for k in range(2):
        o_ref[:, k * padded:k * padded + half] = x_ref[:, k * half:(k + 1) * half].astype(o_ref.dtype)
        o_ref[:, k * padded + half:(k + 1) * padded] = zeros


def _cast_pad_halves(x, half, padded, rows=128):
    n, r, _ = x.shape
    return pl.pallas_call(
        functools.partial(_cast_pad_halves_kernel, half=half, padded=padded),
        grid=(n, r // rows),
        in_specs=[pl.BlockSpec((None, rows, 2 * half), lambda l, i: (l, i, 0))],
        out_specs=pl.BlockSpec((None, rows, 2 * padded), lambda l, i: (l, i, 0)),
        out_shape=jax.ShapeDtypeStruct((n, r, 2 * padded), BF16),
        compiler_params=_params(2),
        name="cast_pad_halves",
    )(x)


def _cast_pad_rows_kernel(x_ref, o_ref):
    rows = x_ref.shape[0]
    o_ref[0:rows, :] = x_ref[...].astype(o_ref.dtype)
    o_ref[rows:, :] = jnp.zeros((o_ref.shape[0] - rows, o_ref.shape[1]), o_ref.dtype)


def _cast_pad_rows(x, padded, cols=256):
    n, r, c = x.shape
    return pl.pallas_call(
        _cast_pad_rows_kernel,
        grid=(n, c // cols),
        in_specs=[pl.BlockSpec((None, r, cols), lambda l, j: (l, 0, j))],
        out_specs=pl.BlockSpec((None, padded, cols), lambda l, j: (l, 0, j)),
        out_shape=jax.ShapeDtypeStruct((n, padded, c), BF16),
        compiler_params=_params(2),
        name="cast_pad_rows",
    )(x)


def _pad_val_gate(x, d_ff, fp):
    return jnp.concatenate([_pad_cols(x[..., :d_ff], fp), _pad_cols(x[..., d_ff:], fp)], axis=-1)


def kernel(x, c, positions, ada_w, ada_b, ffn_w_up, ffn_conv_w, ffn_conv_b, ffn_w_down, sg_w_in, sg_ln_g, sg_ln_b, sg_w_s, sg_b_s, sg_w_out, ret_w_in, ret_gn_g, ret_gn_b, ret_w_out, rwkv_mu, rwkv_w_rkv, rwkv_w0, rwkv_w1, rwkv_w2, rwkv_a0, rwkv_a1, rwkv_a2, rwkv_g1, rwkv_g2, rwkv_k_k, rwkv_k_a, rwkv_r_k, rwkv_ln_g, rwkv_ln_b, rwkv_w_out, final_norm_g):
    batch, seq, d = x.shape
    depth = ada_w.shape[0]
    t = batch * seq
    d_ff = ffn_w_down.shape[1]
    fp = _round_up(d_ff, FFN_SUBCHUNKS * FFN_SUB)

    sg_w_out_b, ret_w_out_b = sg_w_out.astype(BF16), ret_w_out.astype(BF16)
    rwkv_w_rkv_b, rwkv_w_out_b = rwkv_w_rkv.astype(BF16), rwkv_w_out.astype(BF16)
    ffn_w_up_b = _cast_pad_halves(ffn_w_up, d_ff, fp)
    ffn_w_down_b = _cast_pad_rows(ffn_w_down, fp)
    ffn_conv_w_p = _pad_val_gate(ffn_conv_w, d_ff, fp)
    ffn_conv_b_p = _pad_val_gate(ffn_conv_b, d_ff, fp).reshape(depth, 1, 2 * fp)

    mod = _modulation(c, ada_w, ada_b)
    h = x.reshape(t, d)
    cos = sin = None

    for layer in range(depth):
        kind, j = layer % N_MIXERS, layer // N_MIXERS
        if kind == 0:
            z = _norm_matmul(h, mod, layer, 1, 0, sg_w_in, j, batch, n=sg_w_in.shape[-1], epilogue="gelu")
            h = _sg_core(z, h, mod, layer, batch, sg_ln_g[j], sg_ln_b[j], sg_w_s[j], sg_b_s[j],
                         sg_w_out_b, j)
        elif kind == 1:
            if cos is None:
                cos, sin = _rope_tables(positions, d // RET_HEADS // 2)
            qk = _norm_matmul(h, mod, layer, 1, 0, ret_w_in, j, batch, n=2 * d, epilogue="rope",
                              cos=cos, sin=sin)
            v = _norm_matmul(h, mod, layer, 1, 0, ret_w_in, j, batch, n=2 * d, col_offset=2 * d,
                             out_dtype=BF16)
            g = _norm_matmul(h, mod, layer, 1, 0, ret_w_in, j, batch, n=2 * d, col_offset=4 * d)
            o = _ret_core(qk, v, g, ret_gn_g[j], ret_gn_b[j], batch)
            h = _proj_residual(o, ret_w_out_b, j, h, mod, layer, 2, batch)
        else:
            rkv = _rwkv_rkv(h, mod, layer, batch, rwkv_mu[j], rwkv_w_rkv_b, j)
            lora = _round_up(rwkv_w1.shape[-1], V7X_LANES)
            lora_a = _round_up(rwkv_a1.shape[-1], V7X_LANES)
            lw, a, g = _rwkv_lora(
                h, mod, layer, batch, rwkv_mu[j], rwkv_w0[j],
                _pad_cols(rwkv_w1[j], lora).astype(BF16), _pad_rows(rwkv_w2[j], lora).astype(BF16),
                rwkv_a0[j],
                _pad_cols(rwkv_a1[j], lora_a).astype(BF16), _pad_rows(rwkv_a2[j], lora_a).astype(BF16),
                rwkv_g1[j].astype(BF16), rwkv_g2[j].astype(BF16))
            o = _rwkv_core(rkv, lw, a, g, rwkv_k_k[j], rwkv_k_a[j], rwkv_r_k[j],
                           rwkv_ln_g[j], rwkv_ln_b[j], batch)
            h = _proj_residual(o, rwkv_w_out_b, j, h, mod, layer, 2, batch)

        h = _conv_ffn(h, mod, layer, batch, ffn_w_up_b, ffn_conv_w_p, ffn_conv_b_p, ffn_w_down_b,
                      final_gain=final_norm_g if layer == depth - 1 else None)

    return h.reshape(batch, seq, d)
```

```python
import functools
import math

import jax
import jax.numpy as jnp
from jax import lax
from jax.experimental import pallas as pl
from jax.experimental.pallas import tpu as pltpu

F32 = jnp.float32
BF16 = jnp.bfloat16

NORM_EPS = 1e-6
LN_EPS = 1e-5
N_MOD = 6
N_MIXERS = 3

SG_CHUNK = 128
SG_GROUPS = 16

RET_HEADS = 8
RET_CHUNK = 128
ROPE_BASE = 10000.0

RWKV_HEAD_DIM = 64
RWKV_GN_EPS = RWKV_HEAD_DIM * 1e-5
RWKV_DECAY_OFFSET = 0.5
RWKV_CHUNK = 64
RWKV_LANES = 256

CONV_WIDTH = 3
FFN_SUB = 256
FFN_SUBCHUNKS = 2
MATMUL_SUBTILE = 512
WEIGHT_BLOCK_ELEMS = 2048 * 1024

V7X_LANES = 128
V7X_SUBLANES = 8
V7X_VMEM_LIMIT = 56 * 1024 * 1024


def _params(n_axes, vmem=V7X_VMEM_LIMIT):
    return pltpu.CompilerParams(dimension_semantics=("arbitrary",) * n_axes,
                                vmem_limit_bytes=vmem)


def _dot(a, b):
    return jnp.dot(a, b, preferred_element_type=F32)


def _dot_nt(a, b):
    return lax.dot_general(a, b, (((1,), (1,)), ((), ())), preferred_element_type=F32)


def _dot_tn(a, b):
    return lax.dot_general(a, b, (((0,), (0,)), ((), ())), preferred_element_type=F32)


def _rms_mod(h, sc, sh):
    ms = jnp.mean(h * h, axis=-1, keepdims=True)
    return (h * lax.rsqrt(ms + NORM_EPS)) * (1.0 + sc) + sh


def _mod_kernel(c_ref, w_ref, b_ref, o_ref):
    cond = jax.nn.silu(c_ref[...])
    o_ref[...] = _dot(cond.astype(BF16), w_ref[...].astype(BF16)) + b_ref[...]


def _modulation(c, ada_w, ada_b, tn=2048):
    depth, d, n = ada_w.shape
    b = c.shape[0]
    rows = -(-b // V7X_SUBLANES) * V7X_SUBLANES
    c_pad = jnp.pad(c, ((0, rows - b), (0, 0)))
    out = pl.pallas_call(
        _mod_kernel,
        grid=(depth, n // tn),
        in_specs=[
            pl.BlockSpec((rows, d), lambda l, j: (0, 0)),
            pl.BlockSpec((None, d, tn), lambda l, j: (l, 0, j)),
            pl.BlockSpec((None, 1, tn), lambda l, j: (l, 0, j)),
        ],
        out_specs=pl.BlockSpec((None, rows, tn), lambda l, j: (l, 0, j)),
        out_shape=jax.ShapeDtypeStruct((depth, rows, n), F32),
        compiler_params=_params(2),
        name="adaln_mod",
    )(c_pad, ada_w, ada_b.reshape(depth, 1, n))
    return out[:, :b].reshape(depth * b * N_MOD, 1, d)


def _mod_spec(d, layer, k, batch, tiles_per_batch):
    base = layer * batch * N_MOD + k
    return pl.BlockSpec((None, 1, d),
                        lambda i, *_: (base + (i // tiles_per_batch) * N_MOD, 0, 0))


def _norm_matmul_kernel(h_ref, sc_ref, sh_ref, w_ref, *rest, epilogue):
    if epilogue == "rope":
        cos_ref, sin_ref, o_ref, xm_ref = rest
    else:
        o_ref, xm_ref = rest

    @pl.when(pl.program_id(1) == 0)
    def _():
        xm_ref[...] = _rms_mod(h_ref[...], sc_ref[...], sh_ref[...]).astype(xm_ref.dtype)

    xm = xm_ref[...]
    tn = w_ref.shape[1]
    sub = min(tn, MATMUL_SUBTILE)
    for c in range(tn // sub):
        cols = slice(c * sub, (c + 1) * sub)
        y = _dot(xm, w_ref[:, cols])
        if epilogue == "gelu":
            y = jax.nn.gelu(y)
        elif epilogue == "rope":
            cos, sin = cos_ref[...], sin_ref[...]
            half = cos.shape[-1]
            parts = []
            for hd in range(sub // (2 * half)):
                x1 = y[:, 2 * hd * half:(2 * hd + 1) * half]
                x2 = y[:, (2 * hd + 1) * half:(2 * hd + 2) * half]
                parts += [x1 * cos - x2 * sin, x2 * cos + x1 * sin]
            y = jnp.concatenate(parts, axis=-1)
        o_ref[:, cols] = y.astype(o_ref.dtype)


def _norm_matmul(h, mod, layer, k_scale, k_shift, w, w_idx, batch, *, n, col_offset=0, epilogue="none",
                 out_dtype=F32, cos=None, sin=None, tm=1024, tn=1024):
    t, d = h.shape
    tm = min(tm, t // batch)
    tpb = (t // batch) // tm
    off = col_offset // tn
    in_specs = [
        pl.BlockSpec((tm, d), lambda i, j: (i, 0)),
        _mod_spec(d, layer, k_scale, batch, tpb),
        _mod_spec(d, layer, k_shift, batch, tpb),
        pl.BlockSpec((None, d, tn), lambda i, j: (w_idx, 0, off + j)),
    ]
    args = [h, mod, mod, w]
    if epilogue == "rope":
        half = cos.shape[-1]
        in_specs += [pl.BlockSpec((tm, half), lambda i, j: (i, 0))] * 2
        args += [cos, sin]
    return pl.pallas_call(
        functools.partial(_norm_matmul_kernel, epilogue=epilogue),
        grid=(t // tm, n // tn),
        in_specs=in_specs,
        out_specs=pl.BlockSpec((tm, tn), lambda i, j: (i, j)),
        out_shape=jax.ShapeDtypeStruct((t, n), out_dtype),
        scratch_shapes=[pltpu.VMEM((tm, d), w.dtype)],
        compiler_params=_params(2),
        name="norm_matmul_" + epilogue,
    )(*args)


def _proj_residual_kernel(a_ref, w_ref, h_ref, g_ref, o_ref):
    a = a_ref[...]
    tn = w_ref.shape[1]
    sub = min(tn, MATMUL_SUBTILE)
    for c in range(tn // sub):
        cols = slice(c * sub, (c + 1) * sub)
        o_ref[:, cols] = h_ref[:, cols] + g_ref[:, cols] * _dot(a, w_ref[:, cols])


def _proj_residual(a, w, w_idx, h, mod, layer, k_gate, batch, tm=1024, tn=1024):
    t, kdim = a.shape
    d = w.shape[-1]
    tn = min(tn, WEIGHT_BLOCK_ELEMS // kdim)
    tm = min(tm, t // batch)
    tpb = (t // batch) // tm
    base = layer * batch * N_MOD + k_gate
    return pl.pallas_call(
        _proj_residual_kernel,
        grid=(t // tm, d // tn),
        in_specs=[
            pl.BlockSpec((tm, kdim), lambda i, j: (i, 0)),
            pl.BlockSpec((None, kdim, tn), lambda i, j: (w_idx, 0, j)),
            pl.BlockSpec((tm, tn), lambda i, j: (i, j)),
            pl.BlockSpec((None, 1, tn), lambda i, j: (base + (i // tpb) * N_MOD, 0, j)),
        ],
        out_specs=pl.BlockSpec((tm, tn), lambda i, j: (i, j)),
        out_shape=jax.ShapeDtypeStruct((t, d), F32),
        compiler_params=_params(2),
        name="proj_residual",
    )(a, w, h, mod)


def _sg_core_kernel(u_ref, v_ref, h_ref, g_ref, lng_ref, lnb_ref, ws_ref, bs_ref, wo_ref,
                    o_ref, gated_ref):
    tm, width = v_ref.shape
    gdim = width // SG_GROUPS
    v = v_ref[...]
    mu = jnp.mean(v, axis=-1, keepdims=True)
    var = jnp.mean(jnp.square(v - mu), axis=-1, keepdims=True)
    vn = (((v - mu) * lax.rsqrt(var + LN_EPS)) * lng_ref[...] + lnb_ref[...]).astype(BF16)
    row = lax.broadcasted_iota(jnp.int32, (SG_CHUNK, SG_CHUNK), 0)
    col = lax.broadcasted_iota(jnp.int32, (SG_CHUNK, SG_CHUNK), 1)
    causal = row >= col
    for g in range(SG_GROUPS):
        w_causal = jnp.where(causal, ws_ref[g], 0.0).astype(BF16)
        bias = bs_ref[:, g:g + 1]
        cols = slice(g * gdim, (g + 1) * gdim)
        for c in range(tm // SG_CHUNK):
            rows = slice(c * SG_CHUNK, (c + 1) * SG_CHUNK)
            sv = _dot(w_causal, vn[rows, cols]) + bias
            gated_ref[rows, cols] = (u_ref[rows, cols] * sv).astype(BF16)
    o_ref[...] = h_ref[...] + g_ref[...] * _dot(gated_ref[...], wo_ref[...])


def _sg_core(z, h, mod, layer, batch, ln_g, ln_b, w_s, b_s, w_out, w_idx, tm=256):
    t, d = h.shape
    width = z.shape[1] // 2
    tm = min(tm, t // batch)
    tpb = (t // batch) // tm
    return pl.pallas_call(
        _sg_core_kernel,
        grid=(t // tm,),
        in_specs=[
            pl.BlockSpec((tm, width), lambda i: (i, 0)),
            pl.BlockSpec((tm, width), lambda i: (i, 1)),
            pl.BlockSpec((tm, d), lambda i: (i, 0)),
            _mod_spec(d, layer, 2, batch, tpb),
            pl.BlockSpec((1, width), lambda i: (0, 0)),
            pl.BlockSpec((1, width), lambda i: (0, 0)),
            pl.BlockSpec((SG_GROUPS, SG_CHUNK, SG_CHUNK), lambda i: (0, 0, 0)),
            pl.BlockSpec((SG_CHUNK, SG_GROUPS), lambda i: (0, 0)),
            pl.BlockSpec((None, width, d), lambda i: (w_idx, 0, 0)),
        ],
        out_specs=pl.BlockSpec((tm, d), lambda i: (i, 0)),
        out_shape=jax.ShapeDtypeStruct((t, d), F32),
        scratch_shapes=[pltpu.VMEM((tm, width), BF16)],
        compiler_params=_params(1),
        name="sg_core",
    )(z, z, h, mod, ln_g.reshape(1, width), ln_b.reshape(1, width), w_s, b_s.T, w_out)


def _ffn_kernel(h_ref, sc_ref, sh_ref, g_ref, wv_ref, wg_ref, cwv_ref, cwg_ref, cbv_ref, cbg_ref,
                wd_ref, *rest, tiles_per_batch, n_tiles, final_norm):
    if final_norm:
        fg_ref, o_ref, xm_ref, halo_ref = rest[:4]
    else:
        o_ref, xm_ref, halo_ref = rest[:3]
    hbufs = rest[-FFN_SUBCHUNKS:]
    i, j = pl.program_id(0), pl.program_id(1)
    tm = h_ref.shape[0]
    pad = V7X_SUBLANES
    sub = FFN_SUB

    @pl.when(j == 0)
    def _():
        xm_ref[...] = _rms_mod(h_ref[...], sc_ref[...], sh_ref[...]).astype(BF16)
        o_ref[...] = jnp.zeros_like(o_ref)

    seq_start = (i % tiles_per_batch) == 0
    xm = xm_ref[...]
    for k, hb in enumerate(hbufs):
        cols = slice(k * sub, (k + 1) * sub)
        hb[0:pad, :] = jnp.where(seq_start, 0.0, halo_ref[j, k])
        hb[pad:, 0:sub] = _dot(xm, wv_ref[:, cols])
        hb[pad:, sub:] = _dot(xm, wg_ref[:, cols])
        halo_ref[j, k] = hb[tm:, :]

    def conv(hb, lanes, cw, cb):
        acc = hb[pad - 2:pad - 2 + tm, lanes] * cw[0:1, :]
        acc = acc + hb[pad - 1:pad - 1 + tm, lanes] * cw[1:2, :]
        acc = acc + hb[pad:pad + tm, lanes] * cw[2:3, :]
        return acc + cb

    for k, hb in enumerate(hbufs):
        cols = slice(k * sub, (k + 1) * sub)
        val = conv(hb, slice(0, sub), cwv_ref[:, cols], cbv_ref[:, cols])
        gate = conv(hb, slice(sub, 2 * sub), cwg_ref[:, cols], cbg_ref[:, cols])
        act = (jax.nn.silu(gate) * val).astype(BF16)
        o_ref[...] += _dot(act, wd_ref[cols, :])

    @pl.when(j == n_tiles - 1)
    def _():
        out = h_ref[...] + g_ref[...] * o_ref[...]
        if final_norm:
            ms = jnp.mean(out * out, axis=-1, keepdims=True)
            out = (out * lax.rsqrt(ms + NORM_EPS)) * fg_ref[...]
        o_ref[...] = out


def _conv_ffn(h, mod, layer, batch, w_up, conv_w, conv_b, w_down, final_gain=None, tm=512):
    t, d = h.shape
    fp = w_down.shape[1]
    tf = FFN_SUBCHUNKS * FFN_SUB
    nf = fp // tf
    tm = min(tm, t // batch)
    tpb = (t // batch) // tm
    final_norm = final_gain is not None
    in_specs = [
        pl.BlockSpec((tm, d), lambda i, j: (i, 0)),
        _mod_spec(d, layer, 4, batch, tpb),
        _mod_spec(d, layer, 3, batch, tpb),
        _mod_spec(d, layer, 5, batch, tpb),
        pl.BlockSpec((None, d, tf), lambda i, j: (layer, 0, j)),
        pl.BlockSpec((None, d, tf), lambda i, j: (layer, 0, nf + j)),
        pl.BlockSpec((None, CONV_WIDTH, tf), lambda i, j: (layer, 0, j)),
        pl.BlockSpec((None, CONV_WIDTH, tf), lambda i, j: (layer, 0, nf + j)),
        pl.BlockSpec((None, 1, tf), lambda i, j: (layer, 0, j)),
        pl.BlockSpec((None, 1, tf), lambda i, j: (layer, 0, nf + j)),
        pl.BlockSpec((None, tf, d), lambda i, j: (layer, j, 0)),
    ]
    args = [h, mod, mod, mod, w_up, w_up, conv_w, conv_w, conv_b, conv_b, w_down]
    if final_norm:
        in_specs.append(pl.BlockSpec((1, d), lambda i, j: (0, 0)))
        args.append(final_gain.reshape(1, d))
    return pl.pallas_call(
        functools.partial(_ffn_kernel, tiles_per_batch=tpb, n_tiles=nf, final_norm=final_norm),
        grid=(t // tm, nf),
        in_specs=in_specs,
        out_specs=pl.BlockSpec((tm, d), lambda i, j: (i, 0)),
        out_shape=jax.ShapeDtypeStruct((t, d), F32),
        scratch_shapes=[
            pltpu.VMEM((tm, d), BF16),
            pltpu.VMEM((nf, FFN_SUBCHUNKS, V7X_SUBLANES, 2 * FFN_SUB), F32),
        ] + [pltpu.VMEM((tm + V7X_SUBLANES, 2 * FFN_SUB), F32)] * FFN_SUBCHUNKS,
        compiler_params=_params(2),
        name="conv_ffn",
    )(*args)


def _rope_table_kernel(pos_ref, freq_ref, cos_ref, sin_ref):
    ang = pos_ref[...].astype(F32) * freq_ref[...]
    cos_ref[...] = jnp.cos(ang)
    sin_ref[...] = jnp.sin(ang)


def _rope_tables(positions, half, tm=1024):
    t = positions.size
    inv_freq = ROPE_BASE ** (-jnp.arange(half, dtype=F32) / half)
    tm = min(tm, t)
    return pl.pallas_call(
        _rope_table_kernel,
        grid=(t // tm,),
        in_specs=[pl.BlockSpec((tm, 1), lambda i: (i, 0)),
                  pl.BlockSpec((1, half), lambda i: (0, 0))],
        out_specs=[pl.BlockSpec((tm, half), lambda i: (i, 0))] * 2,
        out_shape=[jax.ShapeDtypeStruct((t, half), F32)] * 2,
        compiler_params=_params(1),
        name="rope_tables",
    )(positions.reshape(t, 1), inv_freq.reshape(1, half))


def _ret_core_kernel(q_ref, k_ref, v_ref, g_ref, di_ref, qd_ref, kd_ref, cd_ref, gg_ref, gb_ref,
                     o_ref, state_ref, *, k_scale):
    n_heads, dk, dv = state_ref.shape

    @pl.when(pl.program_id(2) == 0)
    def _():
        state_ref[...] = jnp.zeros_like(state_ref)

    def chunk(rows, hd):
        kcols = slice(hd * dk, (hd + 1) * dk)
        vcols = slice(hd * dv, (hd + 1) * dv)
        q = q_ref[rows, kcols]
        k = k_ref[rows, kcols] * k_scale
        v = v_ref[rows, vcols]
        state = state_ref[hd]
        inner = _dot_nt(q.astype(BF16), k.astype(BF16))
        cross = _dot((q * qd_ref[hd]).astype(BF16), state.astype(BF16))
        update = _dot_tn((k * kd_ref[hd]).astype(BF16), v)
        yield
        state_ref[hd] = state * cd_ref[hd] + update
        out = _dot((inner * di_ref[hd]).astype(BF16), v)
        yield
        out = out + cross
        mu = jnp.mean(out, axis=-1, keepdims=True)
        var = jnp.mean(jnp.square(out - mu), axis=-1, keepdims=True)
        y = ((out - mu) * lax.rsqrt(var + NORM_EPS)) * gg_ref[:, vcols] + gb_ref[:, vcols]
        o_ref[rows, vcols] = (jax.nn.silu(g_ref[rows, vcols]) * y).astype(BF16)

    def body(c, carry):
        rows = pl.ds(pl.multiple_of(c * RET_CHUNK, RET_CHUNK), RET_CHUNK)
        pending = [chunk(rows, hd) for hd in range(n_heads)]
        while pending:
            pending = [gen for gen in pending if next(gen, True) is None]
        return carry

    lax.fori_loop(0, q_ref.shape[0] // RET_CHUNK, body, 0)


def _ret_core(qk, v, g, gn_g, gn_b, batch, rows=256, group=8):
    t, two_d = qk.shape
    d = two_d // 2
    s = t // batch
    rows = min(rows, s)
    spb = s // rows
    n_groups = RET_HEADS // group
    dk = d // RET_HEADS
    dv = v.shape[1] // RET_HEADS
    log_gamma = jnp.log1p(-jnp.exp2(-5.0 - jnp.arange(RET_HEADS, dtype=F32)))
    idx = jnp.arange(RET_CHUNK, dtype=F32)
    rel = idx[:, None] - idx[None, :]
    decay_inner = jnp.where(rel >= 0, jnp.exp(log_gamma[:, None, None] * jnp.maximum(rel, 0.0)), 0.0)
    q_decay = jnp.exp(log_gamma[:, None] * (idx + 1.0))[..., None]
    k_decay = jnp.exp(log_gamma[:, None] * (RET_CHUNK - 1.0 - idx))[..., None]
    chunk_decay = jnp.exp(log_gamma * RET_CHUNK)[:, None, None]
    return pl.pallas_call(
        functools.partial(_ret_core_kernel, k_scale=dk ** -0.5),
        grid=(batch, n_groups, spb),
        in_specs=[
            pl.BlockSpec((rows, group * dk), lambda b, h, i: (b * spb + i, h)),
            pl.BlockSpec((rows, group * dk), lambda b, h, i: (b * spb + i, n_groups + h)),
            pl.BlockSpec((rows, group * dv), lambda b, h, i: (b * spb + i, h)),
            pl.BlockSpec((rows, group * dv), lambda b, h, i: (b * spb + i, h)),
            pl.BlockSpec((group, RET_CHUNK, RET_CHUNK), lambda b, h, i: (h, 0, 0)),
            pl.BlockSpec((group, RET_CHUNK, 1), lambda b, h, i: (h, 0, 0)),
            pl.BlockSpec((group, RET_CHUNK, 1), lambda b, h, i: (h, 0, 0)),
            pl.BlockSpec((group, 1, 1), lambda b, h, i: (h, 0, 0)),
            pl.BlockSpec((1, group * dv), lambda b, h, i: (0, h)),
            pl.BlockSpec((1, group * dv), lambda b, h, i: (0, h)),
        ],
        out_specs=pl.BlockSpec((rows, group * dv), lambda b, h, i: (b * spb + i, h)),
        out_shape=jax.ShapeDtypeStruct((t, v.shape[1]), BF16),
        scratch_shapes=[pltpu.VMEM((group, dk, dv), F32)],
        compiler_params=_params(3),
        name="ret_core",
    )(qk, qk, v, g, decay_inner, q_decay, k_decay, chunk_decay,
      gn_g.reshape(1, -1), gn_b.reshape(1, -1))


def _shifted_inputs(h_ref, hprev_ref, sc_ref, sh_ref, seq_start):
    sc, sh = sc_ref[...], sh_ref[...]
    xm = _rms_mod(h_ref[...], sc, sh)
    prev = _rms_mod(hprev_ref[...], sc, sh)
    last = prev[V7X_SUBLANES - 1:V7X_SUBLANES, :]
    last = jnp.where(seq_start, 0.0, last)
    row = lax.broadcasted_iota(jnp.int32, xm.shape, 0)
    shifted = jnp.where(row == 0, last, pltpu.roll(xm, 1, 0))
    return xm, shifted - xm


def _rwkv_rkv_kernel(h_ref, hprev_ref, sc_ref, sh_ref, mu_ref, w_ref, o_ref, xm_ref, dx_ref, xs_ref,
                     *, tiles_per_batch, tiles_per_proj):
    i, j = pl.program_id(0), pl.program_id(1)

    @pl.when(j == 0)
    def _():
        xm, dx = _shifted_inputs(h_ref, hprev_ref, sc_ref, sh_ref, (i % tiles_per_batch) == 0)
        xm_ref[...] = xm
        dx_ref[...] = dx

    @pl.when(j % tiles_per_proj == 0)
    def _():
        xs_ref[...] = (xm_ref[...] + dx_ref[...] * mu_ref[...]).astype(BF16)

    xs = xs_ref[...]
    tn = w_ref.shape[1]
    sub = min(tn, MATMUL_SUBTILE)
    for c in range(tn // sub):
        cols = slice(c * sub, (c + 1) * sub)
        o_ref[:, cols] = _dot(xs, w_ref[:, cols])


def _prev_rows_spec(tm, d):
    blocks = tm // V7X_SUBLANES
    return pl.BlockSpec((V7X_SUBLANES, d), lambda i, *_: (jnp.maximum(i * blocks - 1, 0), 0))


def _rwkv_rkv(h, mod, layer, batch, mu, w_rkv, w_idx, tm=512, tn=2048):
    t, d = h.shape
    tm = min(tm, t // batch)
    tpb = (t // batch) // tm
    tpp = d // tn
    return pl.pallas_call(
        functools.partial(_rwkv_rkv_kernel, tiles_per_batch=tpb, tiles_per_proj=tpp),
        grid=(t // tm, 3 * tpp),
        in_specs=[
            pl.BlockSpec((tm, d), lambda i, j: (i, 0)),
            _prev_rows_spec(tm, d),
            _mod_spec(d, layer, 1, batch, tpb),
            _mod_spec(d, layer, 0, batch, tpb),
            pl.BlockSpec((None, 1, d), lambda i, j: (j // tpp, 0, 0)),
            pl.BlockSpec((None, None, d, tn), lambda i, j: (w_idx, j // tpp, 0, j % tpp)),
        ],
        out_specs=pl.BlockSpec((tm, tn), lambda i, j: (i, j)),
        out_shape=jax.ShapeDtypeStruct((t, 3 * d), F32),
        scratch_shapes=[pltpu.VMEM((tm, d), F32), pltpu.VMEM((tm, d), F32), pltpu.VMEM((tm, d), BF16)],
        compiler_params=_params(2),
        name="rwkv_rkv",
    )(h, h, mod, mod, mu.reshape(-1, 1, d), w_rkv)


def _rwkv_lora_kernel(h_ref, hprev_ref, sc_ref, sh_ref, mu_ref, w0_ref, w1_ref, w2_ref,
                      a0_ref, a1_ref, a2_ref, g1_ref, g2_ref, lw_ref, a_ref, g_ref, *, tiles_per_batch):
    i = pl.program_id(0)
    xm, dx = _shifted_inputs(h_ref, hprev_ref, sc_ref, sh_ref, (i % tiles_per_batch) == 0)

    def mixed(p):
        return (xm + dx * mu_ref[p]).astype(BF16)

    lora_w = _dot(jnp.tanh(_dot(mixed(3), w1_ref[...])).astype(BF16), w2_ref[...])
    lw_ref[...] = -math.exp(-RWKV_DECAY_OFFSET) * jax.nn.sigmoid(w0_ref[...] + lora_w)
    lora_a = _dot(_dot(mixed(4), a1_ref[...]).astype(BF16), a2_ref[...])
    a_ref[...] = jax.nn.sigmoid(a0_ref[...] + lora_a)
    g_ref[...] = _dot(jax.nn.sigmoid(_dot(mixed(5), g1_ref[...])).astype(BF16), g2_ref[...])


def _rwkv_lora(h, mod, layer, batch, mu, w0, w1, w2, a0, a1, a2, g1, g2, tm=256):
    t, d = h.shape
    tm = min(tm, t // batch)
    tpb = (t // batch) // tm
    full = lambda a: pl.BlockSpec(a.shape, lambda i: (0,) * a.ndim)
    mu3 = mu.reshape(-1, 1, d)
    w0, a0 = w0.reshape(1, d), a0.reshape(1, d)
    consts = [mu3, w0, w1, w2, a0, a1, a2, g1, g2]
    return pl.pallas_call(
        functools.partial(_rwkv_lora_kernel, tiles_per_batch=tpb),
        grid=(t // tm,),
        in_specs=[
            pl.BlockSpec((tm, d), lambda i: (i, 0)),
            _prev_rows_spec(tm, d),
            _mod_spec(d, layer, 1, batch, tpb),
            _mod_spec(d, layer, 0, batch, tpb),
        ] + [full(a) for a in consts],
        out_specs=[pl.BlockSpec((tm, d), lambda i: (i, 0))] * 3,
        out_shape=[jax.ShapeDtypeStruct((t, d), F32)] * 3,
        compiler_params=_params(1),
        name="rwkv_lora",
    )(h, h, mod, mod, *consts)


def _split_dot_left(ones, x, terms):
    acc = None
    for _ in range(terms):
        piece = x.astype(BF16)
        part = _dot(ones, piece)
        acc = part if acc is None else acc + part
        x = x - piece.astype(F32)
    return acc


def _rwkv_core_kernel(r_ref, k_ref, v_ref, lw_ref, a_ref, g_ref, kk_ref, ka_ref, rk_ref,
                      lng_ref, lnb_ref, o_ref, state_ref):
    c_len, w = RWKV_CHUNK, RWKV_LANES
    heads = w // RWKV_HEAD_DIM
    n = heads * c_len

    @pl.when(pl.program_id(2) == 0)
    def _():
        state_ref[...] = jnp.zeros_like(state_ref)

    rb = lax.broadcasted_iota(jnp.int32, (n, w), 0)
    lb = lax.broadcasted_iota(jnp.int32, (n, w), 1)
    head_match = (rb // c_len) == (lb // RWKV_HEAD_DIM)
    tr = lax.broadcasted_iota(jnp.int32, (c_len, n), 0)
    ts = lax.broadcasted_iota(jnp.int32, (c_len, n), 1) % c_len
    strict = tr > ts
    incl = tr >= ts
    eye = jnp.where(tr == ts, 1.0, 0.0)
    cr = lax.broadcasted_iota(jnp.int32, (c_len, c_len), 0)
    cc = lax.broadcasted_iota(jnp.int32, (c_len, c_len), 1)
    tril_ones = jnp.where(cr >= cc, 1.0, 0.0).astype(BF16)

    def expand(x):
        xb = x.astype(BF16)
        return jnp.where(head_match, jnp.concatenate([xb] * heads, axis=0), jnp.zeros((), BF16))

    lane = lax.broadcasted_iota(jnp.int32, (c_len, V7X_LANES), 1)
    first_head = lane < RWKV_HEAD_DIM

    def seg_sum(x):
        out = []
        for col in range(w // V7X_LANES):
            xc = x[:, col * V7X_LANES:(col + 1) * V7X_LANES]
            s0 = jnp.sum(jnp.where(first_head, xc, 0.0), axis=-1, keepdims=True)
            s1 = jnp.sum(jnp.where(first_head, 0.0, xc), axis=-1, keepdims=True)
            out.append(jnp.where(first_head, s0, s1))
        return jnp.concatenate(out, axis=1)

    def chunk(rows, grp):
        lanes = slice(grp * w, (grp + 1) * w)
        r, k, v = r_ref[rows, lanes], k_ref[rows, lanes], v_ref[rows, lanes]
        lw, a, g = lw_ref[rows, lanes], a_ref[rows, lanes], g_ref[rows, lanes]
        k_k, k_a, r_k = kk_ref[:, lanes], ka_ref[:, lanes], rk_ref[:, lanes]
        ln_g, ln_b = lng_ref[:, lanes], lnb_ref[:, lanes]

        kk = k * k_k
        kk_sq = seg_sum(kk * kk)
        kp = k * (1.0 + (a - 1.0) * k_a)
        rk_sum = seg_sum(r * kp * r_k)
        cum = _split_dot_left(tril_ones, lw, 3)
        yield
        kk = kk / jnp.maximum(jnp.sqrt(kk_sq), 1e-12)
        bonus = rk_sum * v
        cum_end = cum[c_len - 1:c_len, :]
        e_pos, e_neg = jnp.exp(cum), jnp.exp(-cum)
        e_tail = jnp.exp(cum_end - cum)
        ba = kk * a
        r_t = (r * e_pos).astype(BF16)
        a_t = (-kk * jnp.exp(cum - lw)).astype(BF16)
        b_t, k_t = ba * e_neg, kp * e_neg
        bk_h = jnp.concatenate([ba * e_tail, kp * e_tail], axis=0).astype(BF16)
        vb = v.astype(BF16)

        gram = _dot_nt(jnp.concatenate([a_t, r_t], axis=0),
                       jnp.concatenate([expand(b_t), expand(k_t)], axis=0))
        yield
        ab = jnp.where(strict, gram[:c_len, :n], 0.0)
        ak = jnp.where(strict, gram[:c_len, n:], 0.0).astype(BF16)
        rbm = jnp.where(incl, gram[c_len:, :n], 0.0).astype(BF16)
        rkm = jnp.where(incl, gram[c_len:, n:], 0.0).astype(BF16)

        tinv = eye + ab
        p = _dot(ab.astype(BF16), expand(ab))
        akv_rkv = _dot(jnp.concatenate([ak, rkm], axis=0), expand(v))
        yield
        for _ in range(c_len.bit_length() - 3):
            both = _dot(jnp.concatenate([p, tinv], axis=0).astype(BF16), expand(p))
            yield
            p, tinv = both[:c_len], tinv + both[c_len:]
        last = _dot(tinv.astype(BF16), expand(p))
        state = state_ref[grp]
        ag_rg = _dot_nt(jnp.concatenate([a_t, r_t], axis=0), state.astype(BF16))
        yield
        tb = (tinv + last).astype(BF16)
        u = _dot(tb, expand(akv_rkv[:c_len] + ag_rg[:c_len]))
        yield
        o = _dot(rbm, expand(u))
        upd = _dot_tn(jnp.concatenate([u.astype(BF16), vb], axis=0), bk_h)
        yield
        o = ag_rg[c_len:] + o + akv_rkv[c_len:]
        state_ref[grp] = state * jnp.exp(cum_end) + jnp.where(
            (rb // RWKV_HEAD_DIM) == (lb // RWKV_HEAD_DIM), upd, 0.0)
        mean = seg_sum(o) * (1.0 / RWKV_HEAD_DIM)
        yield
        cen = o - mean
        var = seg_sum(cen * cen) * (1.0 / RWKV_HEAD_DIM)
        yield
        y = (cen * lax.rsqrt(var + RWKV_GN_EPS)) * ln_g + ln_b
        o_ref[rows, lanes] = ((y + bonus) * g).astype(BF16)

    def body(c, carry):
        rows = pl.ds(pl.multiple_of(c * c_len, c_len), c_len)
        pending = [chunk(rows, grp) for grp in range(state_ref.shape[0])]
        while pending:
            pending = [gen for gen in pending if next(gen, True) is None]
        return carry

    lax.fori_loop(0, r_ref.shape[0] // c_len, body, 0)


def _rwkv_core(rkv, lw, a, g, k_k, k_a, r_k, ln_g, ln_b, batch, rows=256, groups=8):
    t, d = lw.shape
    s = t // batch
    rows = min(rows, s)
    w = RWKV_LANES * groups
    nq = d // w
    spb = s // rows
    seq = lambda off: pl.BlockSpec((rows, w), lambda b, q, i: (b * spb + i, off + q))
    par = pl.BlockSpec((1, w), lambda b, q, i: (0, q))
    vec = lambda x: x.reshape(1, d)
    return pl.pallas_call(
        _rwkv_core_kernel,
        grid=(batch, nq, spb),
        in_specs=[seq(0), seq(nq), seq(2 * nq), seq(0), seq(0), seq(0), par, par, par, par, par],
        out_specs=seq(0),
        out_shape=jax.ShapeDtypeStruct((t, d), BF16),
        scratch_shapes=[pltpu.VMEM((groups, RWKV_LANES, RWKV_LANES), F32)],
        compiler_params=_params(3),
        name="rwkv_core",
    )(rkv, rkv, rkv, lw, a, g, vec(k_k), vec(k_a), vec(r_k), vec(ln_g), vec(ln_b))


def _pad_cols(x, n):
    return jnp.pad(x, [(0, 0)] * (x.ndim - 1) + [(0, n - x.shape[-1])])


def _pad_rows(x, n):
    return jnp.pad(x, [(0, n - x.shape[0])] + [(0, 0)] * (x.ndim - 1))


def _round_up(x, m):
    return -(-x // m) * m


def _cast_pad_halves_kernel(x_ref, o_ref, *, half, padded):
    zeros = jnp.zeros((o_ref.shape[0], padded - half), o_ref.dtype)
    for k in range(2):
        o_ref[:, k * padded:k * padded + half] = x_ref[:, k * half:(k + 1) * half].astype(o_ref.dtype)
        o_ref[:, k * padded + half:(k + 1) * padded] = zeros


def _cast_pad_halves(x, half, padded, rows=128):
    n, r, _ = x.shape
    return pl.pallas_call(
        functools.partial(_cast_pad_halves_kernel, half=half, padded=padded),
        grid=(n, r // rows),
        in_specs=[pl.BlockSpec((None, rows, 2 * half), lambda l, i: (l, i, 0))],
        out_specs=pl.BlockSpec((None, rows, 2 * padded), lambda l, i: (l, i, 0)),
        out_shape=jax.ShapeDtypeStruct((n, r, 2 * padded), BF16),
        compiler_params=_params(2),
        name="cast_pad_halves",
    )(x)


def _cast_pad_rows_kernel(x_ref, o_ref):
    rows = x_ref.shape[0]
    o_ref[0:rows, :] = x_ref[...].astype(o_ref.dtype)
    o_ref[rows:, :] = jnp.zeros((o_ref.shape[0] - rows, o_ref.shape[1]), o_ref.dtype)


def _cast_pad_rows(x, padded, cols=256):
    n, r, c = x.shape
    return pl.pallas_call(
        _cast_pad_rows_kernel,
        grid=(n, c // cols),
        in_specs=[pl.BlockSpec((None, r, cols), lambda l, j: (l, 0, j))],
        out_specs=pl.BlockSpec((None, padded, cols), lambda l, j: (l, 0, j)),
        out_shape=jax.ShapeDtypeStruct((n, padded, c), BF16),
        compiler_params=_params(2),
        name="cast_pad_rows",
    )(x)


def _pad_val_gate(x, d_ff, fp):
    return jnp.concatenate([_pad_cols(x[..., :d_ff], fp), _pad_cols(x[..., d_ff:], fp)], axis=-1)


def kernel(x, c, positions, ada_w, ada_b, ffn_w_up, ffn_conv_w, ffn_conv_b, ffn_w_down, sg_w_in, sg_ln_g, sg_ln_b, sg_w_s, sg_b_s, sg_w_out, ret_w_in, ret_gn_g, ret_gn_b, ret_w_out, rwkv_mu, rwkv_w_rkv, rwkv_w0, rwkv_w1, rwkv_w2, rwkv_a0, rwkv_a1, rwkv_a2, rwkv_g1, rwkv_g2, rwkv_k_k, rwkv_k_a, rwkv_r_k, rwkv_ln_g, rwkv_ln_b, rwkv_w_out, final_norm_g):
    batch, seq, d = x.shape
    depth = ada_w.shape[0]
    t = batch * seq
    d_ff = ffn_w_down.shape[1]
    fp = _round_up(d_ff, FFN_SUBCHUNKS * FFN_SUB)

    sg_w_out_b, ret_w_out_b = sg_w_out.astype(BF16), ret_w_out.astype(BF16)
    rwkv_w_rkv_b, rwkv_w_out_b = rwkv_w_rkv.astype(BF16), rwkv_w_out.astype(BF16)
    ffn_w_up_b = _cast_pad_halves(ffn_w_up, d_ff, fp)
    ffn_w_down_b = _cast_pad_rows(ffn_w_down, fp)
    ffn_conv_w_p = _pad_val_gate(ffn_conv_w, d_ff, fp)
    ffn_conv_b_p = _pad_val_gate(ffn_conv_b, d_ff, fp).reshape(depth, 1, 2 * fp)

    mod = _modulation(c, ada_w, ada_b)
    h = x.reshape(t, d)
    cos = sin = None

    for layer in range(depth):
        kind, j = layer % N_MIXERS, layer // N_MIXERS
        if kind == 0:
            z = _norm_matmul(h, mod, layer, 1, 0, sg_w_in, j, batch, n=sg_w_in.shape[-1], epilogue="gelu")
            h = _sg_core(z, h, mod, layer, batch, sg_ln_g[j], sg_ln_b[j], sg_w_s[j], sg_b_s[j],
                         sg_w_out_b, j)
        elif kind == 1:
            if cos is None:
                cos, sin = _rope_tables(positions, d // RET_HEADS // 2)
            qk = _norm_matmul(h, mod, layer, 1, 0, ret_w_in, j, batch, n=2 * d, epilogue="rope",
                              cos=cos, sin=sin)
            v = _norm_matmul(h, mod, layer, 1, 0, ret_w_in, j, batch, n=2 * d, col_offset=2 * d,
                             out_dtype=BF16)
            g = _norm_matmul(h, mod, layer, 1, 0, ret_w_in, j, batch, n=2 * d, col_offset=4 * d)
            o = _ret_core(qk, v, g, ret_gn_g[j], ret_gn_b[j], batch)
            h = _proj_residual(o, ret_w_out_b, j, h, mod, layer, 2, batch)
        else:
            rkv = _rwkv_rkv(h, mod, layer, batch, rwkv_mu[j], rwkv_w_rkv_b, j)
            lora = _round_up(rwkv_w1.shape[-1], V7X_LANES)
            lora_a = _round_up(rwkv_a1.shape[-1], V7X_LANES)
            lw, a, g = _rwkv_lora(
                h, mod, layer, batch, rwkv_mu[j], rwkv_w0[j],
                _pad_cols(rwkv_w1[j], lora).astype(BF16), _pad_rows(rwkv_w2[j], lora).astype(BF16),
                rwkv_a0[j],
                _pad_cols(rwkv_a1[j], lora_a).astype(BF16), _pad_rows(rwkv_a2[j], lora_a).astype(BF16),
                rwkv_g1[j].astype(BF16), rwkv_g2[j].astype(BF16))
            o = _rwkv_core(rkv, lw, a, g, rwkv_k_k[j], rwkv_k_a[j], rwkv_r_k[j],
                           rwkv_ln_g[j], rwkv_ln_b[j], batch)
            h = _proj_residual(o, rwkv_w_out_b, j, h, mod, layer, 2, batch)

        h = _conv_ffn(h, mod, layer, batch, ffn_w_up_b, ffn_conv_w_p, ffn_conv_b_p, ffn_w_down_b,
                      final_gain=final_norm_g if layer == depth - 1 else None)

    return h.reshape(batch, seq, d)
```

```python
import functools
import math

import jax
import jax.numpy as jnp
from jax import lax
from jax.experimental import pallas as pl
from jax.experimental.pallas import tpu as pltpu

F32 = jnp.float32
BF16 = jnp.bfloat16

NORM_EPS = 1e-6
LN_EPS = 1e-5
N_MOD = 6
N_MIXERS = 3

SG_CHUNK = 128
SG_GROUPS = 16

RET_HEADS = 8
RET_CHUNK = 128
ROPE_BASE = 10000.0

RWKV_HEAD_DIM = 64
RWKV_GN_EPS = RWKV_HEAD_DIM * 1e-5
RWKV_DECAY_OFFSET = 0.5
RWKV_CHUNK = 64
RWKV_LANES = 256

CONV_WIDTH = 3
FFN_SUB = 256
FFN_SUBCHUNKS = 2
MATMUL_SUBTILE = 512
WEIGHT_BLOCK_ELEMS = 2048 * 1024

V7X_LANES = 128
V7X_SUBLANES = 8
V7X_VMEM_LIMIT = 56 * 1024 * 1024


def _params(n_axes, vmem=V7X_VMEM_LIMIT):
    return pltpu.CompilerParams(dimension_semantics=("arbitrary",) * n_axes,
                                vmem_limit_bytes=vmem)


def _dot(a, b):
    return jnp.dot(a, b, preferred_element_type=F32)


def _dot_nt(a, b):
    return lax.dot_general(a, b, (((1,), (1,)), ((), ())), preferred_element_type=F32)


def _dot_tn(a, b):
    return lax.dot_general(a, b, (((0,), (0,)), ((), ())), preferred_element_type=F32)


def _rms_mod(h, sc, sh):
    ms = jnp.mean(h * h, axis=-1, keepdims=True)
    return (h * lax.rsqrt(ms + NORM_EPS)) * (1.0 + sc) + sh


def _mod_kernel(c_ref, w_ref, b_ref, o_ref):
    cond = jax.nn.silu(c_ref[...])
    o_ref[...] = _dot(cond.astype(BF16), w_ref[...].astype(BF16)) + b_ref[...]


def _modulation(c, ada_w, ada_b, tn=2048):
    depth, d, n = ada_w.shape
    b = c.shape[0]
    rows = -(-b // V7X_SUBLANES) * V7X_SUBLANES
    c_pad = jnp.pad(c, ((0, rows - b), (0, 0)))
    out = pl.pallas_call(
        _mod_kernel,
        grid=(depth, n // tn),
        in_specs=[
            pl.BlockSpec((rows, d), lambda l, j: (0, 0)),
            pl.BlockSpec((None, d, tn), lambda l, j: (l, 0, j)),
            pl.BlockSpec((None, 1, tn), lambda l, j: (l, 0, j)),
        ],
        out_specs=pl.BlockSpec((None, rows, tn), lambda l, j: (l, 0, j)),
        out_shape=jax.ShapeDtypeStruct((depth, rows, n), F32),
        compiler_params=_params(2),
        name="adaln_mod",
    )(c_pad, ada_w, ada_b.reshape(depth, 1, n))
    return out[:, :b].reshape(depth * b * N_MOD, 1, d)


def _mod_spec(d, layer, k, batch, tiles_per_batch):
    base = layer * batch * N_MOD + k
    return pl.BlockSpec((None, 1, d),
                        lambda i, *_: (base + (i // tiles_per_batch) * N_MOD, 0, 0))


def _norm_matmul_kernel(h_ref, sc_ref, sh_ref, w_ref, *rest, epilogue):
    if epilogue == "rope":
        cos_ref, sin_ref, o_ref, xm_ref = rest
    else:
        o_ref, xm_ref = rest

    @pl.when(pl.program_id(1) == 0)
    def _():
        xm_ref[...] = _rms_mod(h_ref[...], sc_ref[...], sh_ref[...]).astype(xm_ref.dtype)

    xm = xm_ref[...]
    tn = w_ref.shape[1]
    sub = min(tn, MATMUL_SUBTILE)
    for c in range(tn // sub):
        cols = slice(c * sub, (c + 1) * sub)
        y = _dot(xm, w_ref[:, cols])
        if epilogue == "gelu":
            y = jax.nn.gelu(y)
        elif epilogue == "rope":
            cos, sin = cos_ref[...], sin_ref[...]
            half = cos.shape[-1]
            parts = []
            for hd in range(sub // (2 * half)):
                x1 = y[:, 2 * hd * half:(2 * hd + 1) * half]
                x2 = y[:, (2 * hd + 1) * half:(2 * hd + 2) * half]
                parts += [x1 * cos - x2 * sin, x2 * cos + x1 * sin]
            y = jnp.concatenate(parts, axis=-1)
        o_ref[:, cols] = y.astype(o_ref.dtype)


def _norm_matmul(h, mod, layer, k_scale, k_shift, w, w_idx, batch, *, n, col_offset=0, epilogue="none",
                 out_dtype=F32, cos=None, sin=None, tm=1024, tn=1024):
    t, d = h.shape
    tm = min(tm, t // batch)
    tpb = (t // batch) // tm
    off = col_offset // tn
    in_specs = [
        pl.BlockSpec((tm, d), lambda i, j: (i, 0)),
        _mod_spec(d, layer, k_scale, batch, tpb),
        _mod_spec(d, layer, k_shift, batch, tpb),
        pl.BlockSpec((None, d, tn), lambda i, j: (w_idx, 0, off + j)),
    ]
    args = [h, mod, mod, w]
    if epilogue == "rope":
        half = cos.shape[-1]
        in_specs += [pl.BlockSpec((tm, half), lambda i, j: (i, 0))] * 2
        args += [cos, sin]
    return pl.pallas_call(
        functools.partial(_norm_matmul_kernel, epilogue=epilogue),
        grid=(t // tm, n // tn),
        in_specs=in_specs,
        out_specs=pl.BlockSpec((tm, tn), lambda i, j: (i, j)),
        out_shape=jax.ShapeDtypeStruct((t, n), out_dtype),
        scratch_shapes=[pltpu.VMEM((tm, d), w.dtype)],
        compiler_params=_params(2),
        name="norm_matmul_" + epilogue,
    )(*args)


def _proj_residual_kernel(a_ref, w_ref, h_ref, g_ref, o_ref):
    a = a_ref[...]
    tn = w_ref.shape[1]
    sub = min(tn, MATMUL_SUBTILE)
    for c in range(tn // sub):
        cols = slice(c * sub, (c + 1) * sub)
        o_ref[:, cols] = h_ref[:, cols] + g_ref[:, cols] * _dot(a, w_ref[:, cols])


def _proj_residual(a, w, w_idx, h, mod, layer, k_gate, batch):
    t, kdim = a.shape
    d = w.shape[-1]
    tm = min(2 * WEIGHT_BLOCK_ELEMS // kdim, t // batch)
    tn = min(WEIGHT_BLOCK_ELEMS // kdim, WEIGHT_BLOCK_ELEMS // (2 * tm))
    tpb = (t // batch) // tm
    base = layer * batch * N_MOD + k_gate
    return pl.pallas_call(
        _proj_residual_kernel,
        grid=(t // tm, d // tn),
        in_specs=[
            pl.BlockSpec((tm, kdim), lambda i, j: (i, 0)),
            pl.BlockSpec((None, kdim, tn), lambda i, j: (w_idx, 0, j)),
            pl.BlockSpec((tm, tn), lambda i, j: (i, j)),
            pl.BlockSpec((None, 1, tn), lambda i, j: (base + (i // tpb) * N_MOD, 0, j)),
        ],
        out_specs=pl.BlockSpec((tm, tn), lambda i, j: (i, j)),
        out_shape=jax.ShapeDtypeStruct((t, d), F32),
        compiler_params=_params(2),
        name="proj_residual",
    )(a, w, h, mod)


def _sg_core_kernel(u_ref, v_ref, h_ref, g_ref, lng_ref, lnb_ref, ws_ref, bs_ref, wo_ref,
                    o_ref, gated_ref):
    tm, width = v_ref.shape
    gdim = width // SG_GROUPS
    v = v_ref[...]
    mu = jnp.mean(v, axis=-1, keepdims=True)
    var = jnp.mean(jnp.square(v - mu), axis=-1, keepdims=True)
    vn = (((v - mu) * lax.rsqrt(var + LN_EPS)) * lng_ref[...] + lnb_ref[...]).astype(BF16)
    row = lax.broadcasted_iota(jnp.int32, (SG_CHUNK, SG_CHUNK), 0)
    col = lax.broadcasted_iota(jnp.int32, (SG_CHUNK, SG_CHUNK), 1)
    causal = row >= col
    for g in range(SG_GROUPS):
        w_causal = jnp.where(causal, ws_ref[g], 0.0).astype(BF16)
        bias = bs_ref[:, g:g + 1]
        cols = slice(g * gdim, (g + 1) * gdim)
        for c in range(tm // SG_CHUNK):
            rows = slice(c * SG_CHUNK, (c + 1) * SG_CHUNK)
            sv = _dot(w_causal, vn[rows, cols]) + bias
            gated_ref[rows, cols] = (u_ref[rows, cols] * sv).astype(BF16)
    o_ref[...] = h_ref[...] + g_ref[...] * _dot(gated_ref[...], wo_ref[...])


def _sg_core(z, h, mod, layer, batch, ln_g, ln_b, w_s, b_s, w_out, w_idx, tm=512):
    t, d = h.shape
    width = z.shape[1] // 2
    tm = min(tm, t // batch)
    tpb = (t // batch) // tm
    return pl.pallas_call(
        _sg_core_kernel,
        grid=(t // tm,),
        in_specs=[
            pl.BlockSpec((tm, width), lambda i: (i, 0)),
            pl.BlockSpec((tm, width), lambda i: (i, 1)),
            pl.BlockSpec((tm, d), lambda i: (i, 0)),
            _mod_spec(d, layer, 2, batch, tpb),
            pl.BlockSpec((1, width), lambda i: (0, 0)),
            pl.BlockSpec((1, width), lambda i: (0, 0)),
            pl.BlockSpec((SG_GROUPS, SG_CHUNK, SG_CHUNK), lambda i: (0, 0, 0)),
            pl.BlockSpec((SG_CHUNK, SG_GROUPS), lambda i: (0, 0)),
            pl.BlockSpec((None, width, d), lambda i: (w_idx, 0, 0)),
        ],
        out_specs=pl.BlockSpec((tm, d), lambda i: (i, 0)),
        out_shape=jax.ShapeDtypeStruct((t, d), F32),
        scratch_shapes=[pltpu.VMEM((tm, width), BF16)],
        compiler_params=_params(1),
        name="sg_core",
    )(z, z, h, mod, ln_g.reshape(1, width), ln_b.reshape(1, width), w_s, b_s.T, w_out)


def _ffn_kernel(h_ref, sc_ref, sh_ref, g_ref, wv_ref, wg_ref, cwv_ref, cwg_ref, cbv_ref, cbg_ref,
                wd_ref, *rest, tiles_per_batch, n_tiles, final_norm):
    if final_norm:
        fg_ref, o_ref, xm_ref, halo_ref = rest[:4]
    else:
        o_ref, xm_ref, halo_ref = rest[:3]
    hbufs = rest[-FFN_SUBCHUNKS:]
    i, j = pl.program_id(0), pl.program_id(1)
    tm = h_ref.shape[0]
    pad = V7X_SUBLANES
    sub = FFN_SUB

    @pl.when(j == 0)
    def _():
        xm_ref[...] = _rms_mod(h_ref[...], sc_ref[...], sh_ref[...]).astype(BF16)
        o_ref[...] = jnp.zeros_like(o_ref)

    seq_start = (i % tiles_per_batch) == 0
    xm = xm_ref[...]
    for k, hb in enumerate(hbufs):
        cols = slice(k * sub, (k + 1) * sub)
        hb[0:pad, :] = jnp.where(seq_start, 0.0, halo_ref[j, k])
        hb[pad:, 0:sub] = _dot(xm, wv_ref[:, cols])
        hb[pad:, sub:] = _dot(xm, wg_ref[:, cols])
        halo_ref[j, k] = hb[tm:, :]

    def conv(hb, lanes, cw, cb):
        acc = hb[pad - 2:pad - 2 + tm, lanes] * cw[0:1, :]
        acc = acc + hb[pad - 1:pad - 1 + tm, lanes] * cw[1:2, :]
        acc = acc + hb[pad:pad + tm, lanes] * cw[2:3, :]
        return acc + cb

    for k, hb in enumerate(hbufs):
        cols = slice(k * sub, (k + 1) * sub)
        val = conv(hb, slice(0, sub), cwv_ref[:, cols], cbv_ref[:, cols])
        gate = conv(hb, slice(sub, 2 * sub), cwg_ref[:, cols], cbg_ref[:, cols])
        act = (jax.nn.silu(gate) * val).astype(BF16)
        o_ref[...] += _dot(act, wd_ref[cols, :])

    @pl.when(j == n_tiles - 1)
    def _():
        out = h_ref[...] + g_ref[...] * o_ref[...]
        if final_norm:
            ms = jnp.mean(out * out, axis=-1, keepdims=True)
            out = (out * lax.rsqrt(ms + NORM_EPS)) * fg_ref[...]
        o_ref[...] = out


def _conv_ffn(h, mod, layer, batch, w_up, conv_w, conv_b, w_down, final_gain=None, tm=512):
    t, d = h.shape
    fp = w_down.shape[1]
    tf = FFN_SUBCHUNKS * FFN_SUB
    nf = fp // tf
    tm = min(tm, t // batch)
    tpb = (t // batch) // tm
    final_norm = final_gain is not None
    in_specs = [
        pl.BlockSpec((tm, d), lambda i, j: (i, 0)),
        _mod_spec(d, layer, 4, batch, tpb),
        _mod_spec(d, layer, 3, batch, tpb),
        _mod_spec(d, layer, 5, batch, tpb),
        pl.BlockSpec((None, d, tf), lambda i, j: (layer, 0, j)),
        pl.BlockSpec((None, d, tf), lambda i, j: (layer, 0, nf + j)),
        pl.BlockSpec((None, CONV_WIDTH, tf), lambda i, j: (layer, 0, j)),
        pl.BlockSpec((None, CONV_WIDTH, tf), lambda i, j: (layer, 0, nf + j)),
        pl.BlockSpec((None, 1, tf), lambda i, j: (layer, 0, j)),
        pl.BlockSpec((None, 1, tf), lambda i, j: (layer, 0, nf + j)),
        pl.BlockSpec((None, tf, d), lambda i, j: (layer, j, 0)),
    ]
    args = [h, mod, mod, mod, w_up, w_up, conv_w, conv_w, conv_b, conv_b, w_down]
    if final_norm:
        in_specs.append(pl.BlockSpec((1, d), lambda i, j: (0, 0)))
        args.append(final_gain.reshape(1, d))
    return pl.pallas_call(
        functools.partial(_ffn_kernel, tiles_per_batch=tpb, n_tiles=nf, final_norm=final_norm),
        grid=(t // tm, nf),
        in_specs=in_specs,
        out_specs=pl.BlockSpec((tm, d), lambda i, j: (i, 0)),
        out_shape=jax.ShapeDtypeStruct((t, d), F32),
        scratch_shapes=[
            pltpu.VMEM((tm, d), BF16),
            pltpu.VMEM((nf, FFN_SUBCHUNKS, V7X_SUBLANES, 2 * FFN_SUB), F32),
        ] + [pltpu.VMEM((tm + V7X_SUBLANES, 2 * FFN_SUB), F32)] * FFN_SUBCHUNKS,
        compiler_params=_params(2),
        name="conv_ffn",
    )(*args)


def _rope_table_kernel(pos_ref, freq_ref, cos_ref, sin_ref):
    ang = pos_ref[...].astype(F32) * freq_ref[...]
    cos_ref[...] = jnp.cos(ang)
    sin_ref[...] = jnp.sin(ang)


def _rope_tables(positions, half, tm=1024):
    t = positions.size
    inv_freq = ROPE_BASE ** (-jnp.arange(half, dtype=F32) / half)
    tm = min(tm, t)
    return pl.pallas_call(
        _rope_table_kernel,
        grid=(t // tm,),
        in_specs=[pl.BlockSpec((tm, 1), lambda i: (i, 0)),
                  pl.BlockSpec((1, half), lambda i: (0, 0))],
        out_specs=[pl.BlockSpec((tm, half), lambda i: (i, 0))] * 2,
        out_shape=[jax.ShapeDtypeStruct((t, half), F32)] * 2,
        compiler_params=_params(1),
        name="rope_tables",
    )(positions.reshape(t, 1), inv_freq.reshape(1, half))


def _ret_core_kernel(q_ref, k_ref, v_ref, g_ref, di_ref, qd_ref, kd_ref, cd_ref, gg_ref, gb_ref,
                     o_ref, state_ref, *, k_scale):
    n_heads, dk, dv = state_ref.shape

    @pl.when(pl.program_id(2) == 0)
    def _():
        state_ref[...] = jnp.zeros_like(state_ref)

    def chunk(rows, hd):
        kcols = slice(hd * dk, (hd + 1) * dk)
        vcols = slice(hd * dv, (hd + 1) * dv)
        q = q_ref[rows, kcols]
        k = k_ref[rows, kcols] * k_scale
        v = v_ref[rows, vcols]
        state = state_ref[hd]
        inner = _dot_nt(q.astype(BF16), k.astype(BF16))
        cross = _dot((q * qd_ref[hd]).astype(BF16), state.astype(BF16))
        update = _dot_tn((k * kd_ref[hd]).astype(BF16), v)
        yield
        state_ref[hd] = state * cd_ref[hd] + update
        out = _dot((inner * di_ref[hd]).astype(BF16), v)
        yield
        out = out + cross
        mu = jnp.mean(out, axis=-1, keepdims=True)
        var = jnp.mean(jnp.square(out - mu), axis=-1, keepdims=True)
        y = ((out - mu) * lax.rsqrt(var + NORM_EPS)) * gg_ref[:, vcols] + gb_ref[:, vcols]
        o_ref[rows, vcols] = (jax.nn.silu(g_ref[rows, vcols]) * y).astype(BF16)

    def body(c, carry):
        rows = pl.ds(pl.multiple_of(c * RET_CHUNK, RET_CHUNK), RET_CHUNK)
        pending = [chunk(rows, hd) for hd in range(n_heads)]
        while pending:
            pending = [gen for gen in pending if next(gen, True) is None]
        return carry

    lax.fori_loop(0, q_ref.shape[0] // RET_CHUNK, body, 0)


def _ret_core(qk, v, g, gn_g, gn_b, batch, rows=256, group=8):
    t, two_d = qk.shape
    d = two_d // 2
    s = t // batch
    rows = min(rows, s)
    spb = s // rows
    n_groups = RET_HEADS // group
    dk = d // RET_HEADS
    dv = v.shape[1] // RET_HEADS
    log_gamma = jnp.log1p(-jnp.exp2(-5.0 - jnp.arange(RET_HEADS, dtype=F32)))
    idx = jnp.arange(RET_CHUNK, dtype=F32)
    rel = idx[:, None] - idx[None, :]
    decay_inner = jnp.where(rel >= 0, jnp.exp(log_gamma[:, None, None] * jnp.maximum(rel, 0.0)), 0.0)
    q_decay = jnp.exp(log_gamma[:, None] * (idx + 1.0))[..., None]
    k_decay = jnp.exp(log_gamma[:, None] * (RET_CHUNK - 1.0 - idx))[..., None]
    chunk_decay = jnp.exp(log_gamma * RET_CHUNK)[:, None, None]
    return pl.pallas_call(
        functools.partial(_ret_core_kernel, k_scale=dk ** -0.5),
        grid=(batch, n_groups, spb),
        in_specs=[
            pl.BlockSpec((rows, group * dk), lambda b, h, i: (b * spb + i, h)),
            pl.BlockSpec((rows, group * dk), lambda b, h, i: (b * spb + i, n_groups + h)),
            pl.BlockSpec((rows, group * dv), lambda b, h, i: (b * spb + i, h)),
            pl.BlockSpec((rows, group * dv), lambda b, h, i: (b * spb + i, h)),
            pl.BlockSpec((group, RET_CHUNK, RET_CHUNK), lambda b, h, i: (h, 0, 0)),
            pl.BlockSpec((group, RET_CHUNK, 1), lambda b, h, i: (h, 0, 0)),
            pl.BlockSpec((group, RET_CHUNK, 1), lambda b, h, i: (h, 0, 0)),
            pl.BlockSpec((group, 1, 1), lambda b, h, i: (h, 0, 0)),
            pl.BlockSpec((1, group * dv), lambda b, h, i: (0, h)),
            pl.BlockSpec((1, group * dv), lambda b, h, i: (0, h)),
        ],
        out_specs=pl.BlockSpec((rows, group * dv), lambda b, h, i: (b * spb + i, h)),
        out_shape=jax.ShapeDtypeStruct((t, v.shape[1]), BF16),
        scratch_shapes=[pltpu.VMEM((group, dk, dv), F32)],
        compiler_params=_params(3),
        name="ret_core",
    )(qk, qk, v, g, decay_inner, q_decay, k_decay, chunk_decay,
      gn_g.reshape(1, -1), gn_b.reshape(1, -1))


def _shifted_inputs(h_ref, hprev_ref, sc_ref, sh_ref, seq_start):
    sc, sh = sc_ref[...], sh_ref[...]
    xm = _rms_mod(h_ref[...], sc, sh)
    prev = _rms_mod(hprev_ref[...], sc, sh)
    last = prev[V7X_SUBLANES - 1:V7X_SUBLANES, :]
    last = jnp.where(seq_start, 0.0, last)
    row = lax.broadcasted_iota(jnp.int32, xm.shape, 0)
    shifted = jnp.where(row == 0, last, pltpu.roll(xm, 1, 0))
    return xm, shifted - xm


def _rwkv_rkv_kernel(h_ref, hprev_ref, sc_ref, sh_ref, mu_ref, w_ref, o_ref, xm_ref, dx_ref, xs_ref,
                     *, tiles_per_batch, tiles_per_proj):
    i, j = pl.program_id(0), pl.program_id(1)

    @pl.when(j == 0)
    def _():
        xm, dx = _shifted_inputs(h_ref, hprev_ref, sc_ref, sh_ref, (i % tiles_per_batch) == 0)
        xm_ref[...] = xm
        dx_ref[...] = dx

    @pl.when(j % tiles_per_proj == 0)
    def _():
        xs_ref[...] = (xm_ref[...] + dx_ref[...] * mu_ref[...]).astype(BF16)

    xs = xs_ref[...]
    tn = w_ref.shape[1]
    sub = min(tn, MATMUL_SUBTILE)
    for c in range(tn // sub):
        cols = slice(c * sub, (c + 1) * sub)
        o_ref[:, cols] = _dot(xs, w_ref[:, cols])


def _prev_rows_spec(tm, d):
    blocks = tm // V7X_SUBLANES
    return pl.BlockSpec((V7X_SUBLANES, d), lambda i, *_: (jnp.maximum(i * blocks - 1, 0), 0))


def _rwkv_rkv(h, mod, layer, batch, mu, w_rkv, w_idx, tm=512, tn=2048):
    t, d = h.shape
    tm = min(tm, t // batch)
    tpb = (t // batch) // tm
    tpp = d // tn
    return pl.pallas_call(
        functools.partial(_rwkv_rkv_kernel, tiles_per_batch=tpb, tiles_per_proj=tpp),
        grid=(t // tm, 3 * tpp),
        in_specs=[
            pl.BlockSpec((tm, d), lambda i, j: (i, 0)),
            _prev_rows_spec(tm, d),
            _mod_spec(d, layer, 1, batch, tpb),
            _mod_spec(d, layer, 0, batch, tpb),
            pl.BlockSpec((None, 1, d), lambda i, j: (j // tpp, 0, 0)),
            pl.BlockSpec((None, None, d, tn), lambda i, j: (w_idx, j // tpp, 0, j % tpp)),
        ],
        out_specs=pl.BlockSpec((tm, tn), lambda i, j: (i, j)),
        out_shape=jax.ShapeDtypeStruct((t, 3 * d), F32),
        scratch_shapes=[pltpu.VMEM((tm, d), F32), pltpu.VMEM((tm, d), F32), pltpu.VMEM((tm, d), BF16)],
        compiler_params=_params(2),
        name="rwkv_rkv",
    )(h, h, mod, mod, mu.reshape(-1, 1, d), w_rkv)


def _rwkv_lora_kernel(h_ref, hprev_ref, sc_ref, sh_ref, mu_ref, w0_ref, w1_ref, w2_ref,
                      a0_ref, a1_ref, a2_ref, g1_ref, g2_ref, lw_ref, a_ref, g_ref, *, tiles_per_batch):
    i = pl.program_id(0)
    xm, dx = _shifted_inputs(h_ref, hprev_ref, sc_ref, sh_ref, (i % tiles_per_batch) == 0)

    def mixed(p):
        return (xm + dx * mu_ref[p]).astype(BF16)

    lora_w = _dot(jnp.tanh(_dot(mixed(3), w1_ref[...])).astype(BF16), w2_ref[...])
    lw_ref[...] = -math.exp(-RWKV_DECAY_OFFSET) * jax.nn.sigmoid(w0_ref[...] + lora_w)
    lora_a = _dot(_dot(mixed(4), a1_ref[...]).astype(BF16), a2_ref[...])
    a_ref[...] = jax.nn.sigmoid(a0_ref[...] + lora_a)
    g_ref[...] = _dot(jax.nn.sigmoid(_dot(mixed(5), g1_ref[...])).astype(BF16), g2_ref[...])


def _rwkv_lora(h, mod, layer, batch, mu, w0, w1, w2, a0, a1, a2, g1, g2, tm=256):
    t, d = h.shape
    tm = min(tm, t // batch)
    tpb = (t // batch) // tm
    full = lambda a: pl.BlockSpec(a.shape, lambda i: (0,) * a.ndim)
    mu3 = mu.reshape(-1, 1, d)
    w0, a0 = w0.reshape(1, d), a0.reshape(1, d)
    consts = [mu3, w0, w1, w2, a0, a1, a2, g1, g2]
    return pl.pallas_call(
        functools.partial(_rwkv_lora_kernel, tiles_per_batch=tpb),
        grid=(t // tm,),
        in_specs=[
            pl.BlockSpec((tm, d), lambda i: (i, 0)),
            _prev_rows_spec(tm, d),
            _mod_spec(d, layer, 1, batch, tpb),
            _mod_spec(d, layer, 0, batch, tpb),
        ] + [full(a) for a in consts],
        out_specs=[pl.BlockSpec((tm, d), lambda i: (i, 0))] * 3,
        out_shape=[jax.ShapeDtypeStruct((t, d), F32)] * 3,
        compiler_params=_params(1),
        name="rwkv_lora",
    )(h, h, mod, mod, *consts)


def _split_dot_left(ones, x, terms):
    acc = None
    for _ in range(terms):
        piece = x.astype(BF16)
        part = _dot(ones, piece)
        acc = part if acc is None else acc + part
        x = x - piece.astype(F32)
    return acc


def _rwkv_core_kernel(r_ref, k_ref, v_ref, lw_ref, a_ref, g_ref, kk_ref, ka_ref, rk_ref,
                      lng_ref, lnb_ref, o_ref, state_ref):
    c_len, w = RWKV_CHUNK, RWKV_LANES
    heads = w // RWKV_HEAD_DIM
    n = heads * c_len

    @pl.when(pl.program_id(2) == 0)
    def _():
        state_ref[...] = jnp.zeros_like(state_ref)

    rb = lax.broadcasted_iota(jnp.int32, (n, w), 0)
    lb = lax.broadcasted_iota(jnp.int32, (n, w), 1)
    head_match = (rb // c_len) == (lb // RWKV_HEAD_DIM)
    tr = lax.broadcasted_iota(jnp.int32, (c_len, n), 0)
    ts = lax.broadcasted_iota(jnp.int32, (c_len, n), 1) % c_len
    strict = tr > ts
    incl = tr >= ts
    eye = jnp.where(tr == ts, 1.0, 0.0)
    cr = lax.broadcasted_iota(jnp.int32, (c_len, c_len), 0)
    cc = lax.broadcasted_iota(jnp.int32, (c_len, c_len), 1)
    tril_ones = jnp.where(cr >= cc, 1.0, 0.0).astype(BF16)

    def expand(x):
        xb = x.astype(BF16)
        return jnp.where(head_match, jnp.concatenate([xb] * heads, axis=0), jnp.zeros((), BF16))

    lane = lax.broadcasted_iota(jnp.int32, (c_len, V7X_LANES), 1)
    first_head = lane < RWKV_HEAD_DIM

    def seg_sum(x):
        out = []
        for col in range(w // V7X_LANES):
            xc = x[:, col * V7X_LANES:(col + 1) * V7X_LANES]
            s0 = jnp.sum(jnp.where(first_head, xc, 0.0), axis=-1, keepdims=True)
            s1 = jnp.sum(jnp.where(first_head, 0.0, xc), axis=-1, keepdims=True)
            out.append(jnp.where(first_head, s0, s1))
        return jnp.concatenate(out, axis=1)

    def chunk(rows, grp):
        lanes = slice(grp * w, (grp + 1) * w)
        r, k, v = r_ref[rows, lanes], k_ref[rows, lanes], v_ref[rows, lanes]
        lw, a, g = lw_ref[rows, lanes], a_ref[rows, lanes], g_ref[rows, lanes]
        k_k, k_a, r_k = kk_ref[:, lanes], ka_ref[:, lanes], rk_ref[:, lanes]
        ln_g, ln_b = lng_ref[:, lanes], lnb_ref[:, lanes]

        kk = k * k_k
        kk_sq = seg_sum(kk * kk)
        kp = k * (1.0 + (a - 1.0) * k_a)
        rk_sum = seg_sum(r * kp * r_k)
        cum = _split_dot_left(tril_ones, lw, 3)
        yield
        kk = kk / jnp.maximum(jnp.sqrt(kk_sq), 1e-12)
        bonus = rk_sum * v
        cum_end = cum[c_len - 1:c_len, :]
        e_pos, e_neg = jnp.exp(cum), jnp.exp(-cum)
        e_tail = jnp.exp(cum_end - cum)
        ba = kk * a
        r_t = (r * e_pos).astype(BF16)
        a_t = (-kk * jnp.exp(cum - lw)).astype(BF16)
        b_t, k_t = ba * e_neg, kp * e_neg
        bk_h = jnp.concatenate([ba * e_tail, kp * e_tail], axis=0).astype(BF16)
        vb = v.astype(BF16)

        gram = _dot_nt(jnp.concatenate([a_t, r_t], axis=0),
                       jnp.concatenate([expand(b_t), expand(k_t)], axis=0))
        yield
        ab = jnp.where(strict, gram[:c_len, :n], 0.0)
        ak = jnp.where(strict, gram[:c_len, n:], 0.0).astype(BF16)
        rbm = jnp.where(incl, gram[c_len:, :n], 0.0).astype(BF16)
        rkm = jnp.where(incl, gram[c_len:, n:], 0.0).astype(BF16)

        tinv = eye + ab
        p = _dot(ab.astype(BF16), expand(ab))
        akv_rkv = _dot(jnp.concatenate([ak, rkm], axis=0), expand(v))
        yield
        for _ in range(c_len.bit_length() - 3):
            both = _dot(jnp.concatenate([p, tinv], axis=0).astype(BF16), expand(p))
            yield
            p, tinv = both[:c_len], tinv + both[c_len:]
        last = _dot(tinv.astype(BF16), expand(p))
        state = state_ref[grp]
        ag_rg = _dot_nt(jnp.concatenate([a_t, r_t], axis=0), state.astype(BF16))
        yield
        tb = (tinv + last).astype(BF16)
        u = _dot(tb, expand(akv_rkv[:c_len] + ag_rg[:c_len]))
        yield
        o = _dot(rbm, expand(u))
        upd = _dot_tn(jnp.concatenate([u.astype(BF16), vb], axis=0), bk_h)
        yield
        o = ag_rg[c_len:] + o + akv_rkv[c_len:]
        state_ref[grp] = state * jnp.exp(cum_end) + jnp.where(
            (rb // RWKV_HEAD_DIM) == (lb // RWKV_HEAD_DIM), upd, 0.0)
        mean = seg_sum(o) * (1.0 / RWKV_HEAD_DIM)
        yield
        cen = o - mean
        var = seg_sum(cen * cen) * (1.0 / RWKV_HEAD_DIM)
        yield
        y = (cen * lax.rsqrt(var + RWKV_GN_EPS)) * ln_g + ln_b
        o_ref[rows, lanes] = ((y + bonus) * g).astype(BF16)

    def body(c, carry):
        rows = pl.ds(pl.multiple_of(c * c_len, c_len), c_len)
        pending = [chunk(rows, grp) for grp in range(state_ref.shape[0])]
        while pending:
            pending = [gen for gen in pending if next(gen, True) is None]
        return carry

    lax.fori_loop(0, r_ref.shape[0] // c_len, body, 0)


def _rwkv_core(rkv, lw, a, g, k_k, k_a, r_k, ln_g, ln_b, batch, rows=256, groups=8):
    t, d = lw.shape
    s = t // batch
    rows = min(rows, s)
    w = RWKV_LANES * groups
    nq = d // w
    spb = s // rows
    seq = lambda off: pl.BlockSpec((rows, w), lambda b, q, i: (b * spb + i, off + q))
    par = pl.BlockSpec((1, w), lambda b, q, i: (0, q))
    vec = lambda x: x.reshape(1, d)
    return pl.pallas_call(
        _rwkv_core_kernel,
        grid=(batch, nq, spb),
        in_specs=[seq(0), seq(nq), seq(2 * nq), seq(0), seq(0), seq(0), par, par, par, par, par],
        out_specs=seq(0),
        out_shape=jax.ShapeDtypeStruct((t, d), BF16),
        scratch_shapes=[pltpu.VMEM((groups, RWKV_LANES, RWKV_LANES), F32)],
        compiler_params=_params(3),
        name="rwkv_core",
    )(rkv, rkv, rkv, lw, a, g, vec(k_k), vec(k_a), vec(r_k), vec(ln_g), vec(ln_b))


def _pad_cols(x, n):
    return jnp.pad(x, [(0, 0)] * (x.ndim - 1) + [(0, n - x.shape[-1])])


def _pad_rows(x, n):
    return jnp.pad(x, [(0, n - x.shape[0])] + [(0, 0)] * (x.ndim - 1))


def _round_up(x, m):
    return -(-x // m) * m


def _cast_pad_halves_kernel(x_ref, o_ref, *, half, padded):
    zeros = jnp.zeros((o_ref.shape[0], padded - half), o_ref.dtype)
    for k in range(2):
        o_ref[:, k * padded:k * padded + half] = x_ref[:, k * half:(k + 1) * half].astype(o_ref.dtype)
        o_ref[:, k * padded + half:(k + 1) * padded] = zeros


def _cast_pad_halves(x, half, padded, rows=128):
    n, r, _ = x.shape
    return pl.pallas_call(
        functools.partial(_cast_pad_halves_kernel, half=half, padded=padded),
        grid=(n, r // rows),
        in_specs=[pl.BlockSpec((None, rows, 2 * half), lambda l, i: (l, i, 0))],
        out_specs=pl.BlockSpec((None, rows, 2 * padded), lambda l, i: (l, i, 0)),
        out_shape=jax.ShapeDtypeStruct((n, r, 2 * padded), BF16),
        compiler_params=_params(2),
        name="cast_pad_halves",
    )(x)


def _cast_pad_rows_kernel(x_ref, o_ref):
    rows = x_ref.shape[0]
    o_ref[0:rows, :] = x_ref[...].astype(o_ref.dtype)
    o_ref[rows:, :] = jnp.zeros((o_ref.shape[0] - rows, o_ref.shape[1]), o_ref.dtype)


def _cast_pad_rows(x, padded, cols=256):
    n, r, c = x.shape
    return pl.pallas_call(
        _cast_pad_rows_kernel,
        grid=(n, c // cols),
        in_specs=[pl.BlockSpec((None, r, cols), lambda l, j: (l, 0, j))],
        out_specs=pl.BlockSpec((None, padded, cols), lambda l, j: (l, 0, j)),
        out_shape=jax.ShapeDtypeStruct((n, padded, c), BF16),
        compiler_params=_params(2),
        name="cast_pad_rows",
    )(x)


def _pad_val_gate(x, d_ff, fp):
    return jnp.concatenate([_pad_cols(x[..., :d_ff], fp), _pad_cols(x[..., d_ff:], fp)], axis=-1)


def kernel(x, c, positions, ada_w, ada_b, ffn_w_up, ffn_conv_w, ffn_conv_b, ffn_w_down, sg_w_in, sg_ln_g, sg_ln_b, sg_w_s, sg_b_s, sg_w_out, ret_w_in, ret_gn_g, ret_gn_b, ret_w_out, rwkv_mu, rwkv_w_rkv, rwkv_w0, rwkv_w1, rwkv_w2, rwkv_a0, rwkv_a1, rwkv_a2, rwkv_g1, rwkv_g2, rwkv_k_k, rwkv_k_a, rwkv_r_k, rwkv_ln_g, rwkv_ln_b, rwkv_w_out, final_norm_g):
    batch, seq, d = x.shape
    depth = ada_w.shape[0]
    t = batch * seq
    d_ff = ffn_w_down.shape[1]
    fp = _round_up(d_ff, FFN_SUBCHUNKS * FFN_SUB)

    sg_w_out_b, ret_w_out_b = sg_w_out.astype(BF16), ret_w_out.astype(BF16)
    rwkv_w_rkv_b, rwkv_w_out_b = rwkv_w_rkv.astype(BF16), rwkv_w_out.astype(BF16)
    ffn_w_up_b = _cast_pad_halves(ffn_w_up, d_ff, fp)
    ffn_w_down_b = _cast_pad_rows(ffn_w_down, fp)
    ffn_conv_w_p = _pad_val_gate(ffn_conv_w, d_ff, fp)
    ffn_conv_b_p = _pad_val_gate(ffn_conv_b, d_ff, fp).reshape(depth, 1, 2 * fp)

    mod = _modulation(c, ada_w, ada_b)
    h = x.reshape(t, d)
    cos = sin = None

    for layer in range(depth):
        kind, j = layer % N_MIXERS, layer // N_MIXERS
        if kind == 0:
            z = _norm_matmul(h, mod, layer, 1, 0, sg_w_in, j, batch, n=sg_w_in.shape[-1], epilogue="gelu")
            h = _sg_core(z, h, mod, layer, batch, sg_ln_g[j], sg_ln_b[j], sg_w_s[j], sg_b_s[j],
                         sg_w_out_b, j)
        elif kind == 1:
            if cos is None:
                cos, sin = _rope_tables(positions, d // RET_HEADS // 2)
            qk = _norm_matmul(h, mod, layer, 1, 0, ret_w_in, j, batch, n=2 * d, epilogue="rope",
                              cos=cos, sin=sin)
            v = _norm_matmul(h, mod, layer, 1, 0, ret_w_in, j, batch, n=2 * d, col_offset=2 * d,
                             out_dtype=BF16)
            g = _norm_matmul(h, mod, layer, 1, 0, ret_w_in, j, batch, n=2 * d, col_offset=4 * d)
            o = _ret_core(qk, v, g, ret_gn_g[j], ret_gn_b[j], batch)
            h = _proj_residual(o, ret_w_out_b, j, h, mod, layer, 2, batch)
        else:
            rkv = _rwkv_rkv(h, mod, layer, batch, rwkv_mu[j], rwkv_w_rkv_b, j)
            lora = _round_up(rwkv_w1.shape[-1], V7X_LANES)
            lora_a = _round_up(rwkv_a1.shape[-1], V7X_LANES)
            lw, a, g = _rwkv_lora(
                h, mod, layer, batch, rwkv_mu[j], rwkv_w0[j],
                _pad_cols(rwkv_w1[j], lora).astype(BF16), _pad_rows(rwkv_w2[j], lora).astype(BF16),
                rwkv_a0[j],
                _pad_cols(rwkv_a1[j], lora_a).astype(BF16), _pad_rows(rwkv_a2[j], lora_a).astype(BF16),
                rwkv_g1[j].astype(BF16), rwkv_g2[j].astype(BF16))
            o = _rwkv_core(rkv, lw, a, g, rwkv_k_k[j], rwkv_k_a[j], rwkv_r_k[j],
                           rwkv_ln_g[j], rwkv_ln_b[j], batch)
            h = _proj_residual(o, rwkv_w_out_b, j, h, mod, layer, 2, batch)

        h = _conv_ffn(h, mod, layer, batch, ffn_w_up_b, ffn_conv_w_p, ffn_conv_b_p, ffn_w_down_b,
                      final_gain=final_norm_g if layer == depth - 1 else None)

    return h.reshape(batch, seq, d)
```

```python
import functools
import math

import jax
import jax.numpy as jnp
from jax import lax
from jax.experimental import pallas as pl
from jax.experimental.pallas import tpu as pltpu

F32 = jnp.float32
BF16 = jnp.bfloat16

NORM_EPS = 1e-6
LN_EPS = 1e-5
N_MOD = 6
N_MIXERS = 3

SG_CHUNK = 128
SG_GROUPS = 16

RET_HEADS = 8
RET_CHUNK = 128
ROPE_BASE = 10000.0

RWKV_HEAD_DIM = 64
RWKV_GN_EPS = RWKV_HEAD_DIM * 1e-5
RWKV_DECAY_OFFSET = 0.5
RWKV_CHUNK = 64
RWKV_LANES = 256

CONV_WIDTH = 3
FFN_SUB = 256
FFN_SUBCHUNKS = 2
MATMUL_SUBTILE = 512
WEIGHT_BLOCK_ELEMS = 2048 * 1024

V7X_LANES = 128
V7X_SUBLANES = 8
V7X_VMEM_LIMIT = 56 * 1024 * 1024


def _params(n_axes, vmem=V7X_VMEM_LIMIT, independent_rows=False):
    lead = "parallel" if independent_rows else "arbitrary"
    return pltpu.CompilerParams(dimension_semantics=(lead,) + ("arbitrary",) * (n_axes - 1),
                                vmem_limit_bytes=vmem)


def _dot(a, b):
    return jnp.dot(a, b, preferred_element_type=F32)


def _dot_nt(a, b):
    return lax.dot_general(a, b, (((1,), (1,)), ((), ())), preferred_element_type=F32)


def _dot_tn(a, b):
    return lax.dot_general(a, b, (((0,), (0,)), ((), ())), preferred_element_type=F32)


def _rms_mod(h, sc, sh):
    ms = jnp.mean(h * h, axis=-1, keepdims=True)
    return (h * lax.rsqrt(ms + NORM_EPS)) * (1.0 + sc) + sh


def _mod_kernel(c_ref, w_ref, b_ref, o_ref):
    cond = jax.nn.silu(c_ref[...])
    o_ref[...] = _dot(cond.astype(BF16), w_ref[...].astype(BF16)) + b_ref[...]


def _modulation(c, ada_w, ada_b, tn=2048):
    depth, d, n = ada_w.shape
    b = c.shape[0]
    rows = -(-b // V7X_SUBLANES) * V7X_SUBLANES
    c_pad = jnp.pad(c, ((0, rows - b), (0, 0)))
    out = pl.pallas_call(
        _mod_kernel,
        grid=(depth, n // tn),
        in_specs=[
            pl.BlockSpec((rows, d), lambda l, j: (0, 0)),
            pl.BlockSpec((None, d, tn), lambda l, j: (l, 0, j)),
            pl.BlockSpec((None, 1, tn), lambda l, j: (l, 0, j)),
        ],
        out_specs=pl.BlockSpec((None, rows, tn), lambda l, j: (l, 0, j)),
        out_shape=jax.ShapeDtypeStruct((depth, rows, n), F32),
        compiler_params=_params(2),
        name="adaln_mod",
    )(c_pad, ada_w, ada_b.reshape(depth, 1, n))
    return out[:, :b].reshape(depth * b * N_MOD, 1, d)


def _mod_spec(d, layer, k, batch, tiles_per_batch):
    base = layer * batch * N_MOD + k
    return pl.BlockSpec((None, 1, d),
                        lambda i, *_: (base + (i // tiles_per_batch) * N_MOD, 0, 0))


def _norm_matmul_kernel(h_ref, sc_ref, sh_ref, w_ref, *rest, epilogue):
    if epilogue == "rope":
        cos_ref, sin_ref, o_ref, xm_ref = rest
    else:
        o_ref, xm_ref = rest

    @pl.when(pl.program_id(1) == 0)
    def _():
        xm_ref[...] = _rms_mod(h_ref[...], sc_ref[...], sh_ref[...]).astype(xm_ref.dtype)

    xm = xm_ref[...]
    tn = w_ref.shape[1]
    sub = min(tn, MATMUL_SUBTILE)
    for c in range(tn // sub):
        cols = slice(c * sub, (c + 1) * sub)
        y = _dot(xm, w_ref[:, cols])
        if epilogue == "gelu":
            y = jax.nn.gelu(y)
        elif epilogue == "rope":
            cos, sin = cos_ref[...], sin_ref[...]
            half = cos.shape[-1]
            parts = []
            for hd in range(sub // (2 * half)):
                x1 = y[:, 2 * hd * half:(2 * hd + 1) * half]
                x2 = y[:, (2 * hd + 1) * half:(2 * hd + 2) * half]
                parts += [x1 * cos - x2 * sin, x2 * cos + x1 * sin]
            y = jnp.concatenate(parts, axis=-1)
        o_ref[:, cols] = y.astype(o_ref.dtype)


def _norm_matmul(h, mod, layer, k_scale, k_shift, w, w_idx, batch, *, n, col_offset=0, epilogue="none",
                 out_dtype=F32, cos=None, sin=None, tm=1024, tn=1024):
    t, d = h.shape
    tm = min(tm, t // batch)
    tpb = (t // batch) // tm
    off = col_offset // tn
    in_specs = [
        pl.BlockSpec((tm, d), lambda i, j: (i, 0)),
        _mod_spec(d, layer, k_scale, batch, tpb),
        _mod_spec(d, layer, k_shift, batch, tpb),
        pl.BlockSpec((None, d, tn), lambda i, j: (w_idx, 0, off + j)),
    ]
    args = [h, mod, mod, w]
    if epilogue == "rope":
        half = cos.shape[-1]
        in_specs += [pl.BlockSpec((tm, half), lambda i, j: (i, 0))] * 2
        args += [cos, sin]
    return pl.pallas_call(
        functools.partial(_norm_matmul_kernel, epilogue=epilogue),
        grid=(t // tm, n // tn),
        in_specs=in_specs,
        out_specs=pl.BlockSpec((tm, tn), lambda i, j: (i, j)),
        out_shape=jax.ShapeDtypeStruct((t, n), out_dtype),
        scratch_shapes=[pltpu.VMEM((tm, d), w.dtype)],
        compiler_params=_params(2, independent_rows=True),
        name="norm_matmul_" + epilogue,
    )(*args)


def _proj_residual_kernel(a_ref, w_ref, h_ref, g_ref, o_ref):
    a = a_ref[...]
    tn = w_ref.shape[1]
    sub = min(tn, MATMUL_SUBTILE)
    for c in range(tn // sub):
        cols = slice(c * sub, (c + 1) * sub)
        o_ref[:, cols] = h_ref[:, cols] + g_ref[:, cols] * _dot(a, w_ref[:, cols])


def _proj_residual(a, w, w_idx, h, mod, layer, k_gate, batch, tm=1024, tn=1024):
    t, kdim = a.shape
    d = w.shape[-1]
    tn = min(tn, WEIGHT_BLOCK_ELEMS // kdim)
    tm = min(tm, t // batch)
    tpb = (t // batch) // tm
    base = layer * batch * N_MOD + k_gate
    return pl.pallas_call(
        _proj_residual_kernel,
        grid=(t // tm, d // tn),
        in_specs=[
            pl.BlockSpec((tm, kdim), lambda i, j: (i, 0)),
            pl.BlockSpec((None, kdim, tn), lambda i, j: (w_idx, 0, j)),
            pl.BlockSpec((tm, tn), lambda i, j: (i, j)),
            pl.BlockSpec((None, 1, tn), lambda i, j: (base + (i // tpb) * N_MOD, 0, j)),
        ],
        out_specs=pl.BlockSpec((tm, tn), lambda i, j: (i, j)),
        out_shape=jax.ShapeDtypeStruct((t, d), F32),
        compiler_params=_params(2, independent_rows=True),
        name="proj_residual",
    )(a, w, h, mod)


def _sg_core_kernel(u_ref, v_ref, h_ref, g_ref, lng_ref, lnb_ref, ws_ref, bs_ref, wo_ref,
                    o_ref, gated_ref):
    tm, width = v_ref.shape
    gdim = width // SG_GROUPS
    v = v_ref[...]
    mu = jnp.mean(v, axis=-1, keepdims=True)
    var = jnp.mean(jnp.square(v - mu), axis=-1, keepdims=True)
    vn = (((v - mu) * lax.rsqrt(var + LN_EPS)) * lng_ref[...] + lnb_ref[...]).astype(BF16)
    row = lax.broadcasted_iota(jnp.int32, (SG_CHUNK, SG_CHUNK), 0)
    col = lax.broadcasted_iota(jnp.int32, (SG_CHUNK, SG_CHUNK), 1)
    causal = row >= col
    for g in range(SG_GROUPS):
        w_causal = jnp.where(causal, ws_ref[g], 0.0).astype(BF16)
        bias = bs_ref[:, g:g + 1]
        cols = slice(g * gdim, (g + 1) * gdim)
        for c in range(tm // SG_CHUNK):
            rows = slice(c * SG_CHUNK, (c + 1) * SG_CHUNK)
            sv = _dot(w_causal, vn[rows, cols]) + bias
            gated_ref[rows, cols] = (u_ref[rows, cols] * sv).astype(BF16)
    o_ref[...] = h_ref[...] + g_ref[...] * _dot(gated_ref[...], wo_ref[...])


def _sg_core(z, h, mod, layer, batch, ln_g, ln_b, w_s, b_s, w_out, w_idx, tm=256):
    t, d = h.shape
    width = z.shape[1] // 2
    tm = min(tm, t // batch)
    tpb = (t // batch) // tm
    return pl.pallas_call(
        _sg_core_kernel,
        grid=(t // tm,),
        in_specs=[
            pl.BlockSpec((tm, width), lambda i: (i, 0)),
            pl.BlockSpec((tm, width), lambda i: (i, 1)),
            pl.BlockSpec((tm, d), lambda i: (i, 0)),
            _mod_spec(d, layer, 2, batch, tpb),
            pl.BlockSpec((1, width), lambda i: (0, 0)),
            pl.BlockSpec((1, width), lambda i: (0, 0)),
            pl.BlockSpec((SG_GROUPS, SG_CHUNK, SG_CHUNK), lambda i: (0, 0, 0)),
            pl.BlockSpec((SG_CHUNK, SG_GROUPS), lambda i: (0, 0)),
            pl.BlockSpec((None, width, d), lambda i: (w_idx, 0, 0)),
        ],
        out_specs=pl.BlockSpec((tm, d), lambda i: (i, 0)),
        out_shape=jax.ShapeDtypeStruct((t, d), F32),
        scratch_shapes=[pltpu.VMEM((tm, width), BF16)],
        compiler_params=_params(1),
        name="sg_core",
    )(z, z, h, mod, ln_g.reshape(1, width), ln_b.reshape(1, width), w_s, b_s.T, w_out)


def _ffn_kernel(h_ref, sc_ref, sh_ref, g_ref, wv_ref, wg_ref, cwv_ref, cwg_ref, cbv_ref, cbg_ref,
                wd_ref, *rest, tiles_per_batch, n_tiles, final_norm):
    if final_norm:
        fg_ref, o_ref, xm_ref, halo_ref = rest[:4]
    else:
        o_ref, xm_ref, halo_ref = rest[:3]
    hbufs = rest[-FFN_SUBCHUNKS:]
    i, j = pl.program_id(0), pl.program_id(1)
    tm = h_ref.shape[0]
    pad = V7X_SUBLANES
    sub = FFN_SUB

    @pl.when(j == 0)
    def _():
        xm_ref[...] = _rms_mod(h_ref[...], sc_ref[...], sh_ref[...]).astype(BF16)
        o_ref[...] = jnp.zeros_like(o_ref)

    seq_start = (i % tiles_per_batch) == 0
    xm = xm_ref[...]
    for k, hb in enumerate(hbufs):
        cols = slice(k * sub, (k + 1) * sub)
        hb[0:pad, :] = jnp.where(seq_start, 0.0, halo_ref[j, k])
        hb[pad:, 0:sub] = _dot(xm, wv_ref[:, cols])
        hb[pad:, sub:] = _dot(xm, wg_ref[:, cols])
        halo_ref[j, k] = hb[tm:, :]

    def conv(hb, lanes, cw, cb):
        acc = hb[pad - 2:pad - 2 + tm, lanes] * cw[0:1, :]
        acc = acc + hb[pad - 1:pad - 1 + tm, lanes] * cw[1:2, :]
        acc = acc + hb[pad:pad + tm, lanes] * cw[2:3, :]
        return acc + cb

    for k, hb in enumerate(hbufs):
        cols = slice(k * sub, (k + 1) * sub)
        val = conv(hb, slice(0, sub), cwv_ref[:, cols], cbv_ref[:, cols])
        gate = conv(hb, slice(sub, 2 * sub), cwg_ref[:, cols], cbg_ref[:, cols])
        act = (jax.nn.silu(gate) * val).astype(BF16)
        o_ref[...] += _dot(act, wd_ref[cols, :])

    @pl.when(j == n_tiles - 1)
    def _():
        out = h_ref[...] + g_ref[...] * o_ref[...]
        if final_norm:
            ms = jnp.mean(out * out, axis=-1, keepdims=True)
            out = (out * lax.rsqrt(ms + NORM_EPS)) * fg_ref[...]
        o_ref[...] = out


def _conv_ffn(h, mod, layer, batch, w_up, conv_w, conv_b, w_down, final_gain=None, tm=512):
    t, d = h.shape
    fp = w_down.shape[1]
    tf = FFN_SUBCHUNKS * FFN_SUB
    nf = fp // tf
    tm = min(tm, t // batch)
    tpb = (t // batch) // tm
    final_norm = final_gain is not None
    in_specs = [
        pl.BlockSpec((tm, d), lambda i, j: (i, 0)),
        _mod_spec(d, layer, 4, batch, tpb),
        _mod_spec(d, layer, 3, batch, tpb),
        _mod_spec(d, layer, 5, batch, tpb),
        pl.BlockSpec((None, d, tf), lambda i, j: (layer, 0, j)),
        pl.BlockSpec((None, d, tf), lambda i, j: (layer, 0, nf + j)),
        pl.BlockSpec((None, CONV_WIDTH, tf), lambda i, j: (layer, 0, j)),
        pl.BlockSpec((None, CONV_WIDTH, tf), lambda i, j: (layer, 0, nf + j)),
        pl.BlockSpec((None, 1, tf), lambda i, j: (layer, 0, j)),
        pl.BlockSpec((None, 1, tf), lambda i, j: (layer, 0, nf + j)),
        pl.BlockSpec((None, tf, d), lambda i, j: (layer, j, 0)),
    ]
    args = [h, mod, mod, mod, w_up, w_up, conv_w, conv_w, conv_b, conv_b, w_down]
    if final_norm:
        in_specs.append(pl.BlockSpec((1, d), lambda i, j: (0, 0)))
        args.append(final_gain.reshape(1, d))
    return pl.pallas_call(
        functools.partial(_ffn_kernel, tiles_per_batch=tpb, n_tiles=nf, final_norm=final_norm),
        grid=(t // tm, nf),
        in_specs=in_specs,
        out_specs=pl.BlockSpec((tm, d), lambda i, j: (i, 0)),
        out_shape=jax.ShapeDtypeStruct((t, d), F32),
        scratch_shapes=[
            pltpu.VMEM((tm, d), BF16),
            pltpu.VMEM((nf, FFN_SUBCHUNKS, V7X_SUBLANES, 2 * FFN_SUB), F32),
        ] + [pltpu.VMEM((tm + V7X_SUBLANES, 2 * FFN_SUB), F32)] * FFN_SUBCHUNKS,
        compiler_params=_params(2),
        name="conv_ffn",
    )(*args)


def _rope_table_kernel(pos_ref, freq_ref, cos_ref, sin_ref):
    ang = pos_ref[...].astype(F32) * freq_ref[...]
    cos_ref[...] = jnp.cos(ang)
    sin_ref[...] = jnp.sin(ang)


def _rope_tables(positions, half, tm=1024):
    t = positions.size
    inv_freq = ROPE_BASE ** (-jnp.arange(half, dtype=F32) / half)
    tm = min(tm, t)
    return pl.pallas_call(
        _rope_table_kernel,
        grid=(t // tm,),
        in_specs=[pl.BlockSpec((tm, 1), lambda i: (i, 0)),
                  pl.BlockSpec((1, half), lambda i: (0, 0))],
        out_specs=[pl.BlockSpec((tm, half), lambda i: (i, 0))] * 2,
        out_shape=[jax.ShapeDtypeStruct((t, half), F32)] * 2,
        compiler_params=_params(1),
        name="rope_tables",
    )(positions.reshape(t, 1), inv_freq.reshape(1, half))


def _ret_core_kernel(q_ref, k_ref, v_ref, g_ref, di_ref, qd_ref, kd_ref, cd_ref, gg_ref, gb_ref,
                     o_ref, state_ref, *, k_scale):
    n_heads, dk, dv = state_ref.shape

    @pl.when(pl.program_id(2) == 0)
    def _():
        state_ref[...] = jnp.zeros_like(state_ref)

    def chunk(rows, hd):
        kcols = slice(hd * dk, (hd + 1) * dk)
        vcols = slice(hd * dv, (hd + 1) * dv)
        q = q_ref[rows, kcols]
        k = k_ref[rows, kcols] * k_scale
        v = v_ref[rows, vcols]
        state = state_ref[hd]
        inner = _dot_nt(q.astype(BF16), k.astype(BF16))
        cross = _dot((q * qd_ref[hd]).astype(BF16), state.astype(BF16))
        update = _dot_tn((k * kd_ref[hd]).astype(BF16), v)
        yield
        state_ref[hd] = state * cd_ref[hd] + update
        out = _dot((inner * di_ref[hd]).astype(BF16), v)
        yield
        out = out + cross
        mu = jnp.mean(out, axis=-1, keepdims=True)
        var = jnp.mean(jnp.square(out - mu), axis=-1, keepdims=True)
        y = ((out - mu) * lax.rsqrt(var + NORM_EPS)) * gg_ref[:, vcols] + gb_ref[:, vcols]
        o_ref[rows, vcols] = (jax.nn.silu(g_ref[rows, vcols]) * y).astype(BF16)

    def body(c, carry):
        rows = pl.ds(pl.multiple_of(c * RET_CHUNK, RET_CHUNK), RET_CHUNK)
        pending = [chunk(rows, hd) for hd in range(n_heads)]
        while pending:
            pending = [gen for gen in pending if next(gen, True) is None]
        return carry

    lax.fori_loop(0, q_ref.shape[0] // RET_CHUNK, body, 0)


def _ret_core(qk, v, g, gn_g, gn_b, batch, rows=256, group=8):
    t, two_d = qk.shape
    d = two_d // 2
    s = t // batch
    rows = min(rows, s)
    spb = s // rows
    n_groups = RET_HEADS // group
    dk = d // RET_HEADS
    dv = v.shape[1] // RET_HEADS
    log_gamma = jnp.log1p(-jnp.exp2(-5.0 - jnp.arange(RET_HEADS, dtype=F32)))
    idx = jnp.arange(RET_CHUNK, dtype=F32)
    rel = idx[:, None] - idx[None, :]
    decay_inner = jnp.where(rel >= 0, jnp.exp(log_gamma[:, None, None] * jnp.maximum(rel, 0.0)), 0.0)
    q_decay = jnp.exp(log_gamma[:, None] * (idx + 1.0))[..., None]
    k_decay = jnp.exp(log_gamma[:, None] * (RET_CHUNK - 1.0 - idx))[..., None]
    chunk_decay = jnp.exp(log_gamma * RET_CHUNK)[:, None, None]
    return pl.pallas_call(
        functools.partial(_ret_core_kernel, k_scale=dk ** -0.5),
        grid=(batch, n_groups, spb),
        in_specs=[
            pl.BlockSpec((rows, group * dk), lambda b, h, i: (b * spb + i, h)),
            pl.BlockSpec((rows, group * dk), lambda b, h, i: (b * spb + i, n_groups + h)),
            pl.BlockSpec((rows, group * dv), lambda b, h, i: (b * spb + i, h)),
            pl.BlockSpec((rows, group * dv), lambda b, h, i: (b * spb + i, h)),
            pl.BlockSpec((group, RET_CHUNK, RET_CHUNK), lambda b, h, i: (h, 0, 0)),
            pl.BlockSpec((group, RET_CHUNK, 1), lambda b, h, i: (h, 0, 0)),
            pl.BlockSpec((group, RET_CHUNK, 1), lambda b, h, i: (h, 0, 0)),
            pl.BlockSpec((group, 1, 1), lambda b, h, i: (h, 0, 0)),
            pl.BlockSpec((1, group * dv), lambda b, h, i: (0, h)),
            pl.BlockSpec((1, group * dv), lambda b, h, i: (0, h)),
        ],
        out_specs=pl.BlockSpec((rows, group * dv), lambda b, h, i: (b * spb + i, h)),
        out_shape=jax.ShapeDtypeStruct((t, v.shape[1]), BF16),
        scratch_shapes=[pltpu.VMEM((group, dk, dv), F32)],
        compiler_params=_params(3),
        name="ret_core",
    )(qk, qk, v, g, decay_inner, q_decay, k_decay, chunk_decay,
      gn_g.reshape(1, -1), gn_b.reshape(1, -1))


def _shifted_inputs(h_ref, hprev_ref, sc_ref, sh_ref, seq_start):
    sc, sh = sc_ref[...], sh_ref[...]
    xm = _rms_mod(h_ref[...], sc, sh)
    prev = _rms_mod(hprev_ref[...], sc, sh)
    last = prev[V7X_SUBLANES - 1:V7X_SUBLANES, :]
    last = jnp.where(seq_start, 0.0, last)
    row = lax.broadcasted_iota(jnp.int32, xm.shape, 0)
    shifted = jnp.where(row == 0, last, pltpu.roll(xm, 1, 0))
    return xm, shifted - xm


def _rwkv_rkv_kernel(h_ref, hprev_ref, sc_ref, sh_ref, mu_ref, w_ref, o_ref, xm_ref, dx_ref, xs_ref,
                     *, tiles_per_batch, tiles_per_proj):
    i, j = pl.program_id(0), pl.program_id(1)

    @pl.when(j == 0)
    def _():
        xm, dx = _shifted_inputs(h_ref, hprev_ref, sc_ref, sh_ref, (i % tiles_per_batch) == 0)
        xm_ref[...] = xm
        dx_ref[...] = dx

    @pl.when(j % tiles_per_proj == 0)
    def _():
        xs_ref[...] = (xm_ref[...] + dx_ref[...] * mu_ref[...]).astype(BF16)

    xs = xs_ref[...]
    tn = w_ref.shape[1]
    sub = min(tn, MATMUL_SUBTILE)
    for c in range(tn // sub):
        cols = slice(c * sub, (c + 1) * sub)
        o_ref[:, cols] = _dot(xs, w_ref[:, cols])


def _prev_rows_spec(tm, d):
    blocks = tm // V7X_SUBLANES
    return pl.BlockSpec((V7X_SUBLANES, d), lambda i, *_: (jnp.maximum(i * blocks - 1, 0), 0))


def _rwkv_rkv(h, mod, layer, batch, mu, w_rkv, w_idx, tm=512, tn=2048):
    t, d = h.shape
    tm = min(tm, t // batch)
    tpb = (t // batch) // tm
    tpp = d // tn
    return pl.pallas_call(
        functools.partial(_rwkv_rkv_kernel, tiles_per_batch=tpb, tiles_per_proj=tpp),
        grid=(t // tm, 3 * tpp),
        in_specs=[
            pl.BlockSpec((tm, d), lambda i, j: (i, 0)),
            _prev_rows_spec(tm, d),
            _mod_spec(d, layer, 1, batch, tpb),
            _mod_spec(d, layer, 0, batch, tpb),
            pl.BlockSpec((None, 1, d), lambda i, j: (j // tpp, 0, 0)),
            pl.BlockSpec((None, None, d, tn), lambda i, j: (w_idx, j // tpp, 0, j % tpp)),
        ],
        out_specs=pl.BlockSpec((tm, tn), lambda i, j: (i, j)),
        out_shape=jax.ShapeDtypeStruct((t, 3 * d), F32),
        scratch_shapes=[pltpu.VMEM((tm, d), F32), pltpu.VMEM((tm, d), F32), pltpu.VMEM((tm, d), BF16)],
        compiler_params=_params(2, independent_rows=True),
        name="rwkv_rkv",
    )(h, h, mod, mod, mu.reshape(-1, 1, d), w_rkv)


def _rwkv_lora_kernel(h_ref, hprev_ref, sc_ref, sh_ref, mu_ref, w0_ref, w1_ref, w2_ref,
                      a0_ref, a1_ref, a2_ref, g1_ref, g2_ref, lw_ref, a_ref, g_ref, *, tiles_per_batch):
    i = pl.program_id(0)
    xm, dx = _shifted_inputs(h_ref, hprev_ref, sc_ref, sh_ref, (i % tiles_per_batch) == 0)

    def mixed(p):
        return (xm + dx * mu_ref[p]).astype(BF16)

    lora_w = _dot(jnp.tanh(_dot(mixed(3), w1_ref[...])).astype(BF16), w2_ref[...])
    lw_ref[...] = -math.exp(-RWKV_DECAY_OFFSET) * jax.nn.sigmoid(w0_ref[...] + lora_w)
    lora_a = _dot(_dot(mixed(4), a1_ref[...]).astype(BF16), a2_ref[...])
    a_ref[...] = jax.nn.sigmoid(a0_ref[...] + lora_a)
    g_ref[...] = _dot(jax.nn.sigmoid(_dot(mixed(5), g1_ref[...])).astype(BF16), g2_ref[...])


def _rwkv_lora(h, mod, layer, batch, mu, w0, w1, w2, a0, a1, a2, g1, g2, tm=256):
    t, d = h.shape
    tm = min(tm, t // batch)
    tpb = (t // batch) // tm
    full = lambda a: pl.BlockSpec(a.shape, lambda i: (0,) * a.ndim)
    mu3 = mu.reshape(-1, 1, d)
    w0, a0 = w0.reshape(1, d), a0.reshape(1, d)
    consts = [mu3, w0, w1, w2, a0, a1, a2, g1, g2]
    return pl.pallas_call(
        functools.partial(_rwkv_lora_kernel, tiles_per_batch=tpb),
        grid=(t // tm,),
        in_specs=[
            pl.BlockSpec((tm, d), lambda i: (i, 0)),
            _prev_rows_spec(tm, d),
            _mod_spec(d, layer, 1, batch, tpb),
            _mod_spec(d, layer, 0, batch, tpb),
        ] + [full(a) for a in consts],
        out_specs=[pl.BlockSpec((tm, d), lambda i: (i, 0))] * 3,
        out_shape=[jax.ShapeDtypeStruct((t, d), F32)] * 3,
        compiler_params=_params(1),
        name="rwkv_lora",
    )(h, h, mod, mod, *consts)


def _split_dot(x, ones, terms):
    acc = None
    for _ in range(terms):
        piece = x.astype(BF16)
        part = _dot(piece, ones)
        acc = part if acc is None else acc + part
        x = x - piece.astype(F32)
    return acc


def _split_dot_left(ones, x, terms):
    acc = None
    for _ in range(terms):
        piece = x.astype(BF16)
        part = _dot(ones, piece)
        acc = part if acc is None else acc + part
        x = x - piece.astype(F32)
    return acc


def _rwkv_core_kernel(r_ref, k_ref, v_ref, lw_ref, a_ref, g_ref, kk_ref, ka_ref, rk_ref,
                      lng_ref, lnb_ref, o_ref, state_ref):
    c_len, w = RWKV_CHUNK, RWKV_LANES
    heads = w // RWKV_HEAD_DIM
    n = heads * c_len

    @pl.when(pl.program_id(2) == 0)
    def _():
        state_ref[...] = jnp.zeros_like(state_ref)

    rb = lax.broadcasted_iota(jnp.int32, (n, w), 0)
    lb = lax.broadcasted_iota(jnp.int32, (n, w), 1)
    head_match = (rb // c_len) == (lb // RWKV_HEAD_DIM)
    ones_bd = jnp.where((rb // RWKV_HEAD_DIM) == (lb // RWKV_HEAD_DIM), 1.0, 0.0).astype(BF16)
    tr = lax.broadcasted_iota(jnp.int32, (c_len, n), 0)
    ts = lax.broadcasted_iota(jnp.int32, (c_len, n), 1) % c_len
    strict = tr > ts
    incl = tr >= ts
    eye = jnp.where(tr == ts, 1.0, 0.0)
    cr = lax.broadcasted_iota(jnp.int32, (c_len, c_len), 0)
    cc = lax.broadcasted_iota(jnp.int32, (c_len, c_len), 1)
    tril_ones = jnp.where(cr >= cc, 1.0, 0.0).astype(BF16)

    def expand(x):
        xb = x.astype(BF16)
        return jnp.where(head_match, jnp.concatenate([xb] * heads, axis=0), jnp.zeros((), BF16))

    lane = lax.broadcasted_iota(jnp.int32, (c_len, V7X_LANES), 1)
    first_head = lane < RWKV_HEAD_DIM

    def seg_sum(x):
        out = []
        for col in range(w // V7X_LANES):
            xc = x[:, col * V7X_LANES:(col + 1) * V7X_LANES]
            s0 = jnp.sum(jnp.where(first_head, xc, 0.0), axis=-1, keepdims=True)
            s1 = jnp.sum(jnp.where(first_head, 0.0, xc), axis=-1, keepdims=True)
            out.append(jnp.where(first_head, s0, s1))
        return jnp.concatenate(out, axis=1)

    def chunk(rows, grp):
        lanes = slice(grp * w, (grp + 1) * w)
        r, k, v = r_ref[rows, lanes], k_ref[rows, lanes], v_ref[rows, lanes]
        lw, a, g = lw_ref[rows, lanes], a_ref[rows, lanes], g_ref[rows, lanes]
        k_k, k_a, r_k = kk_ref[:, lanes], ka_ref[:, lanes], rk_ref[:, lanes]
        ln_g, ln_b = lng_ref[:, lanes], lnb_ref[:, lanes]

        kk = k * k_k
        kk_sq = seg_sum(kk * kk)
        kp = k * (1.0 + (a - 1.0) * k_a)
        rk_sum = seg_sum(r * kp * r_k)
        cum = _split_dot_left(tril_ones, lw, 3)
        yield
        kk = kk / jnp.maximum(jnp.sqrt(kk_sq), 1e-12)
        bonus = rk_sum * v
        cum_end = cum[c_len - 1:c_len, :]
        e_pos, e_neg = jnp.exp(cum), jnp.exp(-cum)
        e_tail = jnp.exp(cum_end - cum)
        ba = kk * a
        r_t = (r * e_pos).astype(BF16)
        a_t = (-kk * jnp.exp(cum - lw)).astype(BF16)
        b_t, k_t = ba * e_neg, kp * e_neg
        bk_h = jnp.concatenate([ba * e_tail, kp * e_tail], axis=0).astype(BF16)
        vb = v.astype(BF16)

        gram = _dot_nt(jnp.concatenate([a_t, r_t], axis=0),
                       jnp.concatenate([expand(b_t), expand(k_t)], axis=0))
        yield
        ab = jnp.where(strict, gram[:c_len, :n], 0.0)
        ak = jnp.where(strict, gram[:c_len, n:], 0.0).astype(BF16)
        rbm = jnp.where(incl, gram[c_len:, :n], 0.0).astype(BF16)
        rkm = jnp.where(incl, gram[c_len:, n:], 0.0).astype(BF16)

        tinv = eye + ab
        p = _dot(ab.astype(BF16), expand(ab))
        akv_rkv = _dot(jnp.concatenate([ak, rkm], axis=0), expand(v))
        yield
        for _ in range(c_len.bit_length() - 3):
            both = _dot(jnp.concatenate([p, tinv], axis=0).astype(BF16), expand(p))
            yield
            p, tinv = both[:c_len], tinv + both[c_len:]
        last = _dot(tinv.astype(BF16), expand(p))
        state = state_ref[grp]
        ag_rg = _dot_nt(jnp.concatenate([a_t, r_t], axis=0), state.astype(BF16))
        yield
        tb = (tinv + last).astype(BF16)
        u = _dot(tb, expand(akv_rkv[:c_len] + ag_rg[:c_len]))
        yield
        o = _dot(rbm, expand(u))
        upd = _dot_tn(jnp.concatenate([u.astype(BF16), vb], axis=0), bk_h)
        yield
        o = ag_rg[c_len:] + o + akv_rkv[c_len:]
        state_ref[grp] = state * jnp.exp(cum_end) + jnp.where(
            (rb // RWKV_HEAD_DIM) == (lb // RWKV_HEAD_DIM), upd, 0.0)
        mean = seg_sum(o) * (1.0 / RWKV_HEAD_DIM)
        yield
        cen = o - mean
        var = seg_sum(cen * cen) * (1.0 / RWKV_HEAD_DIM)
        yield
        y = (cen * lax.rsqrt(var + RWKV_GN_EPS)) * ln_g + ln_b
        o_ref[rows, lanes] = ((y + bonus) * g).astype(BF16)

    def body(c, carry):
        rows = pl.ds(pl.multiple_of(c * c_len, c_len), c_len)
        pending = [chunk(rows, grp) for grp in range(state_ref.shape[0])]
        while pending:
            pending = [gen for gen in pending if next(gen, True) is None]
        return carry

    lax.fori_loop(0, r_ref.shape[0] // c_len, body, 0)


def _rwkv_core(rkv, lw, a, g, k_k, k_a, r_k, ln_g, ln_b, batch, rows=256, groups=8):
    t, d = lw.shape
    s = t // batch
    rows = min(rows, s)
    w = RWKV_LANES * groups
    nq = d // w
    spb = s // rows
    seq = lambda off: pl.BlockSpec((rows, w), lambda b, q, i: (b * spb + i, off + q))
    par = pl.BlockSpec((1, w), lambda b, q, i: (0, q))
    vec = lambda x: x.reshape(1, d)
    return pl.pallas_call(
        _rwkv_core_kernel,
        grid=(batch, nq, spb),
        in_specs=[seq(0), seq(nq), seq(2 * nq), seq(0), seq(0), seq(0), par, par, par, par, par],
        out_specs=seq(0),
        out_shape=jax.ShapeDtypeStruct((t, d), BF16),
        scratch_shapes=[pltpu.VMEM((groups, RWKV_LANES, RWKV_LANES), F32)],
        compiler_params=_params(3),
        name="rwkv_core",
    )(rkv, rkv, rkv, lw, a, g, vec(k_k), vec(k_a), vec(r_k), vec(ln_g), vec(ln_b))


def _pad_cols(x, n):
    return jnp.pad(x, [(0, 0)] * (x.ndim - 1) + [(0, n - x.shape[-1])])


def _pad_rows(x, n):
    return jnp.pad(x, [(0, n - x.shape[0])] + [(0, 0)] * (x.ndim - 1))


def _round_up(x, m):
    return -(-x // m) * m


def _cast_pad_halves_kernel(x_ref, o_ref, *, half, padded):
    zeros = jnp.zeros((o_ref.shape[0], padded - half), o_ref.dtype)
    for k in range(2):
        o_ref[:, k * padded:k * padded + half] = x_ref[:, k * half:(k + 1) * half].astype(o_ref.dtype)
        o_ref[:, k * padded + half:(k + 1) * padded] = zeros


def _cast_pad_halves(x, half, padded, rows=128):
    n, r, _ = x.shape
    return pl.pallas_call(
        functools.partial(_cast_pad_halves_kernel, half=half, padded=padded),
        grid=(n, r // rows),
        in_specs=[pl.BlockSpec((None, rows, 2 * half), lambda l, i: (l, i, 0))],
        out_specs=pl.BlockSpec((None, rows, 2 * padded), lambda l, i: (l, i, 0)),
        out_shape=jax.ShapeDtypeStruct((n, r, 2 * padded), BF16),
        compiler_params=_params(2),
        name="cast_pad_halves",
    )(x)


def _cast_pad_rows_kernel(x_ref, o_ref):
    rows = x_ref.shape[0]
    o_ref[0:rows, :] = x_ref[...].astype(o_ref.dtype)
    o_ref[rows:, :] = jnp.zeros((o_ref.shape[0] - rows, o_ref.shape[1]), o_ref.dtype)


def _cast_pad_rows(x, padded, cols=256):
    n, r, c = x.shape
    return pl.pallas_call(
        _cast_pad_rows_kernel,
        grid=(n, c // cols),
        in_specs=[pl.BlockSpec((None, r, cols), lambda l, j: (l, 0, j))],
        out_specs=pl.BlockSpec((None, padded, cols), lambda l, j: (l, 0, j)),
        out_shape=jax.ShapeDtypeStruct((n, padded, c), BF16),
        compiler_params=_params(2),
        name="cast_pad_rows",
    )(x)


def _pad_val_gate(x, d_ff, fp):
    return jnp.concatenate([_pad_cols(x[..., :d_ff], fp), _pad_cols(x[..., d_ff:], fp)], axis=-1)


def kernel(x, c, positions, ada_w, ada_b, ffn_w_up, ffn_conv_w, ffn_conv_b, ffn_w_down, sg_w_in, sg_ln_g, sg_ln_b, sg_w_s, sg_b_s, sg_w_out, ret_w_in, ret_gn_g, ret_gn_b, ret_w_out, rwkv_mu, rwkv_w_rkv, rwkv_w0, rwkv_w1, rwkv_w2, rwkv_a0, rwkv_a1, rwkv_a2, rwkv_g1, rwkv_g2, rwkv_k_k, rwkv_k_a, rwkv_r_k, rwkv_ln_g, rwkv_ln_b, rwkv_w_out, final_norm_g):
    batch, seq, d = x.shape
    depth = ada_w.shape[0]
    t = batch * seq
    d_ff = ffn_w_down.shape[1]
    fp = _round_up(d_ff, FFN_SUBCHUNKS * FFN_SUB)

    sg_w_out_b, ret_w_out_b = sg_w_out.astype(BF16), ret_w_out.astype(BF16)
    rwkv_w_rkv_b, rwkv_w_out_b = rwkv_w_rkv.astype(BF16), rwkv_w_out.astype(BF16)
    ffn_w_up_b = _cast_pad_halves(ffn_w_up, d_ff, fp)
    ffn_w_down_b = _cast_pad_rows(ffn_w_down, fp)
    ffn_conv_w_p = _pad_val_gate(ffn_conv_w, d_ff, fp)
    ffn_conv_b_p = _pad_val_gate(ffn_conv_b, d_ff, fp).reshape(depth, 1, 2 * fp)

    mod = _modulation(c, ada_w, ada_b)
    h = x.reshape(t, d)
    cos = sin = None

    for layer in range(depth):
        kind, j = layer % N_MIXERS, layer // N_MIXERS
        if kind == 0:
            z = _norm_matmul(h, mod, layer, 1, 0, sg_w_in, j, batch, n=sg_w_in.shape[-1], epilogue="gelu")
            h = _sg_core(z, h, mod, layer, batch, sg_ln_g[j], sg_ln_b[j], sg_w_s[j], sg_b_s[j],
                         sg_w_out_b, j)
        elif kind == 1:
            if cos is None:
                cos, sin = _rope_tables(positions, d // RET_HEADS // 2)
            qk = _norm_matmul(h, mod, layer, 1, 0, ret_w_in, j, batch, n=2 * d, epilogue="rope",
                              cos=cos, sin=sin)
            v = _norm_matmul(h, mod, layer, 1, 0, ret_w_in, j, batch, n=2 * d, col_offset=2 * d,
                             out_dtype=BF16)
            g = _norm_matmul(h, mod, layer, 1, 0, ret_w_in, j, batch, n=2 * d, col_offset=4 * d)
            o = _ret_core(qk, v, g, ret_gn_g[j], ret_gn_b[j], batch)
            h = _proj_residual(o, ret_w_out_b, j, h, mod, layer, 2, batch)
        else:
            rkv = _rwkv_rkv(h, mod, layer, batch, rwkv_mu[j], rwkv_w_rkv_b, j)
            lora = _round_up(rwkv_w1.shape[-1], V7X_LANES)
            lora_a = _round_up(rwkv_a1.shape[-1], V7X_LANES)
            lw, a, g = _rwkv_lora(
                h, mod, layer, batch, rwkv_mu[j], rwkv_w0[j],
                _pad_cols(rwkv_w1[j], lora).astype(BF16), _pad_rows(rwkv_w2[j], lora).astype(BF16),
                rwkv_a0[j],
                _pad_cols(rwkv_a1[j], lora_a).astype(BF16), _pad_rows(rwkv_a2[j], lora_a).astype(BF16),
                rwkv_g1[j].astype(BF16), rwkv_g2[j].astype(BF16))
            o = _rwkv_core(rkv, lw, a, g, rwkv_k_k[j], rwkv_k_a[j], rwkv_r_k[j],
                           rwkv_ln_g[j], rwkv_ln_b[j], batch)
            h = _proj_residual(o, rwkv_w_out_b, j, h, mod, layer, 2, batch)

        h = _conv_ffn(h, mod, layer, batch, ffn_w_up_b, ffn_conv_w_p, ffn_conv_b_p, ffn_w_down_b,
                      final_gain=final_norm_g if layer == depth - 1 else None)

    return h.reshape(batch, seq, d)
```
